```python
import math
import jax
import jax.numpy as jnp
from jax import lax
import numpy as np

D_MODEL = 1024
BATCH = 16
SEQ = 256
DEPTH = 2
DEC_BATCH = 2
DEC_SEQ = 4096
PAST_LEN = 256

GRID_W = 64
ROPE_BASE = 10000.0
EPS = 1e-6
Q_BLOCK = 128
H_M = 4
DK_M = 64
DV_M = 128
MLSTM_CHUNK = 64
F_BIAS = 3.0
H_A = 8
Q_RANK = 384
KV_RANK = 256
NOPE_A = 64
ROPE_A = 32
V_A = 64
H_D = 4
DK_D = 64
DV_D = 128
D_FF = 4 * D_MODEL
IN_SPLITS = (H_M * DK_M, H_M * DK_M, H_M * DV_M, H_M * DV_M, 4 * H_M,
             Q_RANK, KV_RANK, ROPE_A,
             H_D * 2 * DK_D, H_D * 2 * DK_D, H_D * DV_D,
             3 * D_MODEL)
IN_COLS = (2 * H_M * DK_M + 2 * H_M * DV_M + 4 * H_M + Q_RANK + KV_RANK + ROPE_A
           + 4 * H_D * DK_D + H_D * DV_D + 3 * D_MODEL)

kernel_name = "hybrid_mlstm_mla_diffattn_dit_step"


def _rms(x, g):
    xf = x.astype(jnp.float32)
    y = xf * lax.rsqrt(jnp.mean(xf * xf, axis=-1, keepdims=True) + EPS)
    return (y * g.astype(jnp.float32)).astype(x.dtype)


def _rev(a):
    return jnp.flip(a, axis=1)


def _axial_tables(n_tokens, dim):
    rows = n_tokens // GRID_W
    row = jnp.repeat(jnp.arange(rows, dtype=jnp.float32), GRID_W)
    col = jnp.tile(jnp.arange(GRID_W, dtype=jnp.float32), rows)
    quarter = dim // 4
    inv = ROPE_BASE ** (-jnp.arange(quarter, dtype=jnp.float32) / quarter)
    ang = jnp.concatenate([row[:, None] * inv, col[:, None] * inv], axis=-1)
    return jnp.cos(ang), jnp.sin(ang)


def _rope(x, cos, sin):
    half = x.shape[-1] // 2
    shape = (x.shape[1],) + (1,) * (x.ndim - 3) + (half,)
    c = cos.reshape(shape).astype(x.dtype)
    s = sin.reshape(shape).astype(x.dtype)
    x1, x2 = x[..., :half], x[..., half:]
    return jnp.concatenate([x1 * c - x2 * s, x1 * s + x2 * c], axis=-1)


def _attend(q, k, v, q2=None, k2=None, lam=None):
    B, S, H, d = q.shape
    nb = S // Q_BLOCK
    scale = d ** -0.5

    def blocks(a):
        return jnp.swapaxes(a.reshape((B, nb, Q_BLOCK) + a.shape[2:]), 0, 1)

    def probs(qb, kk):
        s = jnp.einsum("bqhd,bkhd->bhqk", qb, kk).astype(jnp.float32) * scale
        return jax.nn.softmax(s, axis=-1)

    def one(qbs):
        p = probs(qbs[0], k)
        if q2 is not None:
            p = p - lam * probs(qbs[1], k2)
        return jnp.einsum("bhqk,bkhe->bqhe", p.astype(v.dtype), v)

    qs = (blocks(q),) if q2 is None else (blocks(q), blocks(q2))
    o = lax.map(one, qs)
    return jnp.swapaxes(o, 0, 1).reshape(B, S, H, v.shape[-1])


def _mlstm_chunk(carry, xs):
    C, n, m = carry
    q, k, v, log_i, log_f = xs
    L = q.shape[1]
    b = jnp.cumsum(log_f, axis=1).transpose(0, 2, 1)
    li = log_i.transpose(0, 2, 1)
    causal = jnp.tril(jnp.ones((L, L), dtype=bool))
    dmat = jnp.where(causal, b[..., :, None] - b[..., None, :] + li[..., None, :], -jnp.inf)
    inter = b + m[..., None]
    m_t = jnp.maximum(inter, jnp.max(dmat, axis=-1))
    w_inter = jnp.exp(inter - m_t)
    a = jnp.exp(dmat - m_t[..., None]) * jnp.einsum("blhd,bshd->bhls", q, k)
    num = w_inter[..., None] * jnp.einsum("blhd,bhde->bhle", q, C) + jnp.einsum("bhls,bshe->bhle", a, v)
    den = w_inter * jnp.einsum("blhd,bhd->bhl", q, n) + jnp.sum(a, axis=-1)
    h = num / jnp.maximum(jnp.abs(den), jnp.exp(-m_t))[..., None]
    m_new = m_t[..., -1]
    w_old = jnp.exp(b[..., -1] + m - m_new)
    w_s = jnp.exp(b[..., -1:] - b + li - m_new[..., None])
    C_new = w_old[..., None, None] * C + jnp.einsum("bhs,bshd,bshe->bhde", w_s, k, v)
    n_new = w_old[..., None] * n + jnp.einsum("bhs,bshd->bhd", w_s, k)
    return (C_new, n_new, m_new), h.transpose(0, 2, 1, 3)


def _mlstm_scan(q, k, v, log_i, log_f, state):
    B, S = q.shape[:2]
    nc = S // MLSTM_CHUNK

    def chunks(a):
        return jnp.swapaxes(a.reshape((B, nc, MLSTM_CHUNK) + a.shape[2:]), 0, 1)

    state, h = lax.scan(_mlstm_chunk, state,
                        (chunks(q), chunks(k), chunks(v), chunks(log_i), chunks(log_f)))
    return jnp.swapaxes(h, 0, 1).reshape(B, S, H_M, DV_M), state


def _layer(x, cond, lp, lam_init, ctx):
    B, N, _ = x.shape
    dt = x.dtype
    f32 = jnp.float32
    latent = ctx is not None
    mod = jax.nn.silu(cond) @ lp["w_mod"] + lp["b_mod"]
    sh1, sc1, g1, sh2, sc2, g2 = jnp.split(mod[:, None, :], 6, axis=-1)
    h = _rms(x, lp["norm1_g"]) * (1.0 + sc1) + sh1
    z = h @ lp["w_in"]
    (mq, mk, mv, mo, mg, acq, ackv, akr, dq, dk, dv, gate_pre) = jnp.split(
        z, np.cumsum(IN_SPLITS)[:-1].tolist(), axis=-1)
    if latent:
        cos_a, sin_a = _axial_tables(N, ROPE_A)
        cos_d, sin_d = _axial_tables(N, DK_D)

    if latent:
        C0, n0, m0 = ctx[4].astype(f32), ctx[5].astype(f32), ctx[6].astype(f32)
    else:
        C0 = jnp.zeros((B, 2, H_M, DK_M, DV_M), f32)
        n0 = jnp.zeros((B, 2, H_M, DK_M), f32)
        m0 = jnp.zeros((B, 2, H_M), f32)
    q_m = mq.reshape(B, N, H_M, DK_M).astype(f32) * DK_M ** -0.5
    k_m = mk.reshape(B, N, H_M, DK_M).astype(f32)
    v_m = mv.reshape(B, N, H_M, DV_M).astype(f32)
    gp = (mg + lp["mlstm_gate_b"]).astype(f32).reshape(B, N, 2, 2, H_M)
    log_i = gp[:, :, 0]
    log_f = jax.nn.log_sigmoid(gp[:, :, 1])
    h_f, (Cf, nf, mf) = _mlstm_scan(q_m, k_m, v_m, log_i[:, :, 0], log_f[:, :, 0],
                                    (C0[:, 0], n0[:, 0], m0[:, 0]))
    h_b, (Cb, nb, mb) = _mlstm_scan(_rev(q_m), _rev(k_m), _rev(v_m), _rev(log_i[:, :, 1]),
                                    _rev(log_f[:, :, 1]), (C0[:, 1], n0[:, 1], m0[:, 1]))
    h_m = (h_f + _rev(h_b)).astype(dt)
    o_m = _rms(h_m, lp["mlstm_norm_g"].reshape(H_M, DV_M)).reshape(B, N, H_M * DV_M) * jax.nn.sigmoid(mo)

    q_a = (_rms(acq, lp["mla_q_norm_g"]) @ lp["mla_w_q_up"]).reshape(B, N, H_A, NOPE_A + ROPE_A)
    q_nope, q_rope = q_a[..., :NOPE_A], q_a[..., NOPE_A:]
    ckv = _rms(ackv, lp["mla_kv_norm_g"])
    if latent:
        q_rope = _rope(q_rope, cos_a, sin_a)
        ckv_all = jnp.concatenate([ctx[0], ckv], axis=1)
        kr_all = jnp.concatenate([ctx[1], _rope(akr[:, :, None, :], cos_a, sin_a)[:, :, 0]], axis=1)
    else:
        ckv_all, kr_all = ckv, akr
    T = ckv_all.shape[1]
    kv = (ckv_all @ lp["mla_w_kv_up"]).reshape(B, T, H_A, NOPE_A + V_A)
    k_a = jnp.concatenate([kv[..., :NOPE_A],
                           jnp.broadcast_to(kr_all[:, :, None, :], (B, T, H_A, ROPE_A))], axis=-1)
    o_a = _attend(jnp.concatenate([q_nope, q_rope], axis=-1), k_a, kv[..., NOPE_A:]).reshape(B, N, H_A * V_A)

    q_d = dq.reshape(B, N, H_D, 2, DK_D)
    k_d = dk.reshape(B, N, H_D, 2, DK_D)
    v_d = dv.reshape(B, N, H_D, DV_D)
    if latent:
        q_d = _rope(q_d, cos_d, sin_d)
        k_all = jnp.concatenate([ctx[2].reshape(B, ctx[2].shape[1], H_D, 2, DK_D),
                                 _rope(k_d, cos_d, sin_d)], axis=1)
        v_all = jnp.concatenate([ctx[3], v_d], axis=1)
    else:
        k_all, v_all = k_d, v_d
    lam_v = lp["diff_lambda"].astype(f32)
    lam = jnp.exp(jnp.sum(lam_v[0] * lam_v[1])) - jnp.exp(jnp.sum(lam_v[2] * lam_v[3])) + lam_init
    o_d = _attend(q_d[..., 0, :], k_all[..., 0, :], v_all, q_d[..., 1, :], k_all[..., 1, :], lam)
    o_d = (_rms(o_d, lp["diff_norm_g"].reshape(H_D, DV_D)) * (1.0 - lam_init)).reshape(B, N, H_D * DV_D)

    g_m, g_a, g_d = jnp.split(jax.nn.sigmoid(gate_pre), 3, axis=-1)
    y = (g_m * (o_m @ lp["w_br_mlstm"]) + g_a * (o_a @ lp["w_br_mla"])
         + g_d * (o_d @ lp["w_br_diff"]))
    x = x + g1 * (y @ lp["w_out"])

    h2 = _rms(x, lp["norm2_g"]) * (1.0 + sc2) + sh2
    x = x + g2 * (jnp.square(jax.nn.relu(h2 @ lp["w_ff1"])) @ lp["w_ff2"])
    if latent:
        return x, None
    new_ctx = (ckv, akr, k_d.reshape(B, N, H_D, 2 * DK_D), v_d,
               jnp.stack([Cf, Cb], axis=1).astype(dt),
               jnp.stack([nf, nb], axis=1).astype(dt),
               jnp.stack([mf, mb], axis=1).astype(dt))
    return x, new_ctx


def setup_inputs(seed: int = 0) -> dict:
    key = jax.random.key(seed)
    ks = iter(jax.random.split(key, 40))

    def nrm(shape, scale=1.0):
        return scale * jax.random.normal(next(ks), shape, jnp.float32)

    def gain(shape):
        return 1.0 + 0.02 * nrm(shape)

    return {
        "x_prompt": nrm((BATCH, SEQ, D_MODEL)),
        "x_sample": nrm((DEC_BATCH, DEC_SEQ, D_MODEL)),
        "c": nrm((DEC_BATCH, D_MODEL)),
        "cache_mla_ckv": nrm((DEC_BATCH, DEPTH, PAST_LEN, KV_RANK)),
        "cache_mla_krope": nrm((DEC_BATCH, DEPTH, PAST_LEN, ROPE_A)),
        "cache_diff_k": nrm((DEC_BATCH, DEPTH, PAST_LEN, H_D, 2 * DK_D)),
        "cache_diff_v": nrm((DEC_BATCH, DEPTH, PAST_LEN, H_D, DV_D)),
        "state_mlstm_C": nrm((DEC_BATCH, DEPTH, 2, H_M, DK_M, DV_M), 0.5),
        "state_mlstm_n": nrm((DEC_BATCH, DEPTH, 2, H_M, DK_M), 0.5),
        "state_mlstm_m": nrm((DEC_BATCH, DEPTH, 2, H_M), 0.5),
        "c_ctx": nrm((D_MODEL,)),
        "w_mod": nrm((DEPTH, D_MODEL, 6 * D_MODEL), 0.5 * D_MODEL ** -0.5),
        "b_mod": nrm((DEPTH, 6 * D_MODEL), 0.02),
        "norm1_g": gain((DEPTH, D_MODEL)),
        "w_in": nrm((DEPTH, D_MODEL, IN_COLS), D_MODEL ** -0.5),
        "mlstm_gate_b": jnp.concatenate([nrm((DEPTH, 2 * H_M), 0.1),
                                         F_BIAS + nrm((DEPTH, 2 * H_M), 0.1)], axis=-1),
        "mlstm_norm_g": gain((DEPTH, H_M * DV_M)),
        "mla_q_norm_g": gain((DEPTH, Q_RANK)),
        "mla_w_q_up": nrm((DEPTH, Q_RANK, H_A * (NOPE_A + ROPE_A)), Q_RANK ** -0.5),
        "mla_kv_norm_g": gain((DEPTH, KV_RANK)),
        "mla_w_kv_up": nrm((DEPTH, KV_RANK, H_A * (NOPE_A + V_A)), KV_RANK ** -0.5),
        "diff_lambda": nrm((DEPTH, 4, DK_D), 0.1),
        "diff_norm_g": gain((DEPTH, H_D * DV_D)),
        "w_br_mlstm": nrm((DEPTH, H_M * DV_M, D_MODEL), (H_M * DV_M) ** -0.5),
        "w_br_mla": nrm((DEPTH, H_A * V_A, D_MODEL), (H_A * V_A) ** -0.5),
        "w_br_diff": nrm((DEPTH, H_D * DV_D, D_MODEL), (H_D * DV_D) ** -0.5),
        "w_out": nrm((DEPTH, D_MODEL, D_MODEL), D_MODEL ** -0.5),
        "norm2_g": gain((DEPTH, D_MODEL)),
        "w_ff1": nrm((DEPTH, D_MODEL, D_FF), D_MODEL ** -0.5),
        "w_ff2": nrm((DEPTH, D_FF, D_MODEL), D_FF ** -0.5),
        "final_norm_g": gain((D_MODEL,)),
    }


def reference(x_prompt, x_sample, c, cache_mla_ckv, cache_mla_krope, cache_diff_k, cache_diff_v,
              state_mlstm_C, state_mlstm_n, state_mlstm_m, c_ctx, w_mod, b_mod, norm1_g, w_in,
              mlstm_gate_b, mlstm_norm_g, mla_q_norm_g, mla_w_q_up, mla_kv_norm_g, mla_w_kv_up,
              diff_lambda, diff_norm_g, w_br_mlstm, w_br_mla, w_br_diff, w_out, norm2_g, w_ff1, w_ff2,
              final_norm_g):
    y_p = x_prompt
    y_s = x_sample
    cond_ctx = c_ctx[None, :]
    st = ([], [], [], [], [], [], [])
    for l in range(DEPTH):
        lp = {
            "w_mod": w_mod[l], "b_mod": b_mod[l], "norm1_g": norm1_g[l], "w_in": w_in[l],
            "mlstm_gate_b": mlstm_gate_b[l], "mlstm_norm_g": mlstm_norm_g[l],
            "mla_q_norm_g": mla_q_norm_g[l], "mla_w_q_up": mla_w_q_up[l],
            "mla_kv_norm_g": mla_kv_norm_g[l], "mla_w_kv_up": mla_w_kv_up[l],
            "diff_lambda": diff_lambda[l], "diff_norm_g": diff_norm_g[l],
            "w_br_mlstm": w_br_mlstm[l], "w_br_mla": w_br_mla[l], "w_br_diff": w_br_diff[l],
            "w_out": w_out[l], "norm2_g": norm2_g[l], "w_ff1": w_ff1[l], "w_ff2": w_ff2[l],
        }
        lam_init = 0.8 - 0.6 * math.exp(-0.3 * l)
        y_p, ctx_new = _layer(y_p, cond_ctx, lp, lam_init, None)
        for lst, a in zip(st, ctx_new):
            lst.append(a)
        ctx_l = (cache_mla_ckv[:, l], cache_mla_krope[:, l], cache_diff_k[:, l], cache_diff_v[:, l],
                 state_mlstm_C[:, l], state_mlstm_n[:, l], state_mlstm_m[:, l])
        y_s, _ = _layer(y_s, c, lp, lam_init, ctx_l)
    y_prompt = _rms(y_p, final_norm_g)
    y_sample = _rms(y_s, final_norm_g)
    new_mla_ckv = jnp.stack(st[0], axis=1)
    new_mla_krope = jnp.stack(st[1], axis=1)
    new_diff_k = jnp.stack(st[2], axis=1)
    new_diff_v = jnp.stack(st[3], axis=1)
    new_mlstm_C = jnp.stack(st[4], axis=1)
    new_mlstm_n = jnp.stack(st[5], axis=1)
    new_mlstm_m = jnp.stack(st[6], axis=1)
    return (y_prompt, y_sample, new_mla_ckv, new_mla_krope, new_diff_k, new_diff_v,
            new_mlstm_C, new_mlstm_n, new_mlstm_m)
```

```python
import functools
import math

import jax
import jax.numpy as jnp
from jax import lax
from jax.experimental import pallas as pl
from jax.experimental.pallas import tpu as pltpu

F32 = jnp.float32
BF16 = jnp.bfloat16

D_MODEL = 1024
DEPTH = 2
GRID_W = 64
ROPE_BASE = 10000.0
EPS = 1e-6
H_M, DK_M, DV_M = 4, 64, 128
H_A, Q_RANK, KV_RANK, NOPE_A, ROPE_A, V_A = 8, 384, 256, 64, 32, 64
H_D, DK_D, DV_D = 4, 64, 128
D_FF = 4 * D_MODEL

LANE = 128
VMEM_LIMIT = 56 * 1024 * 1024
ROW_TILE = 256
Q_TILE = 256
CHUNK = 128
NEG = -1e30

C_MQ, C_MK, C_MV, C_MO, C_GI, C_GF = 0, 256, 512, 1024, 1536, 1664
M_COLS = 1792
C_A = 1792
A_COLS = 768
C_D = 2560
D_COLS = 1536
C_G = 4096
G_COLS = 3 * D_MODEL
IN_COLS_P = 7168
S_COLS = H_M * DV_M + LANE


def _dot(a, b):
    return jnp.dot(a, b, preferred_element_type=F32)


def _dot_nt(a, b):
    return lax.dot_general(a, b, (((1,), (1,)), ((), ())), preferred_element_type=F32)


def _dot_tn(a, b):
    return lax.dot_general(a, b, (((0,), (0,)), ((), ())), preferred_element_type=F32)


def _rms_rows(x, g):
    return x * lax.rsqrt(jnp.mean(x * x, axis=-1, keepdims=True) + EPS) * g


def _rope_blocks(x, cos, sa, sb, shift):
    outs = []
    for b in range(x.shape[1] // LANE):
        xb = x[:, b * LANE:(b + 1) * LANE]
        outs.append(xb * cos + pltpu.roll(xb, shift, 1) * sa + pltpu.roll(xb, LANE - shift, 1) * sb)
    return outs[0] if len(outs) == 1 else jnp.concatenate(outs, axis=1)


def _params(sem):
    return pltpu.CompilerParams(dimension_semantics=sem, vmem_limit_bytes=VMEM_LIMIT)


def _const_spec(shape):
    nd = len(shape)
    return pl.BlockSpec(shape, lambda *_: (0,) * nd, pipeline_mode=pl.Buffered(1))


def _mod_kernel(c_ref, w_ref, b_ref, o_ref):
    c = c_ref[...]
    s = (c * jax.nn.sigmoid(c)).astype(BF16)
    o_ref[...] = _dot(s, w_ref[...].astype(BF16)) + b_ref[...]


def _modulation(cond8, w_mod, b_mod):
    tn = 1536
    n6 = 6 * D_MODEL
    return pl.pallas_call(
        _mod_kernel,
        grid=(DEPTH, n6 // tn),
        in_specs=[
            pl.BlockSpec((8, D_MODEL), lambda l, j: (0, 0)),
            pl.BlockSpec((None, D_MODEL, tn), lambda l, j: (l, 0, j)),
            pl.BlockSpec((None, 1, tn), lambda l, j: (l, 0, j)),
        ],
        out_specs=pl.BlockSpec((None, 8, tn), lambda l, j: (l, 0, j)),
        out_shape=jax.ShapeDtypeStruct((DEPTH, 8, n6), F32),
        compiler_params=_params(("parallel", "parallel")),
        name="modulation",
    )(cond8, w_mod, b_mod.reshape(DEPTH, 1, n6))


def _inproj_kernel(latent, *refs):
    if latent:
        (x_ref, mod_ref, g1_ref, w_ref, gq_ref, gkv_ref, wq_ref, wk_ref, wv_ref, tab_ref,
         m_ref, q_ref, k_ref, va_ref, qd_ref, kd_ref, vd_ref, gate_ref) = refs
    else:
        (x_ref, mod_ref, g1_ref, w_ref, gq_ref, gkv_ref, wq_ref, wk_ref, wv_ref,
         m_ref, q_ref, k_ref, va_ref, qd_ref, kd_ref, vd_ref, gate_ref,
         ckv_ref, akr_ref, kdraw_ref, vdraw_ref) = refs
    d = D_MODEL
    x = x_ref[...]
    mod = mod_ref[...]
    h = (_rms_rows(x, g1_ref[...]) * (1.0 + mod[:, d:2 * d]) + mod[:, 0:d]).astype(BF16)

    m_ref[...] = _dot(h, w_ref[:, 0:M_COLS])

    za = _dot(h, w_ref[:, C_A:C_A + A_COLS])
    acq = za[:, 0:Q_RANK]
    ackv = za[:, Q_RANK:Q_RANK + KV_RANK]
    akr = za[:, Q_RANK + KV_RANK:A_COLS]
    q = _dot(_rms_rows(acq, gq_ref[...]).astype(BF16), wq_ref[...])
    ckv = _rms_rows(ackv, gkv_ref[...])
    ckv_b = ckv.astype(BF16)
    kn = _dot(ckv_b, wk_ref[...])
    if not latent:
        ckv_ref[...] = ckv
        akr_ref[...] = akr
    else:
        tab = tab_ref[...]
        cq, saq, sbq = tab[:, 0:128], tab[:, 128:256], tab[:, 256:384]
        q = _rope_blocks(q, cq, saq, sbq, ROPE_A // 2)
        akr = _rope_blocks(akr, cq, saq, sbq, ROPE_A // 2)
    q_ref[...] = (q * (NOPE_A + ROPE_A) ** -0.5).astype(BF16)
    k_ref[...] = (kn + jnp.concatenate([akr] * H_A, axis=1)).astype(BF16)
    va_ref[...] = _dot(ckv_b, wv_ref[...]).astype(BF16)

    zd = _dot(h, w_ref[:, C_D:C_D + D_COLS])
    dq, dk, dv = zd[:, 0:512], zd[:, 512:1024], zd[:, 1024:1536]
    if not latent:
        kdraw_ref[...] = dk
        vdraw_ref[...] = dv
    else:
        cd, sad, sbd = tab[:, 384:512], tab[:, 512:640], tab[:, 640:768]
        dq = _rope_blocks(dq, cd, sad, sbd, DK_D // 2)
        dk = _rope_blocks(dk, cd, sad, sbd, DK_D // 2)
    qd_ref[...] = (dq * DK_D ** -0.5).astype(BF16)
    kd_ref[...] = dk.astype(BF16)
    vd_ref[...] = dv.astype(BF16)

    gate_ref[...] = _dot(h, w_ref[:, C_G:C_G + G_COLS])


def _inproj(x2, mod_l, rows_per_cond, lw, tables):
    latent = tables is not None
    t = x2.shape[0]
    tm = ROW_TILE
    tiles_per_cond = rows_per_cond // tm
    first_row = 1 if latent else 0

    def row(i):
        return (i, 0)

    in_specs = [
        pl.BlockSpec((tm, D_MODEL), row),
        pl.BlockSpec((None, 1, 6 * D_MODEL), lambda i: (first_row + i // tiles_per_cond, 0, 0)),
        _const_spec((1, D_MODEL)),
        _const_spec((D_MODEL, IN_COLS_P)),
        _const_spec((1, Q_RANK)),
        _const_spec((1, KV_RANK)),
        _const_spec((Q_RANK, H_A * LANE)),
        _const_spec((KV_RANK, H_A * LANE)),
        _const_spec((KV_RANK, H_A * V_A)),
    ]
    args = [x2, mod_l, lw["norm1_g"], lw["w_in"], lw["mla_q_norm_g"], lw["mla_kv_norm_g"],
            lw["wq"], lw["wk"], lw["wv"]]
    if latent:
        n_tab = tables.shape[0] // tm
        in_specs.append(pl.BlockSpec((tm, 768), lambda i: (i % n_tab, 0)))
        args.append(tables)
    widths = [(M_COLS, F32), (H_A * LANE, BF16), (H_A * LANE, BF16), (H_A * V_A, BF16),
              (512, BF16), (512, BF16), (512, BF16), (G_COLS, F32)]
    if not latent:
        widths += [(KV_RANK, F32), (LANE, F32), (512, F32), (512, F32)]
    return pl.pallas_call(
        functools.partial(_inproj_kernel, latent),
        grid=(t // tm,),
        in_specs=in_specs,
        out_specs=[pl.BlockSpec((tm, w), row) for w, _ in widths],
        out_shape=[jax.ShapeDtypeStruct((t, w), dt) for w, dt in widths],
        compiler_params=_params(("parallel",)),
        name="inproj_lat" if latent else "inproj_ctx",
    )(*args)


def _mlstm_direction(z, s_prev, m_prev, bi, bfg, smask, d):
    L = CHUNK
    q = (z[:, C_MQ:C_MQ + 256] * DK_M ** -0.5).astype(BF16)
    k = z[:, C_MK:C_MK + 256]
    v = z[:, C_MV:C_MV + 512].astype(BF16)
    gi = z[:, C_GI:C_GI + LANE] + bi
    xf = z[:, C_GF:C_GF + LANE] + bfg
    lf = jnp.minimum(xf, 0.0) - jnp.log(1.0 + jnp.exp(-jnp.abs(xf)))

    r_i = lax.broadcasted_iota(jnp.int32, (L, L), 0)
    c_i = lax.broadcasted_iota(jnp.int32, (L, L), 1)
    mask = (c_i <= r_i) if d == 0 else (c_i >= r_i)
    tri = jnp.where(mask, 1.0, 0.0).astype(BF16)
    lf_hi = lf.astype(BF16)
    r1 = lf - lf_hi.astype(F32)
    lf_mid = r1.astype(BF16)
    lf_lo = (r1 - lf_mid.astype(F32)).astype(BF16)
    b = _dot(tri, lf_hi) + _dot(tri, lf_mid) + _dot(tri, lf_lo)
    g = gi - b
    g_t = g.T
    last = L - 1 if d == 0 else 0

    qs = _dot(q, s_prev.astype(BF16))
    lane256 = lax.broadcasted_iota(jnp.int32, (L, 256), 1)
    lane128 = lax.broadcasted_iota(jnp.int32, (1, LANE), 1)
    kb = k.astype(BF16)

    h_out = []
    w_cols = []
    w_old = []
    m_new = m_prev
    for hd in range(H_M):
        j = d * H_M + hd
        b_col = b[:, j:j + 1]
        g_col = g[:, j:j + 1]
        g_row = g_t[j:j + 1, :]
        mp = m_prev[:, j:j + 1]
        gm = jnp.where(mask, g_row, NEG)
        u_col = jnp.maximum(jnp.max(gm, axis=-1, keepdims=True), mp)
        dmat = jnp.exp(gm - u_col)
        q_h = jnp.where((lane256 >= hd * DK_M) & (lane256 < (hd + 1) * DK_M), q, jnp.zeros_like(q))
        a = dmat * _dot_nt(q_h, kb)
        den_i = jnp.sum(a, axis=-1, keepdims=True)
        num_i = _dot(a.astype(BF16), v[:, hd * DV_M:(hd + 1) * DV_M])
        w_int = jnp.exp(mp - u_col)
        num = w_int * qs[:, hd * DV_M:(hd + 1) * DV_M] + num_i
        den = w_int * qs[:, H_M * DV_M + j:H_M * DV_M + j + 1] + den_i
        h_out.append(num / jnp.maximum(jnp.abs(den), jnp.exp(-(b_col + u_col))))
        u_l = u_col[last:last + 1, :]
        b_l = b_col[last:last + 1, :]
        m_new = jnp.where(lane128 == j, b_l + u_l, m_new)
        w_cols.append(jnp.exp(g_col - u_l))
        w_old.append(jnp.exp(mp - u_l))

    w_exp = jnp.where(lane256 < 64, w_cols[0],
                      jnp.where(lane256 < 128, w_cols[1], jnp.where(lane256 < 192, w_cols[2], w_cols[3])))
    k_w = (k * w_exp).astype(BF16)
    v_aug = jnp.concatenate([v, jnp.ones((L, LANE), BF16)], axis=1)
    upd = _dot_tn(k_w, v_aug)
    n_scale = jnp.zeros((1, LANE), F32)
    for hd in range(H_M):
        n_scale = jnp.where(lane128 == d * H_M + hd, w_old[hd], n_scale)
    scale_row = jnp.concatenate([jnp.broadcast_to(w, (1, DV_M)) for w in w_old] + [n_scale], axis=1)
    s_new = scale_row * s_prev + smask * upd
    return jnp.concatenate(h_out, axis=1), s_new, m_new


def _mlstm_kernel(zf_ref, zb_ref, s0_ref, m0_ref, bi_ref, bf_ref, smask_ref,
                  hf_ref, hb_ref, st_ref, mt_ref, s_scr, m_scr):
    i = pl.program_id(1)

    @pl.when(i == 0)
    def _():
        s_scr[...] = s0_ref[...]
        m_scr[...] = m0_ref[...]

    bi = bi_ref[...]
    bfg = bf_ref[...]
    for d, (z_ref, h_ref) in enumerate(((zf_ref, hf_ref), (zb_ref, hb_ref))):
        h, s_new, m_new = _mlstm_direction(z_ref[...], s_scr[d], m_scr[d], bi, bfg, smask_ref[d], d)
        h_ref[...] = h
        s_scr[d] = s_new
        m_scr[d] = m_new

    @pl.when(i == pl.num_programs(1) - 1)
    def _():
        st_ref[...] = s_scr[...]
        mt_ref[...] = m_scr[...]


def _mlstm(m_in, s0, m0, lw, smask, batch, n_tok):
    nc = n_tok // CHUNK
    t = batch * n_tok
    state_spec = pl.BlockSpec((None, 2, 256, S_COLS), lambda b, i: (b, 0, 0, 0))
    m_spec = pl.BlockSpec((None, 2, 1, LANE), lambda b, i: (b, 0, 0, 0))
    fwd = lambda b, i: (b * nc + i, 0)
    bwd = lambda b, i: (b * nc + nc - 1 - i, 0)
    return pl.pallas_call(
        _mlstm_kernel,
        grid=(batch, nc),
        in_specs=[
            pl.BlockSpec((CHUNK, M_COLS), fwd),
            pl.BlockSpec((CHUNK, M_COLS), bwd),
            state_spec, m_spec,
            pl.BlockSpec((1, LANE), lambda b, i: (0, 0)),
            pl.BlockSpec((1, LANE), lambda b, i: (0, 0)),
            pl.BlockSpec((2, 256, S_COLS), lambda b, i: (0, 0, 0)),
        ],
        out_specs=[
            pl.BlockSpec((CHUNK, H_M * DV_M), fwd),
            pl.BlockSpec((CHUNK, H_M * DV_M), bwd),
            state_spec, m_spec,
        ],
        out_shape=[
            jax.ShapeDtypeStruct((t, H_M * DV_M), F32),
            jax.ShapeDtypeStruct((t, H_M * DV_M), F32),
            jax.ShapeDtypeStruct((batch, 2, 256, S_COLS), F32),
            jax.ShapeDtypeStruct((batch, 2, 1, LANE), F32),
        ],
        scratch_shapes=[pltpu.VMEM((2, 256, S_COLS), F32), pltpu.VMEM((2, 1, LANE), F32)],
        compiler_params=_params(("parallel", "arbitrary")),
        name="mlstm",
    )(m_in, m_in, s0, m0, lw["gate_bi"], lw["gate_bf"], smask)


def _mla_kernel(q_ref, k_ref, v_ref, o_ref):
    lane = lax.broadcasted_iota(jnp.int32, (1, LANE), 1)
    v = v_ref[...]
    out = None
    for e in range(2):
        s = _dot_nt(q_ref[:, e * LANE:(e + 1) * LANE], k_ref[:, e * LANE:(e + 1) * LANE])
        p = jnp.exp(s - jnp.max(s, axis=-1, keepdims=True))
        l = jnp.sum(p, axis=-1, keepdims=True)
        o = _dot(p.astype(BF16), v) / l
        out = o if e == 0 else jnp.where(lane < V_A, out, o)
    o_ref[...] = out.astype(BF16)


def _mla_attention(q, k, v, batch, n_q, n_kv):
    tq = min(Q_TILE, n_q)
    nq = n_q // tq
    return pl.pallas_call(
        _mla_kernel,
        grid=(batch, H_A // 2, nq),
        in_specs=[
            pl.BlockSpec((tq, 2 * LANE), lambda b, j, i: (b * nq + i, j)),
            pl.BlockSpec((n_kv, 2 * LANE), lambda b, j, i: (b, j)),
            pl.BlockSpec((n_kv, LANE), lambda b, j, i: (b, j)),
        ],
        out_specs=pl.BlockSpec((tq, LANE), lambda b, j, i: (b * nq + i, j)),
        out_shape=jax.ShapeDtypeStruct((batch * n_q, H_A * V_A), BF16),
        compiler_params=_params(("parallel", "parallel", "parallel")),
        name="mla_attention",
    )(q, k, v)


def _diff_kernel(lam_init, q_ref, k_ref, v_ref, lam_ref, g_ref, o_ref):
    lane = lax.broadcasted_iota(jnp.int32, (1, LANE), 1)
    q = q_ref[...]
    k = k_ref[...]
    lv = lam_ref[...]
    lam = (jnp.exp(jnp.sum(lv[0:1] * lv[1:2], axis=-1, keepdims=True))
           - jnp.exp(jnp.sum(lv[2:3] * lv[3:4], axis=-1, keepdims=True)) + lam_init)
    zero = jnp.zeros_like(q)
    s1 = _dot_nt(jnp.where(lane < DK_D, q, zero), k)
    s2 = _dot_nt(jnp.where(lane >= DK_D, q, zero), k)
    p1 = jnp.exp(s1 - jnp.max(s1, axis=-1, keepdims=True))
    p2 = jnp.exp(s2 - jnp.max(s2, axis=-1, keepdims=True))
    r1 = 1.0 / jnp.sum(p1, axis=-1, keepdims=True)
    r2 = lam / jnp.sum(p2, axis=-1, keepdims=True)
    o = _dot((p1 * r1 - p2 * r2).astype(BF16), v_ref[...])
    o_ref[...] = (_rms_rows(o, g_ref[...]) * (1.0 - lam_init)).astype(BF16)


def _diff_attention(q, k, v, lam_v, g, lam_init, batch, n_q, n_kv):
    tq = min(Q_TILE, n_q)
    nq = n_q // tq
    return pl.pallas_call(
        functools.partial(_diff_kernel, lam_init),
        grid=(batch, H_D, nq),
        in_specs=[
            pl.BlockSpec((tq, LANE), lambda b, h, i: (b * nq + i, h)),
            pl.BlockSpec((n_kv, LANE), lambda b, h, i: (b, h)),
            pl.BlockSpec((n_kv, LANE), lambda b, h, i: (b, h)),
            pl.BlockSpec((4, DK_D), lambda b, h, i: (0, 0)),
            pl.BlockSpec((1, LANE), lambda b, h, i: (0, h)),
        ],
        out_specs=pl.BlockSpec((tq, LANE), lambda b, h, i: (b * nq + i, h)),
        out_shape=jax.ShapeDtypeStruct((batch * n_q, H_D * DV_D), BF16),
        compiler_params=_params(("parallel", "parallel", "parallel")),
        name="diff_attention",
    )(q, k, v, lam_v, g)


def _merge_kernel(final, x_ref, mod_ref, hf_ref, hb_ref, mo_ref, oa_ref, od_ref, gate_ref,
                  gm_ref, wbm_ref, wba_ref, wbd_ref, wout_ref, g2_ref, wff1_ref, wff2_ref, gfin_ref, o_ref):
    d = D_MODEL
    mod = mod_ref[...]
    hm = hf_ref[...] + hb_ref[...]
    gm = gm_ref[...]
    o_m = jnp.concatenate(
        [_rms_rows(hm[:, h * DV_M:(h + 1) * DV_M], gm[:, h * DV_M:(h + 1) * DV_M]) for h in range(H_M)], axis=1)
    o_m = (o_m * jax.nn.sigmoid(mo_ref[...])).astype(BF16)
    gate = jax.nn.sigmoid(gate_ref[...])
    y = (gate[:, 0:d] * _dot(o_m, wbm_ref[...]) + gate[:, d:2 * d] * _dot(oa_ref[...], wba_ref[...])
         + gate[:, 2 * d:3 * d] * _dot(od_ref[...], wbd_ref[...]))
    x = x_ref[...] + mod[:, 2 * d:3 * d] * _dot(y.astype(BF16), wout_ref[...])
    h2 = (_rms_rows(x, g2_ref[...]) * (1.0 + mod[:, 4 * d:5 * d]) + mod[:, 3 * d:4 * d]).astype(BF16)
    f = jnp.maximum(_dot(h2, wff1_ref[...]), 0.0)
    x = x + mod[:, 5 * d:6 * d] * _dot((f * f).astype(BF16), wff2_ref[...])
    if final:
        x = _rms_rows(x, gfin_ref[...])
    o_ref[...] = x


def _merge(x2, mod_l, rows_per_cond, first_row, m_in, h_f, h_b, o_a, o_d, gate, lw, gfin, final):
    t = x2.shape[0]
    tm = ROW_TILE
    tiles_per_cond = rows_per_cond // tm
    row = lambda i: (i, 0)
    in_specs = [
        pl.BlockSpec((tm, D_MODEL), row),
        pl.BlockSpec((None, 1, 6 * D_MODEL), lambda i: (first_row + i // tiles_per_cond, 0, 0)),
        pl.BlockSpec((tm, 512), row),
        pl.BlockSpec((tm, 512), row),
        pl.BlockSpec((tm, 512), lambda i: (i, C_MO // 512)),
        pl.BlockSpec((tm, 512), row),
        pl.BlockSpec((tm, 512), row),
        pl.BlockSpec((tm, G_COLS), row),
        _const_spec((1, 512)),
        _const_spec((512, D_MODEL)),
        _const_spec((512, D_MODEL)),
        _const_spec((512, D_MODEL)),
        _const_spec((D_MODEL, D_MODEL)),
        _const_spec((1, D_MODEL)),
        _const_spec((D_MODEL, D_FF)),
        _const_spec((D_FF, D_MODEL)),
        _const_spec((1, D_MODEL)),
    ]
    return pl.pallas_call(
        functools.partial(_merge_kernel, final),
        grid=(t // tm,),
        in_specs=in_specs,
        out_specs=pl.BlockSpec((tm, D_MODEL), row),
        out_shape=jax.ShapeDtypeStruct((t, D_MODEL), F32),
        compiler_params=_params(("parallel",)),
        name="merge_mlp",
    )(x2, mod_l, h_f, h_b, m_in, o_a, o_d, gate, lw["mlstm_norm_g"], lw["w_br_mlstm"], lw["w_br_mla"],
      lw["w_br_diff"], lw["w_out"], lw["norm2_g"], lw["w_ff1"], lw["w_ff2"], gfin)


def _prep_layer_weights(l, w_in, mlstm_gate_b, norm1_g, mlstm_norm_g, mla_q_norm_g, mla_w_q_up, mla_kv_norm_g,
                        mla_w_kv_up, diff_lambda, diff_norm_g, w_br_mlstm, w_br_mla, w_br_diff, w_out,
                        norm2_g, w_ff1, w_ff2):
    w = w_in[l]
    z = lambda n: jnp.zeros((D_MODEL, n), F32)
    c_mg = 2 * H_M * DK_M + 2 * H_M * DV_M
    c_acq = c_mg + 4 * H_M
    c_akr = c_acq + Q_RANK + KV_RANK
    c_dq = c_akr + ROPE_A
    w_p = jnp.concatenate([
        w[:, :c_mg], w[:, c_mg:c_mg + 8], z(LANE - 8), w[:, c_mg + 8:c_mg + 16], z(LANE - 8),
        w[:, c_acq:c_akr], z(NOPE_A), w[:, c_akr:c_dq], z(LANE - NOPE_A - ROPE_A), w[:, c_dq:]], axis=1)
    gb = mlstm_gate_b[l]
    pad8 = lambda a: jnp.concatenate([a, jnp.zeros((LANE - 8,), F32)])[None, :]
    wq = mla_w_q_up[l].reshape(Q_RANK, H_A, NOPE_A + ROPE_A)
    wq = jnp.pad(wq, ((0, 0), (0, 0), (0, LANE - NOPE_A - ROPE_A))).reshape(Q_RANK, H_A * LANE)
    wkv = mla_w_kv_up[l].reshape(KV_RANK, H_A, NOPE_A + V_A)
    wk = jnp.pad(wkv[:, :, :NOPE_A], ((0, 0), (0, 0), (0, LANE - NOPE_A))).reshape(KV_RANK, H_A * LANE)
    wv = wkv[:, :, NOPE_A:].reshape(KV_RANK, H_A * V_A)
    return {
        "w_in": w_p.astype(BF16),
        "gate_bi": pad8(gb[:8]), "gate_bf": pad8(gb[8:]),
        "norm1_g": norm1_g[l][None, :], "norm2_g": norm2_g[l][None, :],
        "mlstm_norm_g": mlstm_norm_g[l][None, :],
        "mla_q_norm_g": mla_q_norm_g[l][None, :], "mla_kv_norm_g": mla_kv_norm_g[l][None, :],
        "wq": wq.astype(BF16), "wk": wk.astype(BF16), "wv": wv.astype(BF16),
        "diff_lambda": diff_lambda[l], "diff_norm_g": diff_norm_g[l][None, :],
        "w_br_mlstm": w_br_mlstm[l].astype(BF16), "w_br_mla": w_br_mla[l].astype(BF16),
        "w_br_diff": w_br_diff[l].astype(BF16), "w_out": w_out[l].astype(BF16),
        "w_ff1": w_ff1[l].astype(BF16), "w_ff2": w_ff2[l].astype(BF16),
    }


def _rope_tables(n_tokens):
    rows = n_tokens // GRID_W
    row = jnp.repeat(jnp.arange(rows, dtype=F32), GRID_W)
    col = jnp.tile(jnp.arange(GRID_W, dtype=F32), rows)

    def cs(dim):
        quarter = dim // 4
        inv = ROPE_BASE ** (-jnp.arange(quarter, dtype=F32) / quarter)
        ang = jnp.concatenate([row[:, None] * inv, col[:, None] * inv], axis=-1)
        return jnp.cos(ang), jnp.sin(ang)

    one = lambda n: jnp.ones((n_tokens, n), F32)
    zero = lambda n: jnp.zeros((n_tokens, n), F32)
    ca, sa = cs(ROPE_A)
    cd, sd = cs(DK_D)
    return jnp.concatenate([
        one(64), ca, ca, one(32),
        zero(80), sa, zero(32),
        zero(64), -sa, zero(48),
        cd, cd, cd, cd,
        zero(32), sd, zero(32), sd,
        -sd, zero(32), -sd, zero(32)], axis=1)


def _state_mask():
    r = jnp.arange(256)[:, None] // DK_M
    c = jnp.arange(S_COLS)[None, :]
    diag = (c < H_M * DV_M) & (c // DV_M == r)
    return jnp.stack([(diag | (c == H_M * DV_M + d * H_M + r)) for d in range(2)]).astype(F32)


def _pack_state(c0, n0, m0):
    b = c0.shape[0]
    s = jnp.zeros((b, 2, 256, S_COLS), F32)
    m = jnp.zeros((b, 2, 1, LANE), F32)
    for h in range(H_M):
        s = s.at[:, :, h * DK_M:(h + 1) * DK_M, h * DV_M:(h + 1) * DV_M].set(c0[:, :, h])
        for d in range(2):
            s = s.at[:, d, h * DK_M:(h + 1) * DK_M, H_M * DV_M + d * H_M + h].set(n0[:, d, h])
            m = m.at[:, d, 0, d * H_M + h].set(m0[:, d, h])
    return s, m


def _unpack_state(s, m):
    c = jnp.stack([s[:, :, h * DK_M:(h + 1) * DK_M, h * DV_M:(h + 1) * DV_M] for h in range(H_M)], axis=2)
    n = jnp.stack([jnp.stack([s[:, d, h * DK_M:(h + 1) * DK_M, H_M * DV_M + d * H_M + h] for h in range(H_M)], axis=1)
                   for d in range(2)], axis=1)
    mm = jnp.stack([m[:, d, 0, d * H_M:(d + 1) * H_M] for d in range(2)], axis=1)
    return c, n, mm


def _layer(x2, batch, n_tok, mod_l, lw, lam_init, smask, gfin, final, ctx, tables):
    latent = ctx is not None
    rows_per_cond = n_tok if latent else batch * n_tok
    first_row = 1 if latent else 0
    outs = _inproj(x2, mod_l, rows_per_cond, lw, tables)
    m_in, q_a, k_a, v_a, q_d, k_d, v_d, gate = outs[:8]
    if latent:
        ck_a, cv_a, ck_d, cv_d, s0, m0 = ctx
        cat = lambda c, new: jnp.concatenate(
            [c, new.reshape(batch, n_tok, new.shape[-1])], axis=1).reshape(-1, new.shape[-1])
        k_a, v_a, k_d, v_d = cat(ck_a, k_a), cat(cv_a, v_a), cat(ck_d, k_d), cat(cv_d, v_d)
        n_kv = ck_a.shape[1] + n_tok
    else:
        s0 = jnp.zeros((batch, 2, 256, S_COLS), F32)
        m0 = jnp.zeros((batch, 2, 1, LANE), F32)
        n_kv = n_tok
    h_f, h_b, s_t, m_t = _mlstm(m_in, s0, m0, lw, smask, batch, n_tok)
    o_a = _mla_attention(q_a, k_a, v_a, batch, n_tok, n_kv)
    o_d = _diff_attention(q_d, k_d, v_d, lw["diff_lambda"], lw["diff_norm_g"], lam_init, batch, n_tok, n_kv)
    x_new = _merge(x2, mod_l, rows_per_cond, first_row, m_in, h_f, h_b, o_a, o_d, gate, lw, gfin, final)
    new_ctx = None
    if not latent:
        ckv, akr, kd_raw, vd_raw = outs[8:]
        new_ctx = (ckv, akr[:, NOPE_A:NOPE_A + ROPE_A], kd_raw, vd_raw) + _unpack_state(s_t, m_t)
    return x_new, new_ctx


def _cache_kv_kernel(ckv_ref, kr_ref, wk_ref, wv_ref, k_ref, v_ref):
    ckv = ckv_ref[...].astype(BF16)
    k_ref[...] = (_dot(ckv, wk_ref[...]) + jnp.concatenate([kr_ref[...]] * H_A, axis=1)).astype(BF16)
    v_ref[...] = _dot(ckv, wv_ref[...]).astype(BF16)


def _cache_kv(ckv, kr, lw):
    t = ckv.shape[0]
    tm = min(ROW_TILE, t)
    row = lambda i: (i, 0)
    return pl.pallas_call(
        _cache_kv_kernel,
        grid=(t // tm,),
        in_specs=[pl.BlockSpec((tm, KV_RANK), row), pl.BlockSpec((tm, LANE), row),
                  _const_spec((KV_RANK, H_A * LANE)), _const_spec((KV_RANK, H_A * V_A))],
        out_specs=[pl.BlockSpec((tm, H_A * LANE), row), pl.BlockSpec((tm, H_A * V_A), row)],
        out_shape=[jax.ShapeDtypeStruct((t, H_A * LANE), BF16), jax.ShapeDtypeStruct((t, H_A * V_A), BF16)],
        compiler_params=_params(("parallel",)),
        name="cache_kv",
    )(ckv, jnp.pad(kr, ((0, 0), (NOPE_A, LANE - NOPE_A - ROPE_A))), lw["wk"], lw["wv"])


def kernel(x_prompt, x_sample, c, cache_mla_ckv, cache_mla_krope, cache_diff_k, cache_diff_v, state_mlstm_C, state_mlstm_n, state_mlstm_m, c_ctx, w_mod, b_mod, norm1_g, w_in, mlstm_gate_b, mlstm_norm_g, mla_q_norm_g, mla_w_q_up, mla_kv_norm_g, mla_w_kv_up, diff_lambda, diff_norm_g, w_br_mlstm, w_br_mla, w_br_diff, w_out, norm2_g, w_ff1, w_ff2, final_norm_g):
    bp, sp, _ = x_prompt.shape
    bs, ss, _ = x_sample.shape
    past = cache_mla_ckv.shape[2]
    assert bs + 1 <= 8 and sp % ROW_TILE == 0 and ss % ROW_TILE == 0 and ss % GRID_W == 0

    cond8 = jnp.concatenate([c_ctx[None, :], c, jnp.zeros((8 - 1 - bs, D_MODEL), F32)], axis=0)
    mod = _modulation(cond8, w_mod, b_mod).reshape(DEPTH, 8, 1, 6 * D_MODEL)
    tables = _rope_tables(ss)
    smask = _state_mask()
    gfin = final_norm_g[None, :]

    y_p = x_prompt.reshape(bp * sp, D_MODEL)
    y_s = x_sample.reshape(bs * ss, D_MODEL)
    st = [[] for _ in range(7)]
    for l in range(DEPTH):
        lw = _prep_layer_weights(l, w_in, mlstm_gate_b, norm1_g, mlstm_norm_g, mla_q_norm_g, mla_w_q_up,
                                 mla_kv_norm_g, mla_w_kv_up, diff_lambda, diff_norm_g, w_br_mlstm, w_br_mla,
                                 w_br_diff, w_out, norm2_g, w_ff1, w_ff2)
        lam_init = 0.8 - 0.6 * math.exp(-0.3 * l)
        final = l == DEPTH - 1
        y_p, new_ctx = _layer(y_p, bp, sp, mod[l], lw, lam_init, smask, gfin, final, None, None)
        for lst, a in zip(st, new_ctx):
            lst.append(a)
        ck_a, cv_a = _cache_kv(cache_mla_ckv[:, l].reshape(bs * past, KV_RANK),
                               cache_mla_krope[:, l].reshape(bs * past, ROPE_A), lw)
        s0, m0 = _pack_state(state_mlstm_C[:, l], state_mlstm_n[:, l], state_mlstm_m[:, l])
        ctx = (ck_a.reshape(bs, past, -1), cv_a.reshape(bs, past, -1),
               cache_diff_k[:, l].reshape(bs, past, H_D * 2 * DK_D).astype(BF16),
               cache_diff_v[:, l].reshape(bs, past, H_D * DV_D).astype(BF16), s0, m0)
        y_s, _ = _layer(y_s, bs, ss, mod[l], lw, lam_init, smask, gfin, final, ctx, tables)

    stack = lambda lst, shape: jnp.stack([a.reshape((bp,) + shape) for a in lst], axis=1)
    return (y_p.reshape(bp, sp, D_MODEL), y_s.reshape(bs, ss, D_MODEL),
            stack(st[0], (sp, KV_RANK)), stack(st[1], (sp, ROPE_A)),
            stack(st[2], (sp, H_D, 2 * DK_D)), stack(st[3], (sp, H_D, DV_D)),
            jnp.stack(st[4], axis=1), jnp.stack(st[5], axis=1), jnp.stack(st[6], axis=1))
```

```python
import functools
import math

import jax
import jax.numpy as jnp
from jax import lax
from jax.experimental import pallas as pl
from jax.experimental.pallas import tpu as pltpu

F32 = jnp.float32
BF16 = jnp.bfloat16

D_MODEL = 1024
DEPTH = 2
GRID_W = 64
ROPE_BASE = 10000.0
EPS = 1e-6
H_M, DK_M, DV_M = 4, 64, 128
H_A, Q_RANK, KV_RANK, NOPE_A, ROPE_A, V_A = 8, 384, 256, 64, 32, 64
H_D, DK_D, DV_D = 4, 64, 128
D_FF = 4 * D_MODEL

LANE = 128
VMEM_LIMIT = 56 * 1024 * 1024
ROW_TILE = 256
Q_TILE = 256
CHUNK = 128
KEY_CHUNKS = 4
NEG = -1e30
LOG2E = 1.4426950408889634

C_MQ, C_MK, C_MV, C_MO, C_GI, C_GF = 0, 256, 512, 1024, 1536, 1664
M_COLS = 1792
C_A = 1792
A_COLS = 768
C_D = 2560
D_COLS = 1536
C_G = 4096
G_COLS = 3 * D_MODEL
IN_COLS_P = 7168
S_COLS = H_M * DV_M + LANE


def _dot(a, b):
    return jnp.dot(a, b, preferred_element_type=F32)


def _dot_nt(a, b):
    return lax.dot_general(a, b, (((1,), (1,)), ((), ())), preferred_element_type=F32)


def _dot_tn(a, b):
    return lax.dot_general(a, b, (((0,), (0,)), ((), ())), preferred_element_type=F32)


def _rms_rows(x, g):
    return x * lax.rsqrt(jnp.mean(x * x, axis=-1, keepdims=True) + EPS) * g


def _rope_blocks(x, cos, sa, sb, shift):
    outs = []
    for b in range(x.shape[1] // LANE):
        xb = x[:, b * LANE:(b + 1) * LANE]
        outs.append(xb * cos + pltpu.roll(xb, shift, 1) * sa + pltpu.roll(xb, LANE - shift, 1) * sb)
    return outs[0] if len(outs) == 1 else jnp.concatenate(outs, axis=1)


def _params(sem):
    return pltpu.CompilerParams(dimension_semantics=sem, vmem_limit_bytes=VMEM_LIMIT)


def _const_spec(shape):
    nd = len(shape)
    return pl.BlockSpec(shape, lambda *_: (0,) * nd, pipeline_mode=pl.Buffered(1))


def _mod_kernel(c_ref, w_ref, b_ref, o_ref):
    c = c_ref[...]
    s = (c * jax.nn.sigmoid(c)).astype(BF16)
    o_ref[...] = _dot(s, w_ref[...].astype(BF16)) + b_ref[...]


def _modulation(cond8, w_mod, b_mod):
    tn = 1536
    n6 = 6 * D_MODEL
    return pl.pallas_call(
        _mod_kernel,
        grid=(DEPTH, n6 // tn),
        in_specs=[
            pl.BlockSpec((8, D_MODEL), lambda l, j: (0, 0)),
            pl.BlockSpec((None, D_MODEL, tn), lambda l, j: (l, 0, j)),
            pl.BlockSpec((None, 1, tn), lambda l, j: (l, 0, j)),
        ],
        out_specs=pl.BlockSpec((None, 8, tn), lambda l, j: (l, 0, j)),
        out_shape=jax.ShapeDtypeStruct((DEPTH, 8, n6), F32),
        compiler_params=_params(("parallel", "parallel")),
        name="modulation",
    )(cond8, w_mod, b_mod.reshape(DEPTH, 1, n6))


def _inproj_kernel(latent, *refs):
    if latent:
        (x_ref, mod_ref, g1_ref, w_ref, gq_ref, gkv_ref, wq_ref, wk_ref, wv_ref, tab_ref,
         m_ref, q_ref, k_ref, va_ref, qd_ref, kd_ref, vd_ref, gate_ref) = refs
    else:
        (x_ref, mod_ref, g1_ref, w_ref, gq_ref, gkv_ref, wq_ref, wk_ref, wv_ref,
         m_ref, q_ref, k_ref, va_ref, qd_ref, kd_ref, vd_ref, gate_ref,
         ckv_ref, akr_ref, kdraw_ref, vdraw_ref) = refs
    d = D_MODEL
    x = x_ref[...]
    mod = mod_ref[...]
    h = (_rms_rows(x, g1_ref[...]) * (1.0 + mod[:, d:2 * d]) + mod[:, 0:d]).astype(BF16)

    m_ref[...] = _dot(h, w_ref[:, 0:M_COLS])

    za = _dot(h, w_ref[:, C_A:C_A + A_COLS])
    acq = za[:, 0:Q_RANK]
    ackv = za[:, Q_RANK:Q_RANK + KV_RANK]
    akr = za[:, Q_RANK + KV_RANK:A_COLS]
    q = _dot(_rms_rows(acq, gq_ref[...]).astype(BF16), wq_ref[...])
    ckv = _rms_rows(ackv, gkv_ref[...])
    ckv_b = ckv.astype(BF16)
    kn = _dot(ckv_b, wk_ref[...])
    if not latent:
        ckv_ref[...] = ckv
        akr_ref[...] = akr
    else:
        tab = tab_ref[...]
        cq, saq, sbq = tab[:, 0:128], tab[:, 128:256], tab[:, 256:384]
        q = _rope_blocks(q, cq, saq, sbq, ROPE_A // 2)
        akr = _rope_blocks(akr, cq, saq, sbq, ROPE_A // 2)
    q_ref[...] = (q * (LOG2E * (NOPE_A + ROPE_A) ** -0.5)).astype(BF16)
    k_ref[...] = (kn + jnp.concatenate([akr] * H_A, axis=1)).astype(BF16)
    va_ref[...] = _dot(ckv_b, wv_ref[...]).astype(BF16)

    zd = _dot(h, w_ref[:, C_D:C_D + D_COLS])
    dq, dk, dv = zd[:, 0:512], zd[:, 512:1024], zd[:, 1024:1536]
    if not latent:
        kdraw_ref[...] = dk
        vdraw_ref[...] = dv
    else:
        cd, sad, sbd = tab[:, 384:512], tab[:, 512:640], tab[:, 640:768]
        dq = _rope_blocks(dq, cd, sad, sbd, DK_D // 2)
        dk = _rope_blocks(dk, cd, sad, sbd, DK_D // 2)
    qd_ref[...] = (dq * (LOG2E * DK_D ** -0.5)).astype(BF16)
    kd_ref[...] = dk.astype(BF16)
    vd_ref[...] = dv.astype(BF16)

    gate_ref[...] = _dot(h, w_ref[:, C_G:C_G + G_COLS])


def _inproj(x2, mod_l, rows_per_cond, lw, tables):
    latent = tables is not None
    t = x2.shape[0]
    tm = ROW_TILE
    tiles_per_cond = rows_per_cond // tm
    first_row = 1 if latent else 0

    def row(i):
        return (i, 0)

    in_specs = [
        pl.BlockSpec((tm, D_MODEL), row),
        pl.BlockSpec((None, 1, 6 * D_MODEL), lambda i: (first_row + i // tiles_per_cond, 0, 0)),
        _const_spec((1, D_MODEL)),
        _const_spec((D_MODEL, IN_COLS_P)),
        _const_spec((1, Q_RANK)),
        _const_spec((1, KV_RANK)),
        _const_spec((Q_RANK, H_A * LANE)),
        _const_spec((KV_RANK, H_A * LANE)),
        _const_spec((KV_RANK, H_A * V_A)),
    ]
    args = [x2, mod_l, lw["norm1_g"], lw["w_in"], lw["mla_q_norm_g"], lw["mla_kv_norm_g"],
            lw["wq"], lw["wk"], lw["wv"]]
    if latent:
        n_tab = tables.shape[0] // tm
        in_specs.append(pl.BlockSpec((tm, 768), lambda i: (i % n_tab, 0)))
        args.append(tables)
    widths = [(M_COLS, F32), (H_A * LANE, BF16), (H_A * LANE, BF16), (H_A * V_A, BF16),
              (512, BF16), (512, BF16), (512, BF16), (G_COLS, F32)]
    if not latent:
        widths += [(KV_RANK, F32), (LANE, F32), (512, F32), (512, F32)]
    return pl.pallas_call(
        functools.partial(_inproj_kernel, latent),
        grid=(t // tm,),
        in_specs=in_specs,
        out_specs=[pl.BlockSpec((tm, w), row) for w, _ in widths],
        out_shape=[jax.ShapeDtypeStruct((t, w), dt) for w, dt in widths],
        compiler_params=_params(("parallel",)),
        name="inproj_lat" if latent else "inproj_ctx",
    )(*args)


def _mlstm_direction(z, s_prev, m_prev, bi, bfg, smask, d):
    L = CHUNK
    q = (z[:, C_MQ:C_MQ + 256] * DK_M ** -0.5).astype(BF16)
    k = z[:, C_MK:C_MK + 256]
    v = z[:, C_MV:C_MV + 512].astype(BF16)
    gi = z[:, C_GI:C_GI + LANE] + bi
    xf = z[:, C_GF:C_GF + LANE] + bfg
    lf = jnp.minimum(xf, 0.0) - jnp.log(1.0 + jnp.exp(-jnp.abs(xf)))

    r_i = lax.broadcasted_iota(jnp.int32, (L, L), 0)
    c_i = lax.broadcasted_iota(jnp.int32, (L, L), 1)
    mask = (c_i <= r_i) if d == 0 else (c_i >= r_i)
    tri = jnp.where(mask, 1.0, 0.0).astype(BF16)
    lf_hi = lf.astype(BF16)
    r1 = lf - lf_hi.astype(F32)
    lf_mid = r1.astype(BF16)
    lf_lo = (r1 - lf_mid.astype(F32)).astype(BF16)
    b = _dot(tri, lf_hi) + _dot(tri, lf_mid) + _dot(tri, lf_lo)
    g = gi - b
    g_t = g.T
    last = L - 1 if d == 0 else 0

    qs = _dot(q, s_prev.astype(BF16))
    lane256 = lax.broadcasted_iota(jnp.int32, (L, 256), 1)
    lane128 = lax.broadcasted_iota(jnp.int32, (1, LANE), 1)
    kb = k.astype(BF16)

    h_out = []
    w_cols = []
    w_old = []
    m_new = m_prev
    for hd in range(H_M):
        j = d * H_M + hd
        b_col = b[:, j:j + 1]
        g_col = g[:, j:j + 1]
        g_row = g_t[j:j + 1, :]
        mp = m_prev[:, j:j + 1]
        gm = jnp.where(mask, g_row, NEG)
        u_col = jnp.maximum(jnp.max(gm, axis=-1, keepdims=True), mp)
        dmat = jnp.exp(gm - u_col)
        q_h = jnp.where((lane256 >= hd * DK_M) & (lane256 < (hd + 1) * DK_M), q, jnp.zeros_like(q))
        a = dmat * _dot_nt(q_h, kb)
        den_i = jnp.sum(a, axis=-1, keepdims=True)
        num_i = _dot(a.astype(BF16), v[:, hd * DV_M:(hd + 1) * DV_M])
        w_int = jnp.exp(mp - u_col)
        num = w_int * qs[:, hd * DV_M:(hd + 1) * DV_M] + num_i
        den = w_int * qs[:, H_M * DV_M + j:H_M * DV_M + j + 1] + den_i
        h_out.append(num / jnp.maximum(jnp.abs(den), jnp.exp(-(b_col + u_col))))
        u_l = u_col[last:last + 1, :]
        b_l = b_col[last:last + 1, :]
        m_new = jnp.where(lane128 == j, b_l + u_l, m_new)
        w_cols.append(jnp.exp(g_col - u_l))
        w_old.append(jnp.exp(mp - u_l))

    w_exp = jnp.where(lane256 < 64, w_cols[0],
                      jnp.where(lane256 < 128, w_cols[1], jnp.where(lane256 < 192, w_cols[2], w_cols[3])))
    k_w = (k * w_exp).astype(BF16)
    v_aug = jnp.concatenate([v, jnp.ones((L, LANE), BF16)], axis=1)
    upd = _dot_tn(k_w, v_aug)
    n_scale = jnp.zeros((1, LANE), F32)
    for hd in range(H_M):
        n_scale = jnp.where(lane128 == d * H_M + hd, w_old[hd], n_scale)
    scale_row = jnp.concatenate([jnp.broadcast_to(w, (1, DV_M)) for w in w_old] + [n_scale], axis=1)
    s_new = scale_row * s_prev + smask * upd
    return jnp.concatenate(h_out, axis=1), s_new, m_new


def _mlstm_kernel(zf_ref, zb_ref, s0_ref, m0_ref, bi_ref, bf_ref, smask_ref,
                  hf_ref, hb_ref, st_ref, mt_ref, s_scr, m_scr):
    i = pl.program_id(1)

    @pl.when(i == 0)
    def _():
        s_scr[...] = s0_ref[...]
        m_scr[...] = m0_ref[...]

    bi = bi_ref[...]
    bfg = bf_ref[...]
    for d, (z_ref, h_ref) in enumerate(((zf_ref, hf_ref), (zb_ref, hb_ref))):
        h, s_new, m_new = _mlstm_direction(z_ref[...], s_scr[d], m_scr[d], bi, bfg, smask_ref[d], d)
        h_ref[...] = h
        s_scr[d] = s_new
        m_scr[d] = m_new

    @pl.when(i == pl.num_programs(1) - 1)
    def _():
        st_ref[...] = s_scr[...]
        mt_ref[...] = m_scr[...]


def _mlstm(m_in, s0, m0, lw, smask, batch, n_tok):
    nc = n_tok // CHUNK
    t = batch * n_tok
    state_spec = pl.BlockSpec((None, 2, 256, S_COLS), lambda b, i: (b, 0, 0, 0))
    m_spec = pl.BlockSpec((None, 2, 1, LANE), lambda b, i: (b, 0, 0, 0))
    fwd = lambda b, i: (b * nc + i, 0)
    bwd = lambda b, i: (b * nc + nc - 1 - i, 0)
    return pl.pallas_call(
        _mlstm_kernel,
        grid=(batch, nc),
        in_specs=[
            pl.BlockSpec((CHUNK, M_COLS), fwd),
            pl.BlockSpec((CHUNK, M_COLS), bwd),
            state_spec, m_spec,
            pl.BlockSpec((1, LANE), lambda b, i: (0, 0)),
            pl.BlockSpec((1, LANE), lambda b, i: (0, 0)),
            pl.BlockSpec((2, 256, S_COLS), lambda b, i: (0, 0, 0)),
        ],
        out_specs=[
            pl.BlockSpec((CHUNK, H_M * DV_M), fwd),
            pl.BlockSpec((CHUNK, H_M * DV_M), bwd),
            state_spec, m_spec,
        ],
        out_shape=[
            jax.ShapeDtypeStruct((t, H_M * DV_M), F32),
            jax.ShapeDtypeStruct((t, H_M * DV_M), F32),
            jax.ShapeDtypeStruct((batch, 2, 256, S_COLS), F32),
            jax.ShapeDtypeStruct((batch, 2, 1, LANE), F32),
        ],
        scratch_shapes=[pltpu.VMEM((2, 256, S_COLS), F32), pltpu.VMEM((2, 1, LANE), F32)],
        compiler_params=_params(("parallel", "arbitrary")),
        name="mlstm",
    )(m_in, m_in, s0, m0, lw["gate_bi"], lw["gate_bf"], smask)


def _pipelined_attention(kernel_fn, inputs, extra_specs, batch, n_q, n_kv, n_groups, widths, n_maps, name):
    q_w, k_w, v_w, out_w = widths
    tq = min(Q_TILE, n_q)
    nq = n_q // tq
    n_units = batch * n_groups * nq
    nxt = lambda s: jnp.minimum(s, n_units - 1)
    cur = lambda s: jnp.maximum(s - 1, 0)
    row_blk = lambda u: (u // (n_groups * nq)) * nq + u % nq
    grp = lambda u: (u // nq) % n_groups
    bat = lambda u: u // (n_groups * nq)
    score = pltpu.VMEM((n_maps, n_kv, tq), F32)
    cmax = pltpu.VMEM((n_maps, 1, tq), F32)
    return pl.pallas_call(
        functools.partial(kernel_fn, nq),
        grid=(n_units + 1,),
        in_specs=[
            pl.BlockSpec((tq, q_w), lambda s: (row_blk(nxt(s)), grp(nxt(s)))),
            pl.BlockSpec((n_kv, k_w), lambda s: (bat(nxt(s)), grp(nxt(s)))),
            pl.BlockSpec((n_kv, v_w), lambda s: (bat(cur(s)), grp(cur(s)))),
        ] + extra_specs(lambda s: grp(cur(s))),
        out_specs=pl.BlockSpec((tq, out_w), lambda s: (row_blk(cur(s)), grp(cur(s)))),
        out_shape=jax.ShapeDtypeStruct((batch * n_q, n_groups * out_w), BF16),
        scratch_shapes=[score, cmax, score, cmax, pltpu.VMEM((v_w, n_kv), BF16)],
        compiler_params=_params(("arbitrary",)),
        name=name,
    )(*inputs)


def _pipeline_prologue(nq, v_ref, s_b, m_b, vt_scr):
    s = pl.program_id(0)

    @pl.when(s == 0)
    def _():
        s_b[...] = jnp.zeros_like(s_b)
        m_b[...] = jnp.zeros_like(m_b)

    @pl.when(lax.rem(jnp.maximum(s - 1, 0), nq) == 0)
    def _():
        vt_scr[...] = v_ref[...].T

    return lax.rem(s, 2)


def _key_chunks(n_kv):
    blocks = n_kv // LANE
    n = min(KEY_CHUNKS, blocks)
    edges = [(i * blocks // n) * LANE for i in range(n + 1)]
    return list(zip(edges[:-1], edges[1:]))


def _score_map(k_of, q, vt_of, s_n, m_n, s_c, m_c, a):
    mc = m_c[a]
    mx = l = acc = None
    for c0, c1 in _key_chunks(s_n.shape[1]):
        st = _dot_nt(k_of(c0, c1), q)
        s_n[a, c0:c1, :] = st
        cm = jnp.max(st, axis=0, keepdims=True)
        mx = cm if mx is None else jnp.maximum(mx, cm)
        p = jnp.exp2(s_c[a, c0:c1, :] - mc)
        ps = jnp.sum(p, axis=0, keepdims=True)
        l = ps if l is None else l + ps
        pv = _dot(vt_of(c0, c1), p.astype(BF16))
        acc = pv if acc is None else acc + pv
    m_n[a] = mx
    return acc, l


def _mla_kernel(nq, q_ref, k_ref, v_ref, o_ref, s_a, m_a, s_b, m_b, vt_scr):
    parity = _pipeline_prologue(nq, v_ref, s_b, m_b, vt_scr)

    def body(s_n, m_n, s_c, m_c):
        outs = []
        for e in range(2):
            acc, l = _score_map(lambda c0, c1: k_ref[c0:c1, e * LANE:(e + 1) * LANE],
                                q_ref[:, e * LANE:(e + 1) * LANE],
                                lambda c0, c1: vt_scr[e * V_A:(e + 1) * V_A, c0:c1],
                                s_n, m_n, s_c, m_c, e)
            outs.append(acc / l)
        o_ref[...] = jnp.concatenate(outs, axis=0).T.astype(BF16)

    @pl.when(parity == 0)
    def _():
        body(s_a, m_a, s_b, m_b)

    @pl.when(parity == 1)
    def _():
        body(s_b, m_b, s_a, m_a)


def _mla_attention(q, k, v, batch, n_q, n_kv):
    return _pipelined_attention(_mla_kernel, (q, k, v), lambda cur_grp: [], batch, n_q, n_kv, H_A // 2,
                                (2 * LANE, 2 * LANE, LANE, LANE), 2, "mla_attention")


def _diff_kernel(lam_init, nq, q_ref, k_ref, v_ref, lam_ref, g_ref, o_ref, s_a, m_a, s_b, m_b, vt_scr):
    parity = _pipeline_prologue(nq, v_ref, s_b, m_b, vt_scr)

    def body(s_n, m_n, s_c, m_c):
        lane = lax.broadcasted_iota(jnp.int32, (1, LANE), 1)
        q = q_ref[...]
        zero = jnp.zeros_like(q)
        lv = lam_ref[...]
        lam = (jnp.exp(jnp.sum(lv[0:1] * lv[1:2], axis=-1, keepdims=True))
               - jnp.exp(jnp.sum(lv[2:3] * lv[3:4], axis=-1, keepdims=True)) + lam_init)
        k_of = lambda c0, c1: k_ref[c0:c1, :]
        vt_of = lambda c0, c1: vt_scr[:, c0:c1]
        acc1, l1 = _score_map(k_of, jnp.where(lane < DK_D, q, zero), vt_of, s_n, m_n, s_c, m_c, 0)
        acc2, l2 = _score_map(k_of, jnp.where(lane >= DK_D, q, zero), vt_of, s_n, m_n, s_c, m_c, 1)
        o = (acc1 / l1 - acc2 * (lam / l2)).T
        o_ref[...] = (_rms_rows(o, g_ref[...]) * (1.0 - lam_init)).astype(BF16)

    @pl.when(parity == 0)
    def _():
        body(s_a, m_a, s_b, m_b)

    @pl.when(parity == 1)
    def _():
        body(s_b, m_b, s_a, m_a)


def _diff_attention(q, k, v, lam_v, g, lam_init, batch, n_q, n_kv):
    extra = lambda cur_grp: [pl.BlockSpec((4, DK_D), lambda s: (0, 0)),
                             pl.BlockSpec((1, LANE), lambda s: (0, cur_grp(s)))]
    return _pipelined_attention(functools.partial(_diff_kernel, lam_init), (q, k, v, lam_v, g), extra,
                                batch, n_q, n_kv, H_D, (LANE, LANE, LANE, LANE), 2, "diff_attention")


def _merge_kernel(final, x_ref, mod_ref, hf_ref, hb_ref, mo_ref, oa_ref, od_ref, gate_ref,
                  gm_ref, wbm_ref, wba_ref, wbd_ref, wout_ref, g2_ref, wff1_ref, wff2_ref, gfin_ref, o_ref):
    d = D_MODEL
    mod = mod_ref[...]
    hm = hf_ref[...] + hb_ref[...]
    gm = gm_ref[...]
    o_m = jnp.concatenate(
        [_rms_rows(hm[:, h * DV_M:(h + 1) * DV_M], gm[:, h * DV_M:(h + 1) * DV_M]) for h in range(H_M)], axis=1)
    o_m = (o_m * jax.nn.sigmoid(mo_ref[...])).astype(BF16)
    gate = jax.nn.sigmoid(gate_ref[...])
    y = (gate[:, 0:d] * _dot(o_m, wbm_ref[...]) + gate[:, d:2 * d] * _dot(oa_ref[...], wba_ref[...])
         + gate[:, 2 * d:3 * d] * _dot(od_ref[...], wbd_ref[...]))
    x = x_ref[...] + mod[:, 2 * d:3 * d] * _dot(y.astype(BF16), wout_ref[...])
    h2 = (_rms_rows(x, g2_ref[...]) * (1.0 + mod[:, 4 * d:5 * d]) + mod[:, 3 * d:4 * d]).astype(BF16)
    f = jnp.maximum(_dot(h2, wff1_ref[...]), 0.0)
    x = x + mod[:, 5 * d:6 * d] * _dot((f * f).astype(BF16), wff2_ref[...])
    if final:
        x = _rms_rows(x, gfin_ref[...])
    o_ref[...] = x


def _merge(x2, mod_l, rows_per_cond, first_row, m_in, h_f, h_b, o_a, o_d, gate, lw, gfin, final):
    t = x2.shape[0]
    tm = ROW_TILE
    tiles_per_cond = rows_per_cond // tm
    row = lambda i: (i, 0)
    in_specs = [
        pl.BlockSpec((tm, D_MODEL), row),
        pl.BlockSpec((None, 1, 6 * D_MODEL), lambda i: (first_row + i // tiles_per_cond, 0, 0)),
        pl.BlockSpec((tm, 512), row),
        pl.BlockSpec((tm, 512), row),
        pl.BlockSpec((tm, 512), lambda i: (i, C_MO // 512)),
        pl.BlockSpec((tm, 512), row),
        pl.BlockSpec((tm, 512), row),
        pl.BlockSpec((tm, G_COLS), row),
        _const_spec((1, 512)),
        _const_spec((512, D_MODEL)),
        _const_spec((512, D_MODEL)),
        _const_spec((512, D_MODEL)),
        _const_spec((D_MODEL, D_MODEL)),
        _const_spec((1, D_MODEL)),
        _const_spec((D_MODEL, D_FF)),
        _const_spec((D_FF, D_MODEL)),
        _const_spec((1, D_MODEL)),
    ]
    return pl.pallas_call(
        functools.partial(_merge_kernel, final),
        grid=(t // tm,),
        in_specs=in_specs,
        out_specs=pl.BlockSpec((tm, D_MODEL), row),
        out_shape=jax.ShapeDtypeStruct((t, D_MODEL), F32),
        compiler_params=_params(("parallel",)),
        name="merge_mlp",
    )(x2, mod_l, h_f, h_b, m_in, o_a, o_d, gate, lw["mlstm_norm_g"], lw["w_br_mlstm"], lw["w_br_mla"],
      lw["w_br_diff"], lw["w_out"], lw["norm2_g"], lw["w_ff1"], lw["w_ff2"], gfin)


def _prep_layer_weights(l, w_in, mlstm_gate_b, norm1_g, mlstm_norm_g, mla_q_norm_g, mla_w_q_up, mla_kv_norm_g,
                        mla_w_kv_up, diff_lambda, diff_norm_g, w_br_mlstm, w_br_mla, w_br_diff, w_out,
                        norm2_g, w_ff1, w_ff2):
    w = w_in[l]
    z = lambda n: jnp.zeros((D_MODEL, n), F32)
    c_mg = 2 * H_M * DK_M + 2 * H_M * DV_M
    c_acq = c_mg + 4 * H_M
    c_akr = c_acq + Q_RANK + KV_RANK
    c_dq = c_akr + ROPE_A
    w_p = jnp.concatenate([
        w[:, :c_mg], w[:, c_mg:c_mg + 8], z(LANE - 8), w[:, c_mg + 8:c_mg + 16], z(LANE - 8),
        w[:, c_acq:c_akr], z(NOPE_A), w[:, c_akr:c_dq], z(LANE - NOPE_A - ROPE_A), w[:, c_dq:]], axis=1)
    gb = mlstm_gate_b[l]
    pad8 = lambda a: jnp.concatenate([a, jnp.zeros((LANE - 8,), F32)])[None, :]
    wq = mla_w_q_up[l].reshape(Q_RANK, H_A, NOPE_A + ROPE_A)
    wq = jnp.pad(wq, ((0, 0), (0, 0), (0, LANE - NOPE_A - ROPE_A))).reshape(Q_RANK, H_A * LANE)
    wkv = mla_w_kv_up[l].reshape(KV_RANK, H_A, NOPE_A + V_A)
    wk = jnp.pad(wkv[:, :, :NOPE_A], ((0, 0), (0, 0), (0, LANE - NOPE_A))).reshape(KV_RANK, H_A * LANE)
    wv = wkv[:, :, NOPE_A:].reshape(KV_RANK, H_A * V_A)
    return {
        "w_in": w_p.astype(BF16),
        "gate_bi": pad8(gb[:8]), "gate_bf": pad8(gb[8:]),
        "norm1_g": norm1_g[l][None, :], "norm2_g": norm2_g[l][None, :],
        "mlstm_norm_g": mlstm_norm_g[l][None, :],
        "mla_q_norm_g": mla_q_norm_g[l][None, :], "mla_kv_norm_g": mla_kv_norm_g[l][None, :],
        "wq": wq.astype(BF16), "wk": wk.astype(BF16), "wv": wv.astype(BF16),
        "diff_lambda": diff_lambda[l], "diff_norm_g": diff_norm_g[l][None, :],
        "w_br_mlstm": w_br_mlstm[l].astype(BF16), "w_br_mla": w_br_mla[l].astype(BF16),
        "w_br_diff": w_br_diff[l].astype(BF16), "w_out": w_out[l].astype(BF16),
        "w_ff1": w_ff1[l].astype(BF16), "w_ff2": w_ff2[l].astype(BF16),
    }


def _rope_tables(n_tokens):
    rows = n_tokens // GRID_W
    row = jnp.repeat(jnp.arange(rows, dtype=F32), GRID_W)
    col = jnp.tile(jnp.arange(GRID_W, dtype=F32), rows)

    def cs(dim):
        quarter = dim // 4
        inv = ROPE_BASE ** (-jnp.arange(quarter, dtype=F32) / quarter)
        ang = jnp.concatenate([row[:, None] * inv, col[:, None] * inv], axis=-1)
        return jnp.cos(ang), jnp.sin(ang)

    one = lambda n: jnp.ones((n_tokens, n), F32)
    zero = lambda n: jnp.zeros((n_tokens, n), F32)
    ca, sa = cs(ROPE_A)
    cd, sd = cs(DK_D)
    return jnp.concatenate([
        one(64), ca, ca, one(32),
        zero(80), sa, zero(32),
        zero(64), -sa, zero(48),
        cd, cd, cd, cd,
        zero(32), sd, zero(32), sd,
        -sd, zero(32), -sd, zero(32)], axis=1)


def _state_mask():
    r = jnp.arange(256)[:, None] // DK_M
    c = jnp.arange(S_COLS)[None, :]
    diag = (c < H_M * DV_M) & (c // DV_M == r)
    return jnp.stack([(diag | (c == H_M * DV_M + d * H_M + r)) for d in range(2)]).astype(F32)


def _pack_state(c0, n0, m0):
    b = c0.shape[0]
    s = jnp.zeros((b, 2, 256, S_COLS), F32)
    m = jnp.zeros((b, 2, 1, LANE), F32)
    for h in range(H_M):
        s = s.at[:, :, h * DK_M:(h + 1) * DK_M, h * DV_M:(h + 1) * DV_M].set(c0[:, :, h])
        for d in range(2):
            s = s.at[:, d, h * DK_M:(h + 1) * DK_M, H_M * DV_M + d * H_M + h].set(n0[:, d, h])
            m = m.at[:, d, 0, d * H_M + h].set(m0[:, d, h])
    return s, m


def _unpack_state(s, m):
    c = jnp.stack([s[:, :, h * DK_M:(h + 1) * DK_M, h * DV_M:(h + 1) * DV_M] for h in range(H_M)], axis=2)
    n = jnp.stack([jnp.stack([s[:, d, h * DK_M:(h + 1) * DK_M, H_M * DV_M + d * H_M + h] for h in range(H_M)], axis=1)
                   for d in range(2)], axis=1)
    mm = jnp.stack([m[:, d, 0, d * H_M:(d + 1) * H_M] for d in range(2)], axis=1)
    return c, n, mm


def _layer(x2, batch, n_tok, mod_l, lw, lam_init, smask, gfin, final, ctx, tables):
    latent = ctx is not None
    rows_per_cond = n_tok if latent else batch * n_tok
    first_row = 1 if latent else 0
    outs = _inproj(x2, mod_l, rows_per_cond, lw, tables)
    m_in, q_a, k_a, v_a, q_d, k_d, v_d, gate = outs[:8]
    if latent:
        ck_a, cv_a, ck_d, cv_d, s0, m0 = ctx
        cat = lambda c, new: jnp.concatenate(
            [c, new.reshape(batch, n_tok, new.shape[-1])], axis=1).reshape(-1, new.shape[-1])
        k_a, v_a, k_d, v_d = cat(ck_a, k_a), cat(cv_a, v_a), cat(ck_d, k_d), cat(cv_d, v_d)
        n_kv = ck_a.shape[1] + n_tok
    else:
        s0 = jnp.zeros((batch, 2, 256, S_COLS), F32)
        m0 = jnp.zeros((batch, 2, 1, LANE), F32)
        n_kv = n_tok
    h_f, h_b, s_t, m_t = _mlstm(m_in, s0, m0, lw, smask, batch, n_tok)
    o_a = _mla_attention(q_a, k_a, v_a, batch, n_tok, n_kv)
    o_d = _diff_attention(q_d, k_d, v_d, lw["diff_lambda"], lw["diff_norm_g"], lam_init, batch, n_tok, n_kv)
    x_new = _merge(x2, mod_l, rows_per_cond, first_row, m_in, h_f, h_b, o_a, o_d, gate, lw, gfin, final)
    new_ctx = None
    if not latent:
        ckv, akr, kd_raw, vd_raw = outs[8:]
        new_ctx = (ckv, akr[:, NOPE_A:NOPE_A + ROPE_A], kd_raw, vd_raw) + _unpack_state(s_t, m_t)
    return x_new, new_ctx


def _cache_kv_kernel(ckv_ref, kr_ref, wk_ref, wv_ref, k_ref, v_ref):
    ckv = ckv_ref[...].astype(BF16)
    k_ref[...] = (_dot(ckv, wk_ref[...]) + jnp.concatenate([kr_ref[...]] * H_A, axis=1)).astype(BF16)
    v_ref[...] = _dot(ckv, wv_ref[...]).astype(BF16)


def _cache_kv(ckv, kr, lw):
    t = ckv.shape[0]
    tm = min(ROW_TILE, t)
    row = lambda i: (i, 0)
    return pl.pallas_call(
        _cache_kv_kernel,
        grid=(t // tm,),
        in_specs=[pl.BlockSpec((tm, KV_RANK), row), pl.BlockSpec((tm, LANE), row),
                  _const_spec((KV_RANK, H_A * LANE)), _const_spec((KV_RANK, H_A * V_A))],
        out_specs=[pl.BlockSpec((tm, H_A * LANE), row), pl.BlockSpec((tm, H_A * V_A), row)],
        out_shape=[jax.ShapeDtypeStruct((t, H_A * LANE), BF16), jax.ShapeDtypeStruct((t, H_A * V_A), BF16)],
        compiler_params=_params(("parallel",)),
        name="cache_kv",
    )(ckv, jnp.pad(kr, ((0, 0), (NOPE_A, LANE - NOPE_A - ROPE_A))), lw["wk"], lw["wv"])


def kernel(x_prompt, x_sample, c, cache_mla_ckv, cache_mla_krope, cache_diff_k, cache_diff_v, state_mlstm_C, state_mlstm_n, state_mlstm_m, c_ctx, w_mod, b_mod, norm1_g, w_in, mlstm_gate_b, mlstm_norm_g, mla_q_norm_g, mla_w_q_up, mla_kv_norm_g, mla_w_kv_up, diff_lambda, diff_norm_g, w_br_mlstm, w_br_mla, w_br_diff, w_out, norm2_g, w_ff1, w_ff2, final_norm_g):
    bp, sp, _ = x_prompt.shape
    bs, ss, _ = x_sample.shape
    past = cache_mla_ckv.shape[2]
    assert bs + 1 <= 8 and sp % ROW_TILE == 0 and ss % ROW_TILE == 0 and ss % GRID_W == 0

    cond8 = jnp.concatenate([c_ctx[None, :], c, jnp.zeros((8 - 1 - bs, D_MODEL), F32)], axis=0)
    mod = _modulation(cond8, w_mod, b_mod).reshape(DEPTH, 8, 1, 6 * D_MODEL)
    tables = _rope_tables(ss)
    smask = _state_mask()
    gfin = final_norm_g[None, :]

    y_p = x_prompt.reshape(bp * sp, D_MODEL)
    y_s = x_sample.reshape(bs * ss, D_MODEL)
    st = [[] for _ in range(7)]
    for l in range(DEPTH):
        lw = _prep_layer_weights(l, w_in, mlstm_gate_b, norm1_g, mlstm_norm_g, mla_q_norm_g, mla_w_q_up,
                                 mla_kv_norm_g, mla_w_kv_up, diff_lambda, diff_norm_g, w_br_mlstm, w_br_mla,
                                 w_br_diff, w_out, norm2_g, w_ff1, w_ff2)
        lam_init = 0.8 - 0.6 * math.exp(-0.3 * l)
        final = l == DEPTH - 1
        y_p, new_ctx = _layer(y_p, bp, sp, mod[l], lw, lam_init, smask, gfin, final, None, None)
        for lst, a in zip(st, new_ctx):
            lst.append(a)
        ck_a, cv_a = _cache_kv(cache_mla_ckv[:, l].reshape(bs * past, KV_RANK),
                               cache_mla_krope[:, l].reshape(bs * past, ROPE_A), lw)
        s0, m0 = _pack_state(state_mlstm_C[:, l], state_mlstm_n[:, l], state_mlstm_m[:, l])
        ctx = (ck_a.reshape(bs, past, -1), cv_a.reshape(bs, past, -1),
               cache_diff_k[:, l].reshape(bs, past, H_D * 2 * DK_D).astype(BF16),
               cache_diff_v[:, l].reshape(bs, past, H_D * DV_D).astype(BF16), s0, m0)
        y_s, _ = _layer(y_s, bs, ss, mod[l], lw, lam_init, smask, gfin, final, ctx, tables)

    stack = lambda lst, shape: jnp.stack([a.reshape((bp,) + shape) for a in lst], axis=1)
    return (y_p.reshape(bp, sp, D_MODEL), y_s.reshape(bs, ss, D_MODEL),
            stack(st[0], (sp, KV_RANK)), stack(st[1], (sp, ROPE_A)),
            stack(st[2], (sp, H_D, 2 * DK_D)), stack(st[3], (sp, H_D, DV_D)),
            jnp.stack(st[4], axis=1), jnp.stack(st[5], axis=1), jnp.stack(st[6], axis=1))
```

```python
import functools
import math

import jax
import jax.numpy as jnp
from jax import lax
from jax.experimental import pallas as pl
from jax.experimental.pallas import tpu as pltpu

F32 = jnp.float32
BF16 = jnp.bfloat16

D_MODEL = 1024
DEPTH = 2
GRID_W = 64
ROPE_BASE = 10000.0
EPS = 1e-6
H_M, DK_M, DV_M = 4, 64, 128
H_A, Q_RANK, KV_RANK, NOPE_A, ROPE_A, V_A = 8, 384, 256, 64, 32, 64
H_D, DK_D, DV_D = 4, 64, 128
D_FF = 4 * D_MODEL

LANE = 128
VMEM_LIMIT = 56 * 1024 * 1024
ROW_TILE = 256
Q_TILE = 256
CHUNK = 128
KEY_CHUNKS = 4
NEG = -1e30
LOG2E = 1.4426950408889634

C_MQ, C_MK, C_MV, C_MO, C_GI, C_GF = 0, 256, 512, 1024, 1536, 1664
M_COLS = 1792
C_A = 1792
A_COLS = 768
C_D = 2560
D_COLS = 1536
C_G = 4096
G_COLS = 3 * D_MODEL
IN_COLS_P = 7168
S_COLS = H_M * DV_M + LANE


def _dot(a, b):
    return jnp.dot(a, b, preferred_element_type=F32)


def _dot_nt(a, b):
    return lax.dot_general(a, b, (((1,), (1,)), ((), ())), preferred_element_type=F32)


def _dot_tn(a, b):
    return lax.dot_general(a, b, (((0,), (0,)), ((), ())), preferred_element_type=F32)


def _rms_rows(x, g):
    return x * lax.rsqrt(jnp.mean(x * x, axis=-1, keepdims=True) + EPS) * g


def _rope_blocks(x, cos, sa, sb, shift):
    outs = []
    for b in range(x.shape[1] // LANE):
        xb = x[:, b * LANE:(b + 1) * LANE]
        outs.append(xb * cos + pltpu.roll(xb, shift, 1) * sa + pltpu.roll(xb, LANE - shift, 1) * sb)
    return outs[0] if len(outs) == 1 else jnp.concatenate(outs, axis=1)


def _params(sem):
    return pltpu.CompilerParams(dimension_semantics=sem, vmem_limit_bytes=VMEM_LIMIT)


def _layer_spec(l, shape):
    nd = len(shape)
    return pl.BlockSpec((None,) + shape, lambda *_: (l,) + (0,) * nd, pipeline_mode=pl.Buffered(1))


def _mod_spec(l, first_row, tiles_per_cond):
    return pl.BlockSpec((None, None, 1, 6 * D_MODEL), lambda i: (l, first_row + i // tiles_per_cond, 0, 0))


_ANY = pl.BlockSpec(memory_space=pl.ANY)


def _mod_kernel(c_ref, w_ref, b_ref, o_ref):
    c = c_ref[...]
    s = (c * jax.nn.sigmoid(c)).astype(BF16)
    o_ref[...] = _dot(s, w_ref[...].astype(BF16)) + b_ref[...]


def _modulation(cond8, w_mod, b_mod):
    tn = 1536
    n6 = 6 * D_MODEL
    return pl.pallas_call(
        _mod_kernel,
        grid=(DEPTH, n6 // tn),
        in_specs=[
            pl.BlockSpec((8, D_MODEL), lambda l, j: (0, 0)),
            pl.BlockSpec((None, D_MODEL, tn), lambda l, j: (l, 0, j)),
            pl.BlockSpec((None, 1, tn), lambda l, j: (l, 0, j)),
        ],
        out_specs=pl.BlockSpec((None, 8, tn), lambda l, j: (l, 0, j)),
        out_shape=jax.ShapeDtypeStruct((DEPTH, 8, n6), F32),
        compiler_params=_params(("parallel", "parallel")),
        name="modulation",
    )(cond8, w_mod, b_mod.reshape(DEPTH, 1, n6))


def _inproj_kernel(latent, n_aliased, *refs):
    (x_ref, mod_ref, g1_ref, w_ref, gq_ref, gkv_ref, wq_ref, wk_ref, wv_ref) = refs[:9]
    refs = refs[9:]
    if latent:
        tab_ref = refs[0]
        (m_ref, q_ref, k_ref, va_ref, qd_ref, kd_ref, vd_ref, gate_ref) = refs[1:]
    else:
        (m_ref, q_ref, k_ref, va_ref, qd_ref, kd_ref, vd_ref, gate_ref,
         ckv_ref, akr_ref, kdraw_ref, vdraw_ref) = refs[n_aliased:]
    d = D_MODEL
    x = x_ref[...]
    mod = mod_ref[...]
    h = (_rms_rows(x, g1_ref[...]) * (1.0 + mod[:, d:2 * d]) + mod[:, 0:d]).astype(BF16)

    m_ref[...] = _dot(h, w_ref[:, 0:M_COLS])

    za = _dot(h, w_ref[:, C_A:C_A + A_COLS])
    acq = za[:, 0:Q_RANK]
    ackv = za[:, Q_RANK:Q_RANK + KV_RANK]
    akr = za[:, Q_RANK + KV_RANK:A_COLS]
    q = _dot(_rms_rows(acq, gq_ref[...]).astype(BF16), wq_ref[...])
    ckv = _rms_rows(ackv, gkv_ref[...])
    ckv_b = ckv.astype(BF16)
    kn = _dot(ckv_b, wk_ref[...])
    if not latent:
        ckv_ref[...] = ckv
        akr_ref[...] = akr
    else:
        tab = tab_ref[...]
        cq, saq, sbq = tab[:, 0:128], tab[:, 128:256], tab[:, 256:384]
        q = _rope_blocks(q, cq, saq, sbq, ROPE_A // 2)
        akr = _rope_blocks(akr, cq, saq, sbq, ROPE_A // 2)
    q_ref[...] = (q * (LOG2E * (NOPE_A + ROPE_A) ** -0.5)).astype(BF16)
    k_ref[...] = (kn + jnp.concatenate([akr] * H_A, axis=1)).astype(BF16)
    va_ref[...] = _dot(ckv_b, wv_ref[...]).astype(BF16)

    zd = _dot(h, w_ref[:, C_D:C_D + D_COLS])
    dq, dk, dv = zd[:, 0:512], zd[:, 512:1024], zd[:, 1024:1536]
    if not latent:
        kdraw_ref[...] = dk
        vdraw_ref[...] = dv
    else:
        cd, sad, sbd = tab[:, 384:512], tab[:, 512:640], tab[:, 640:768]
        dq = _rope_blocks(dq, cd, sad, sbd, DK_D // 2)
        dk = _rope_blocks(dk, cd, sad, sbd, DK_D // 2)
    qd_ref[...] = (dq * (LOG2E * DK_D ** -0.5)).astype(BF16)
    kd_ref[...] = dk.astype(BF16)
    vd_ref[...] = dv.astype(BF16)

    gate_ref[...] = _dot(h, w_ref[:, C_G:C_G + G_COLS])


def _inproj(x2, l, mod, wts, batch, n_tok, past=0, tables=None, ctx_stacks=None):
    latent = tables is not None
    t = batch * n_tok
    tm = ROW_TILE
    tpb = n_tok // tm
    row = lambda i: (i, 0)
    kv_row = lambda i: ((i // tpb) * ((past + n_tok) // tm) + past // tm + i % tpb, 0)
    stack_row = lambda i: (((i // tpb) * DEPTH + l) * tpb + i % tpb, 0)

    in_specs = [
        pl.BlockSpec((tm, D_MODEL), row),
        _mod_spec(l, 1, tpb) if latent else _mod_spec(l, 0, batch * tpb),
        _layer_spec(l, (1, D_MODEL)),
        _layer_spec(l, (D_MODEL, IN_COLS_P)),
        _layer_spec(l, (1, Q_RANK)),
        _layer_spec(l, (1, KV_RANK)),
        _layer_spec(l, (Q_RANK, H_A * LANE)),
        _layer_spec(l, (KV_RANK, H_A * LANE)),
        _layer_spec(l, (KV_RANK, H_A * V_A)),
    ]
    args = [x2, mod, wts["norm1_g"], wts["w_in"], wts["mla_q_norm_g"], wts["mla_kv_norm_g"],
            wts["wq"], wts["wk"], wts["wv"]]
    kv_rows = batch * (past + n_tok)
    outs = [(M_COLS, F32, t, row), (H_A * LANE, BF16, t, row), (H_A * LANE, BF16, kv_rows, kv_row),
            (H_A * V_A, BF16, kv_rows, kv_row), (512, BF16, t, row), (512, BF16, kv_rows, kv_row),
            (512, BF16, kv_rows, kv_row), (G_COLS, F32, t, row)]
    aliases = {}
    if latent:
        in_specs.append(pl.BlockSpec((tm, 768), lambda i: (i % tpb, 0)))
        args.append(tables)
    else:
        st_rows = batch * DEPTH * n_tok
        outs += [(KV_RANK, F32, st_rows, stack_row), (LANE, F32, st_rows, stack_row),
                 (512, F32, st_rows, stack_row), (512, F32, st_rows, stack_row)]
        if ctx_stacks is not None:
            aliases = {len(args) + n: 8 + n for n in range(4)}
            in_specs += [_ANY] * 4
            args += list(ctx_stacks)
    return pl.pallas_call(
        functools.partial(_inproj_kernel, latent, len(aliases)),
        grid=(t // tm,),
        in_specs=in_specs,
        out_specs=[pl.BlockSpec((tm, w), imap) for w, _, _, imap in outs],
        out_shape=[jax.ShapeDtypeStruct((rows, w), dt) for w, dt, rows, _ in outs],
        input_output_aliases=aliases,
        compiler_params=_params(("parallel",)),
        name="inproj_lat" if latent else "inproj_ctx",
    )(*args)


def _mlstm_direction(z, s_prev, m_prev, bi, bfg, smask, d):
    L = CHUNK
    q = (z[:, C_MQ:C_MQ + 256] * DK_M ** -0.5).astype(BF16)
    k = z[:, C_MK:C_MK + 256]
    v = z[:, C_MV:C_MV + 512].astype(BF16)
    gi = z[:, C_GI:C_GI + LANE] + bi
    xf = z[:, C_GF:C_GF + LANE] + bfg
    lf = jnp.minimum(xf, 0.0) - jnp.log(1.0 + jnp.exp(-jnp.abs(xf)))

    r_i = lax.broadcasted_iota(jnp.int32, (L, L), 0)
    c_i = lax.broadcasted_iota(jnp.int32, (L, L), 1)
    mask = (c_i <= r_i) if d == 0 else (c_i >= r_i)
    tri = jnp.where(mask, 1.0, 0.0).astype(BF16)
    lf_hi = lf.astype(BF16)
    r1 = lf - lf_hi.astype(F32)
    lf_mid = r1.astype(BF16)
    lf_lo = (r1 - lf_mid.astype(F32)).astype(BF16)
    b = _dot(tri, lf_hi) + _dot(tri, lf_mid) + _dot(tri, lf_lo)
    g = gi - b
    g_t = g.T
    last = L - 1 if d == 0 else 0

    qs = _dot(q, s_prev.astype(BF16))
    lane256 = lax.broadcasted_iota(jnp.int32, (L, 256), 1)
    lane128 = lax.broadcasted_iota(jnp.int32, (1, LANE), 1)
    kb = k.astype(BF16)

    h_out = []
    w_cols = []
    w_old = []
    m_new = m_prev
    for hd in range(H_M):
        j = d * H_M + hd
        b_col = b[:, j:j + 1]
        g_col = g[:, j:j + 1]
        g_row = g_t[j:j + 1, :]
        mp = m_prev[:, j:j + 1]
        gm = jnp.where(mask, g_row, NEG)
        u_col = jnp.maximum(jnp.max(gm, axis=-1, keepdims=True), mp)
        dmat = jnp.exp(gm - u_col)
        q_h = jnp.where((lane256 >= hd * DK_M) & (lane256 < (hd + 1) * DK_M), q, jnp.zeros_like(q))
        a = dmat * _dot_nt(q_h, kb)
        den_i = jnp.sum(a, axis=-1, keepdims=True)
        num_i = _dot(a.astype(BF16), v[:, hd * DV_M:(hd + 1) * DV_M])
        w_int = jnp.exp(mp - u_col)
        num = w_int * qs[:, hd * DV_M:(hd + 1) * DV_M] + num_i
        den = w_int * qs[:, H_M * DV_M + j:H_M * DV_M + j + 1] + den_i
        h_out.append(num / jnp.maximum(jnp.abs(den), jnp.exp(-(b_col + u_col))))
        u_l = u_col[last:last + 1, :]
        b_l = b_col[last:last + 1, :]
        m_new = jnp.where(lane128 == j, b_l + u_l, m_new)
        w_cols.append(jnp.exp(g_col - u_l))
        w_old.append(jnp.exp(mp - u_l))

    w_exp = jnp.where(lane256 < 64, w_cols[0],
                      jnp.where(lane256 < 128, w_cols[1], jnp.where(lane256 < 192, w_cols[2], w_cols[3])))
    k_w = (k * w_exp).astype(BF16)
    v_aug = jnp.concatenate([v, jnp.ones((L, LANE), BF16)], axis=1)
    upd = _dot_tn(k_w, v_aug)
    n_scale = jnp.zeros((1, LANE), F32)
    for hd in range(H_M):
        n_scale = jnp.where(lane128 == d * H_M + hd, w_old[hd], n_scale)
    scale_row = jnp.concatenate([jnp.broadcast_to(w, (1, DV_M)) for w in w_old] + [n_scale], axis=1)
    s_new = scale_row * s_prev + smask * upd
    return jnp.concatenate(h_out, axis=1), s_new, m_new


def _mlstm_kernel(has_state, *refs):
    zf_ref, zb_ref, bi_ref, bf_ref, smask_ref = refs[:5]
    refs = refs[5:]
    if has_state:
        c0_ref, n0_ref, m0_ref = refs[:3]
        refs = refs[3:]
    hf_ref, hb_ref, ct_ref, nt_ref, mt_ref, s_scr, m_scr = refs
    i = pl.program_id(1)
    hv = H_M * DV_M

    @pl.when(i == 0)
    def _():
        if has_state:
            for d in range(2):
                tiled = jnp.concatenate([c0_ref[d]] * H_M + [jnp.broadcast_to(n0_ref[d], (256, LANE))], axis=1)
                s_scr[d] = smask_ref[d] * tiled
                m_scr[d] = m0_ref[...]
        else:
            s_scr[...] = jnp.zeros_like(s_scr)
            m_scr[...] = jnp.zeros_like(m_scr)

    bi = bi_ref[...]
    bfg = bf_ref[...]
    for d, (z_ref, h_ref) in enumerate(((zf_ref, hf_ref), (zb_ref, hb_ref))):
        h, s_new, m_new = _mlstm_direction(z_ref[...], s_scr[d], m_scr[d], bi, bfg, smask_ref[d], d)
        h_ref[...] = h
        s_scr[d] = s_new
        m_scr[d] = m_new

    @pl.when(i == pl.num_programs(1) - 1)
    def _():
        for d in range(2):
            s = s_scr[d]
            ct_ref[d] = (s[:, 0:DV_M] + s[:, DV_M:2 * DV_M]) + (s[:, 2 * DV_M:3 * DV_M] + s[:, 3 * DV_M:hv])
            nt_ref[d] = jnp.sum(s[:, hv:], axis=-1, keepdims=True)
        lane = lax.broadcasted_iota(jnp.int32, (1, LANE), 1)
        mt_ref[...] = jnp.where(lane < H_M, m_scr[0], m_scr[1])


def _mlstm(m_in, l, wts, smask, batch, n_tok, state=None):
    nc = n_tok // CHUNK
    t = batch * n_tok
    hv = H_M * DV_M
    fwd = lambda b, i: (b * nc + i, 0)
    bwd = lambda b, i: (b * nc + nc - 1 - i, 0)
    in_specs = [
        pl.BlockSpec((CHUNK, M_COLS), fwd),
        pl.BlockSpec((CHUNK, M_COLS), bwd),
        pl.BlockSpec((None, 1, LANE), lambda b, i: (l, 0, 0)),
        pl.BlockSpec((None, 1, LANE), lambda b, i: (l, 0, 0)),
        pl.BlockSpec((2, 256, S_COLS), lambda b, i: (0, 0, 0)),
    ]
    args = [m_in, m_in, wts["gate_bi"], wts["gate_bf"], smask]
    if state is not None:
        in_specs += [
            pl.BlockSpec((None, None, 2, 256, DV_M), lambda b, i: (b, l, 0, 0, 0)),
            pl.BlockSpec((None, None, 2, 256, 1), lambda b, i: (b, l, 0, 0, 0)),
            pl.BlockSpec((None, None, 1, LANE), lambda b, i: (b, l, 0, 0)),
        ]
        args += list(state)
    return pl.pallas_call(
        functools.partial(_mlstm_kernel, state is not None),
        grid=(batch, nc),
        in_specs=in_specs,
        out_specs=[
            pl.BlockSpec((CHUNK, hv), fwd),
            pl.BlockSpec((CHUNK, hv), bwd),
            pl.BlockSpec((None, 2, 256, DV_M), lambda b, i: (b, 0, 0, 0)),
            pl.BlockSpec((None, 2, 256, 1), lambda b, i: (b, 0, 0, 0)),
            pl.BlockSpec((None, 1, LANE), lambda b, i: (b, 0, 0)),
        ],
        out_shape=[
            jax.ShapeDtypeStruct((t, hv), F32),
            jax.ShapeDtypeStruct((t, hv), F32),
            jax.ShapeDtypeStruct((batch, 2, 256, DV_M), F32),
            jax.ShapeDtypeStruct((batch, 2, 256, 1), F32),
            jax.ShapeDtypeStruct((batch, 1, LANE), F32),
        ],
        scratch_shapes=[pltpu.VMEM((2, 256, S_COLS), F32), pltpu.VMEM((2, 1, LANE), F32)],
        compiler_params=_params(("parallel", "arbitrary")),
        name="mlstm",
    )(*args)


def _pipelined_attention(kernel_fn, inputs, extra_specs, batch, n_q, n_kv, n_groups, widths, n_maps, name):
    q_w, k_w, v_w, out_w = widths
    tq = min(Q_TILE, n_q)
    nq = n_q // tq
    n_units = batch * n_groups * nq
    nxt = lambda s: jnp.minimum(s, n_units - 1)
    cur = lambda s: jnp.maximum(s - 1, 0)
    row_blk = lambda u: (u // (n_groups * nq)) * nq + u % nq
    grp = lambda u: (u // nq) % n_groups
    bat = lambda u: u // (n_groups * nq)
    score = pltpu.VMEM((n_maps, n_kv, tq), F32)
    cmax = pltpu.VMEM((n_maps, 1, tq), F32)
    return pl.pallas_call(
        functools.partial(kernel_fn, nq),
        grid=(n_units + 1,),
        in_specs=[
            pl.BlockSpec((tq, q_w), lambda s: (row_blk(nxt(s)), grp(nxt(s)))),
            pl.BlockSpec((n_kv, k_w), lambda s: (bat(nxt(s)), grp(nxt(s)))),
            pl.BlockSpec((n_kv, v_w), lambda s: (bat(cur(s)), grp(cur(s)))),
        ] + extra_specs(lambda s: grp(cur(s))),
        out_specs=pl.BlockSpec((tq, out_w), lambda s: (row_blk(cur(s)), grp(cur(s)))),
        out_shape=jax.ShapeDtypeStruct((batch * n_q, n_groups * out_w), BF16),
        scratch_shapes=[score, cmax, score, cmax, pltpu.VMEM((v_w, n_kv), BF16)],
        compiler_params=_params(("arbitrary",)),
        name=name,
    )(*inputs)


def _pipeline_prologue(nq, v_ref, s_b, m_b, vt_scr):
    s = pl.program_id(0)

    @pl.when(s == 0)
    def _():
        s_b[...] = jnp.zeros_like(s_b)
        m_b[...] = jnp.zeros_like(m_b)

    @pl.when(lax.rem(jnp.maximum(s - 1, 0), nq) == 0)
    def _():
        vt_scr[...] = v_ref[...].T

    return lax.rem(s, 2)


def _key_chunks(n_kv):
    blocks = n_kv // LANE
    n = min(KEY_CHUNKS, blocks)
    edges = [(i * blocks // n) * LANE for i in range(n + 1)]
    return list(zip(edges[:-1], edges[1:]))


def _score_map(k_of, q, vt_of, s_n, m_n, s_c, m_c, a):
    mc = m_c[a]
    mx = l = acc = None
    for c0, c1 in _key_chunks(s_n.shape[1]):
        st = _dot_nt(k_of(c0, c1), q)
        s_n[a, c0:c1, :] = st
        cm = jnp.max(st, axis=0, keepdims=True)
        mx = cm if mx is None else jnp.maximum(mx, cm)
        p = jnp.exp2(s_c[a, c0:c1, :] - mc)
        ps = jnp.sum(p, axis=0, keepdims=True)
        l = ps if l is None else l + ps
        pv = _dot(vt_of(c0, c1), p.astype(BF16))
        acc = pv if acc is None else acc + pv
    m_n[a] = mx
    return acc, l


def _mla_kernel(nq, q_ref, k_ref, v_ref, o_ref, s_a, m_a, s_b, m_b, vt_scr):
    parity = _pipeline_prologue(nq, v_ref, s_b, m_b, vt_scr)

    def body(s_n, m_n, s_c, m_c):
        outs = []
        for e in range(2):
            acc, l = _score_map(lambda c0, c1: k_ref[c0:c1, e * LANE:(e + 1) * LANE],
                                q_ref[:, e * LANE:(e + 1) * LANE],
                                lambda c0, c1: vt_scr[e * V_A:(e + 1) * V_A, c0:c1],
                                s_n, m_n, s_c, m_c, e)
            outs.append(acc / l)
        o_ref[...] = jnp.concatenate(outs, axis=0).T.astype(BF16)

    @pl.when(parity == 0)
    def _():
        body(s_a, m_a, s_b, m_b)

    @pl.when(parity == 1)
    def _():
        body(s_b, m_b, s_a, m_a)


def _mla_attention(q, k, v, batch, n_q, n_kv):
    return _pipelined_attention(_mla_kernel, (q, k, v), lambda cur_grp: [], batch, n_q, n_kv, H_A // 2,
                                (2 * LANE, 2 * LANE, LANE, LANE), 2, "mla_attention")


def _diff_kernel(lam_init, nq, q_ref, k_ref, v_ref, lam_ref, g_ref, o_ref, s_a, m_a, s_b, m_b, vt_scr):
    parity = _pipeline_prologue(nq, v_ref, s_b, m_b, vt_scr)

    def body(s_n, m_n, s_c, m_c):
        lane = lax.broadcasted_iota(jnp.int32, (1, LANE), 1)
        q = q_ref[...]
        zero = jnp.zeros_like(q)
        lv = lam_ref[...]
        lam = (jnp.exp(jnp.sum(lv[0:1] * lv[1:2], axis=-1, keepdims=True))
               - jnp.exp(jnp.sum(lv[2:3] * lv[3:4], axis=-1, keepdims=True)) + lam_init)
        k_of = lambda c0, c1: k_ref[c0:c1, :]
        vt_of = lambda c0, c1: vt_scr[:, c0:c1]
        acc1, l1 = _score_map(k_of, jnp.where(lane < DK_D, q, zero), vt_of, s_n, m_n, s_c, m_c, 0)
        acc2, l2 = _score_map(k_of, jnp.where(lane >= DK_D, q, zero), vt_of, s_n, m_n, s_c, m_c, 1)
        o = (acc1 / l1 - acc2 * (lam / l2)).T
        o_ref[...] = (_rms_rows(o, g_ref[...]) * (1.0 - lam_init)).astype(BF16)

    @pl.when(parity == 0)
    def _():
        body(s_a, m_a, s_b, m_b)

    @pl.when(parity == 1)
    def _():
        body(s_b, m_b, s_a, m_a)


def _diff_attention(q, k, v, l, wts, lam_init, batch, n_q, n_kv):
    extra = lambda cur_grp: [pl.BlockSpec((None, 4, DK_D), lambda s: (l, 0, 0)),
                             pl.BlockSpec((None, 1, LANE), lambda s: (l, 0, cur_grp(s)))]
    return _pipelined_attention(functools.partial(_diff_kernel, lam_init),
                                (q, k, v, wts["diff_lambda"], wts["diff_norm_g"]), extra,
                                batch, n_q, n_kv, H_D, (LANE, LANE, LANE, LANE), 2, "diff_attention")


def _merge_kernel(final, x_ref, mod_ref, hf_ref, hb_ref, mo_ref, oa_ref, od_ref, gate_ref,
                  gm_ref, wbm_ref, wba_ref, wbd_ref, wout_ref, g2_ref, wff1_ref, wff2_ref, gfin_ref, o_ref):
    d = D_MODEL
    mod = mod_ref[...]
    hm = hf_ref[...] + hb_ref[...]
    gm = gm_ref[...]
    o_m = jnp.concatenate(
        [_rms_rows(hm[:, h * DV_M:(h + 1) * DV_M], gm[:, h * DV_M:(h + 1) * DV_M]) for h in range(H_M)], axis=1)
    o_m = (o_m * jax.nn.sigmoid(mo_ref[...])).astype(BF16)
    gate = jax.nn.sigmoid(gate_ref[...])
    y = (gate[:, 0:d] * _dot(o_m, wbm_ref[...]) + gate[:, d:2 * d] * _dot(oa_ref[...], wba_ref[...])
         + gate[:, 2 * d:3 * d] * _dot(od_ref[...], wbd_ref[...]))
    x = x_ref[...] + mod[:, 2 * d:3 * d] * _dot(y.astype(BF16), wout_ref[...])
    h2 = (_rms_rows(x, g2_ref[...]) * (1.0 + mod[:, 4 * d:5 * d]) + mod[:, 3 * d:4 * d]).astype(BF16)
    f = jnp.maximum(_dot(h2, wff1_ref[...]), 0.0)
    x = x + mod[:, 5 * d:6 * d] * _dot((f * f).astype(BF16), wff2_ref[...])
    if final:
        x = _rms_rows(x, gfin_ref[...])
    o_ref[...] = x


def _merge(x2, l, mod, wts, batch, n_tok, latent, m_in, h_f, h_b, o_a, o_d, gate, gfin):
    t = x2.shape[0]
    tm = ROW_TILE
    tpb = n_tok // tm
    row = lambda i: (i, 0)
    in_specs = [
        pl.BlockSpec((tm, D_MODEL), row),
        _mod_spec(l, 1, tpb) if latent else _mod_spec(l, 0, batch * tpb),
        pl.BlockSpec((tm, 512), row),
        pl.BlockSpec((tm, 512), row),
        pl.BlockSpec((tm, 512), lambda i: (i, C_MO // 512)),
        pl.BlockSpec((tm, 512), row),
        pl.BlockSpec((tm, 512), row),
        pl.BlockSpec((tm, G_COLS), row),
        _layer_spec(l, (1, 512)),
        _layer_spec(l, (512, D_MODEL)),
        _layer_spec(l, (512, D_MODEL)),
        _layer_spec(l, (512, D_MODEL)),
        _layer_spec(l, (D_MODEL, D_MODEL)),
        _layer_spec(l, (1, D_MODEL)),
        _layer_spec(l, (D_MODEL, D_FF)),
        _layer_spec(l, (D_FF, D_MODEL)),
        pl.BlockSpec((1, D_MODEL), lambda i: (0, 0)),
    ]
    return pl.pallas_call(
        functools.partial(_merge_kernel, l == DEPTH - 1),
        grid=(t // tm,),
        in_specs=in_specs,
        out_specs=pl.BlockSpec((tm, D_MODEL), row),
        out_shape=jax.ShapeDtypeStruct((t, D_MODEL), F32),
        compiler_params=_params(("parallel",)),
        name="merge_mlp",
    )(x2, mod, h_f, h_b, m_in, o_a, o_d, gate, wts["mlstm_norm_g"], wts["w_br_mlstm"], wts["w_br_mla"],
      wts["w_br_diff"], wts["w_out"], wts["norm2_g"], wts["w_ff1"], wts["w_ff2"], gfin)


def _prep_weights(w_in, mlstm_gate_b, norm1_g, mlstm_norm_g, mla_q_norm_g, mla_w_q_up, mla_kv_norm_g,
                  mla_w_kv_up, diff_lambda, diff_norm_g, w_br_mlstm, w_br_mla, w_br_diff, w_out,
                  norm2_g, w_ff1, w_ff2):
    depth = w_in.shape[0]
    z = lambda n: jnp.zeros((depth, D_MODEL, n), BF16)
    wb = w_in.astype(BF16)
    c_mg = 2 * H_M * DK_M + 2 * H_M * DV_M
    c_acq = c_mg + 4 * H_M
    c_akr = c_acq + Q_RANK + KV_RANK
    c_dq = c_akr + ROPE_A
    w_p = jnp.concatenate([
        wb[..., :c_mg], wb[..., c_mg:c_mg + 8], z(LANE - 8), wb[..., c_mg + 8:c_mg + 16], z(LANE - 8),
        wb[..., c_acq:c_akr], z(NOPE_A), wb[..., c_akr:c_dq], z(LANE - NOPE_A - ROPE_A), wb[..., c_dq:]], axis=2)
    pad8 = lambda a: jnp.pad(a, ((0, 0), (0, LANE - 8)))[:, None, :]
    wq = mla_w_q_up.astype(BF16).reshape(depth, Q_RANK, H_A, NOPE_A + ROPE_A)
    wq = jnp.pad(wq, ((0, 0), (0, 0), (0, 0), (0, LANE - NOPE_A - ROPE_A))).reshape(depth, Q_RANK, H_A * LANE)
    wkv = mla_w_kv_up.astype(BF16).reshape(depth, KV_RANK, H_A, NOPE_A + V_A)
    wk = jnp.pad(wkv[..., :NOPE_A], ((0, 0), (0, 0), (0, 0), (0, LANE - NOPE_A))).reshape(depth, KV_RANK, H_A * LANE)
    wv = wkv[..., NOPE_A:].reshape(depth, KV_RANK, H_A * V_A)
    row = lambda a: a[:, None, :]
    return {
        "w_in": w_p,
        "gate_bi": pad8(mlstm_gate_b[:, :8]), "gate_bf": pad8(mlstm_gate_b[:, 8:]),
        "norm1_g": row(norm1_g), "norm2_g": row(norm2_g), "mlstm_norm_g": row(mlstm_norm_g),
        "mla_q_norm_g": row(mla_q_norm_g), "mla_kv_norm_g": row(mla_kv_norm_g),
        "wq": wq, "wk": wk, "wv": wv,
        "diff_lambda": diff_lambda, "diff_norm_g": row(diff_norm_g),
        "w_br_mlstm": w_br_mlstm.astype(BF16), "w_br_mla": w_br_mla.astype(BF16),
        "w_br_diff": w_br_diff.astype(BF16), "w_out": w_out.astype(BF16),
        "w_ff1": w_ff1.astype(BF16), "w_ff2": w_ff2.astype(BF16),
    }


def _rope_tables(n_tokens):
    rows = n_tokens // GRID_W
    row = jnp.repeat(jnp.arange(rows, dtype=F32), GRID_W)
    col = jnp.tile(jnp.arange(GRID_W, dtype=F32), rows)

    def cs(dim):
        quarter = dim // 4
        inv = ROPE_BASE ** (-jnp.arange(quarter, dtype=F32) / quarter)
        ang = jnp.concatenate([row[:, None] * inv, col[:, None] * inv], axis=-1)
        return jnp.cos(ang), jnp.sin(ang)

    one = lambda n: jnp.ones((n_tokens, n), F32)
    zero = lambda n: jnp.zeros((n_tokens, n), F32)
    ca, sa = cs(ROPE_A)
    cd, sd = cs(DK_D)
    return jnp.concatenate([
        one(64), ca, ca, one(32),
        zero(80), sa, zero(32),
        zero(64), -sa, zero(48),
        cd, cd, cd, cd,
        zero(32), sd, zero(32), sd,
        -sd, zero(32), -sd, zero(32)], axis=1)


def _state_mask():
    r = jnp.arange(256)[:, None] // DK_M
    c = jnp.arange(S_COLS)[None, :]
    diag = (c < H_M * DV_M) & (c // DV_M == r)
    return jnp.stack([(diag | (c == H_M * DV_M + d * H_M + r)) for d in range(2)]).astype(F32)


def _layer(x2, l, mod, wts, smask, gfin, batch, n_tok, cache=None, tables=None, ctx_stacks=None):
    latent = cache is not None
    lam_init = 0.8 - 0.6 * math.exp(-0.3 * l)
    past = cache["past"] if latent else 0
    outs = _inproj(x2, l, mod, wts, batch, n_tok, past, tables, ctx_stacks)
    m_in, q_a, k_a, v_a, q_d, k_d, v_d, gate = outs[:8]
    if latent:
        k_a, v_a, k_d, v_d = _cache_kv(l, wts, cache, batch, n_tok, (k_a, v_a, k_d, v_d))
    h_f, h_b, c_t, n_t, m_t = _mlstm(m_in, l, wts, smask, batch, n_tok, cache["state"] if latent else None)
    o_a = _mla_attention(q_a, k_a, v_a, batch, n_tok, past + n_tok)
    o_d = _diff_attention(q_d, k_d, v_d, l, wts, lam_init, batch, n_tok, past + n_tok)
    x_new = _merge(x2, l, mod, wts, batch, n_tok, latent, m_in, h_f, h_b, o_a, o_d, gate, gfin)
    return x_new, tuple(outs[8:]), (c_t, n_t, m_t)


def _cache_kv_kernel(ckv_ref, kr_ref, kd_ref, vd_ref, wk_ref, wv_ref, *refs):
    ka_out, va_out, kd_out, vd_out = refs[4:]
    ckv = ckv_ref[...].astype(BF16)
    ka_out[...] = (_dot(ckv, wk_ref[...]) + jnp.concatenate([kr_ref[...]] * H_A, axis=1)).astype(BF16)
    va_out[...] = _dot(ckv, wv_ref[...]).astype(BF16)
    kd_out[...] = kd_ref[...].astype(BF16)
    vd_out[...] = vd_ref[...].astype(BF16)


def _cache_kv(l, wts, cache, batch, n_tok, bufs):
    past = cache["past"]
    tm = ROW_TILE
    ppb = past // tm
    tiles = (past + n_tok) // tm
    cached = lambda i: (i // ppb, l, i % ppb, 0)
    out_row = lambda i: ((i // ppb) * tiles + i % ppb, 0)
    widths = (H_A * LANE, H_A * V_A, 512, 512)
    return pl.pallas_call(
        _cache_kv_kernel,
        grid=(batch * ppb,),
        in_specs=[pl.BlockSpec((None, None, tm, KV_RANK), cached),
                  pl.BlockSpec((None, None, tm, LANE), cached),
                  pl.BlockSpec((None, None, tm, 512), cached),
                  pl.BlockSpec((None, None, tm, 512), cached),
                  _layer_spec(l, (KV_RANK, H_A * LANE)), _layer_spec(l, (KV_RANK, H_A * V_A))] + [_ANY] * 4,
        out_specs=[pl.BlockSpec((tm, w), out_row) for w in widths],
        out_shape=[jax.ShapeDtypeStruct(b.shape, b.dtype) for b in bufs],
        input_output_aliases={6 + n: n for n in range(4)},
        compiler_params=_params(("parallel",)),
        name="cache_kv",
    )(cache["ckv"], cache["krope"], cache["diff_k"], cache["diff_v"], wts["wk"], wts["wv"], *bufs)


def kernel(x_prompt, x_sample, c, cache_mla_ckv, cache_mla_krope, cache_diff_k, cache_diff_v, state_mlstm_C, state_mlstm_n, state_mlstm_m, c_ctx, w_mod, b_mod, norm1_g, w_in, mlstm_gate_b, mlstm_norm_g, mla_q_norm_g, mla_w_q_up, mla_kv_norm_g, mla_w_kv_up, diff_lambda, diff_norm_g, w_br_mlstm, w_br_mla, w_br_diff, w_out, norm2_g, w_ff1, w_ff2, final_norm_g):
    bp, sp, _ = x_prompt.shape
    bs, ss, _ = x_sample.shape
    past = cache_mla_ckv.shape[2]
    assert bs + 1 <= 8 and ss % GRID_W == 0
    assert sp % ROW_TILE == 0 and ss % ROW_TILE == 0 and past % ROW_TILE == 0

    cond8 = jnp.concatenate([c_ctx[None, :], c, jnp.zeros((8 - 1 - bs, D_MODEL), F32)], axis=0)
    mod = _modulation(cond8, w_mod, b_mod).reshape(DEPTH, 8, 1, 6 * D_MODEL)
    wts = _prep_weights(w_in, mlstm_gate_b, norm1_g, mlstm_norm_g, mla_q_norm_g, mla_w_q_up, mla_kv_norm_g,
                        mla_w_kv_up, diff_lambda, diff_norm_g, w_br_mlstm, w_br_mla, w_br_diff, w_out,
                        norm2_g, w_ff1, w_ff2)
    tables = _rope_tables(ss)
    smask = _state_mask()
    gfin = final_norm_g[None, :]
    rows = H_M * DK_M
    cache = {
        "past": past,
        "ckv": cache_mla_ckv,
        "krope": jnp.pad(cache_mla_krope, ((0, 0), (0, 0), (0, 0), (NOPE_A, LANE - NOPE_A - ROPE_A))),
        "diff_k": cache_diff_k.reshape(bs, DEPTH, past, H_D * 2 * DK_D),
        "diff_v": cache_diff_v.reshape(bs, DEPTH, past, H_D * DV_D),
        "state": (state_mlstm_C.reshape(bs, DEPTH, 2, rows, DV_M),
                  state_mlstm_n.reshape(bs, DEPTH, 2, rows, 1),
                  jnp.pad(state_mlstm_m.reshape(bs, DEPTH, 1, 2 * H_M), ((0, 0), (0, 0), (0, 0), (0, LANE - 2 * H_M)))),
    }

    y_p = x_prompt.reshape(bp * sp, D_MODEL)
    y_s = x_sample.reshape(bs * ss, D_MODEL)
    stacks = None
    states = []
    for l in range(DEPTH):
        y_p, stacks, state = _layer(y_p, l, mod, wts, smask, gfin, bp, sp, ctx_stacks=stacks)
        states.append(state)
        y_s, _, _ = _layer(y_s, l, mod, wts, smask, gfin, bs, ss, cache=cache, tables=tables)

    ckv, akr, kd, vd = stacks
    c_t = jnp.stack([s[0] for s in states], axis=1).reshape(bp, DEPTH, 2, H_M, DK_M, DV_M)
    n_t = jnp.stack([s[1] for s in states], axis=1).reshape(bp, DEPTH, 2, H_M, DK_M)
    m_t = jnp.stack([s[2][:, 0, :2 * H_M] for s in states], axis=1).reshape(bp, DEPTH, 2, H_M)
    return (y_p.reshape(bp, sp, D_MODEL), y_s.reshape(bs, ss, D_MODEL),
            ckv.reshape(bp, DEPTH, sp, KV_RANK),
            akr[:, NOPE_A:NOPE_A + ROPE_A].reshape(bp, DEPTH, sp, ROPE_A),
            kd.reshape(bp, DEPTH, sp, H_D, 2 * DK_D), vd.reshape(bp, DEPTH, sp, H_D, DV_D),
            c_t, n_t, m_t)
```

```python
import functools
import math

import jax
import jax.numpy as jnp
from jax import lax
from jax.experimental import pallas as pl
from jax.experimental.pallas import tpu as pltpu

F32 = jnp.float32
BF16 = jnp.bfloat16

D_MODEL = 1024
DEPTH = 2
GRID_W = 64
ROPE_BASE = 10000.0
EPS = 1e-6
H_M, DK_M, DV_M = 4, 64, 128
H_A, Q_RANK, KV_RANK, NOPE_A, ROPE_A, V_A = 8, 384, 256, 64, 32, 64
H_D, DK_D, DV_D = 4, 64, 128
D_FF = 4 * D_MODEL

LANE = 128
VMEM_LIMIT = 56 * 1024 * 1024
ROW_TILE = 256
Q_TILE = 256
CHUNK = 128
KEY_CHUNKS = 4
SCORE_VMEM_BUDGET = 24 * 1024 * 1024
MLSTM_STREAMS = 2
NEG = -1e30
LOG2E = 1.4426950408889634

C_MQ, C_MK, C_MV, C_MO, C_GI, C_GF = 0, 256, 512, 1024, 1536, 1664
M_COLS = 1792
C_A = 1792
A_COLS = 768
C_D = 2560
D_COLS = 1536
C_G = 4096
G_COLS = 3 * D_MODEL
IN_COLS_P = 7168
S_COLS = H_M * DV_M + LANE


def _dot(a, b):
    return jnp.dot(a, b, preferred_element_type=F32)


def _dot_nt(a, b):
    return lax.dot_general(a, b, (((1,), (1,)), ((), ())), preferred_element_type=F32)


def _dot_tn(a, b):
    return lax.dot_general(a, b, (((0,), (0,)), ((), ())), preferred_element_type=F32)


def _rms_rows(x, g):
    return x * lax.rsqrt(jnp.mean(x * x, axis=-1, keepdims=True) + EPS) * g


def _rope_blocks(x, cos, sa, sb, shift):
    outs = []
    for b in range(x.shape[1] // LANE):
        xb = x[:, b * LANE:(b + 1) * LANE]
        outs.append(xb * cos + pltpu.roll(xb, shift, 1) * sa + pltpu.roll(xb, LANE - shift, 1) * sb)
    return outs[0] if len(outs) == 1 else jnp.concatenate(outs, axis=1)


def _params(sem):
    return pltpu.CompilerParams(dimension_semantics=sem, vmem_limit_bytes=VMEM_LIMIT)


def _layer_spec(l, shape):
    nd = len(shape)
    return pl.BlockSpec((None,) + shape, lambda *_: (l,) + (0,) * nd, pipeline_mode=pl.Buffered(1))


def _mod_spec(l, first_row, tiles_per_cond):
    return pl.BlockSpec((None, None, 1, 6 * D_MODEL), lambda i: (l, first_row + i // tiles_per_cond, 0, 0))


_ANY = pl.BlockSpec(memory_space=pl.ANY)


def _mod_kernel(c_ref, w_ref, b_ref, o_ref):
    c = c_ref[...]
    s = (c * jax.nn.sigmoid(c)).astype(BF16)
    o_ref[...] = _dot(s, w_ref[...].astype(BF16)) + b_ref[...]


def _modulation(cond8, w_mod, b_mod):
    tn = 1536
    n6 = 6 * D_MODEL
    return pl.pallas_call(
        _mod_kernel,
        grid=(DEPTH, n6 // tn),
        in_specs=[
            pl.BlockSpec((8, D_MODEL), lambda l, j: (0, 0)),
            pl.BlockSpec((None, D_MODEL, tn), lambda l, j: (l, 0, j)),
            pl.BlockSpec((None, 1, tn), lambda l, j: (l, 0, j)),
        ],
        out_specs=pl.BlockSpec((None, 8, tn), lambda l, j: (l, 0, j)),
        out_shape=jax.ShapeDtypeStruct((DEPTH, 8, n6), F32),
        compiler_params=_params(("parallel", "parallel")),
        name="modulation",
    )(cond8, w_mod, b_mod.reshape(DEPTH, 1, n6))


def _inproj_kernel(latent, n_aliased, *refs):
    (x_ref, mod_ref, g1_ref, w_ref, gq_ref, gkv_ref, wq_ref, wk_ref, wv_ref) = refs[:9]
    refs = refs[9:]
    if latent:
        tab_ref = refs[0]
        (m_ref, q_ref, k_ref, va_ref, qd_ref, kd_ref, vd_ref, gate_ref) = refs[1:]
    else:
        (m_ref, q_ref, k_ref, va_ref, qd_ref, kd_ref, vd_ref, gate_ref,
         ckv_ref, akr_ref, kdraw_ref, vdraw_ref) = refs[n_aliased:]
    d = D_MODEL
    x = x_ref[...]
    mod = mod_ref[...]
    h = (_rms_rows(x, g1_ref[...]) * (1.0 + mod[:, d:2 * d]) + mod[:, 0:d]).astype(BF16)

    m_ref[...] = _dot(h, w_ref[:, 0:M_COLS])

    za = _dot(h, w_ref[:, C_A:C_A + A_COLS])
    acq = za[:, 0:Q_RANK]
    ackv = za[:, Q_RANK:Q_RANK + KV_RANK]
    akr = za[:, Q_RANK + KV_RANK:A_COLS]
    q = _dot(_rms_rows(acq, gq_ref[...]).astype(BF16), wq_ref[...])
    ckv = _rms_rows(ackv, gkv_ref[...])
    ckv_b = ckv.astype(BF16)
    kn = _dot(ckv_b, wk_ref[...])
    if not latent:
        ckv_ref[...] = ckv
        akr_ref[...] = akr
    else:
        tab = tab_ref[...]
        cq, saq, sbq = tab[:, 0:128], tab[:, 128:256], tab[:, 256:384]
        q = _rope_blocks(q, cq, saq, sbq, ROPE_A // 2)
        akr = _rope_blocks(akr, cq, saq, sbq, ROPE_A // 2)
    q_ref[...] = (q * (LOG2E * (NOPE_A + ROPE_A) ** -0.5)).astype(BF16)
    k_ref[...] = (kn + jnp.concatenate([akr] * H_A, axis=1)).astype(BF16)
    va_ref[...] = _dot(ckv_b, wv_ref[...]).astype(BF16)

    zd = _dot(h, w_ref[:, C_D:C_D + D_COLS])
    dq, dk, dv = zd[:, 0:512], zd[:, 512:1024], zd[:, 1024:1536]
    if not latent:
        kdraw_ref[...] = dk
        vdraw_ref[...] = dv
    else:
        cd, sad, sbd = tab[:, 384:512], tab[:, 512:640], tab[:, 640:768]
        dq = _rope_blocks(dq, cd, sad, sbd, DK_D // 2)
        dk = _rope_blocks(dk, cd, sad, sbd, DK_D // 2)
    qd_ref[...] = (dq * (LOG2E * DK_D ** -0.5)).astype(BF16)
    kd_ref[...] = dk.astype(BF16)
    vd_ref[...] = dv.astype(BF16)

    gate_ref[...] = _dot(h, w_ref[:, C_G:C_G + G_COLS])


def _inproj(x2, l, mod, wts, batch, n_tok, past=0, tables=None, ctx_stacks=None):
    latent = tables is not None
    t = batch * n_tok
    tm = ROW_TILE
    tpb = n_tok // tm
    row = lambda i: (i, 0)
    kv_row = lambda i: ((i // tpb) * ((past + n_tok) // tm) + past // tm + i % tpb, 0)
    stack_row = lambda i: (((i // tpb) * DEPTH + l) * tpb + i % tpb, 0)

    in_specs = [
        pl.BlockSpec((tm, D_MODEL), row),
        _mod_spec(l, 1, tpb) if latent else _mod_spec(l, 0, batch * tpb),
        _layer_spec(l, (1, D_MODEL)),
        _layer_spec(l, (D_MODEL, IN_COLS_P)),
        _layer_spec(l, (1, Q_RANK)),
        _layer_spec(l, (1, KV_RANK)),
        _layer_spec(l, (Q_RANK, H_A * LANE)),
        _layer_spec(l, (KV_RANK, H_A * LANE)),
        _layer_spec(l, (KV_RANK, H_A * V_A)),
    ]
    args = [x2, mod, wts["norm1_g"], wts["w_in"], wts["mla_q_norm_g"], wts["mla_kv_norm_g"],
            wts["wq"], wts["wk"], wts["wv"]]
    kv_rows = batch * (past + n_tok)
    outs = [(M_COLS, F32, t, row), (H_A * LANE, BF16, t, row), (H_A * LANE, BF16, kv_rows, kv_row),
            (H_A * V_A, BF16, kv_rows, kv_row), (512, BF16, t, row), (512, BF16, kv_rows, kv_row),
            (512, BF16, kv_rows, kv_row), (G_COLS, F32, t, row)]
    aliases = {}
    if latent:
        in_specs.append(pl.BlockSpec((tm, 768), lambda i: (i % tpb, 0)))
        args.append(tables)
    else:
        st_rows = batch * DEPTH * n_tok
        outs += [(KV_RANK, F32, st_rows, stack_row), (LANE, F32, st_rows, stack_row),
                 (512, F32, st_rows, stack_row), (512, F32, st_rows, stack_row)]
        if ctx_stacks is not None:
            aliases = {len(args) + n: 8 + n for n in range(4)}
            in_specs += [_ANY] * 4
            args += list(ctx_stacks)
    return pl.pallas_call(
        functools.partial(_inproj_kernel, latent, len(aliases)),
        grid=(t // tm,),
        in_specs=in_specs,
        out_specs=[pl.BlockSpec((tm, w), imap) for w, _, _, imap in outs],
        out_shape=[jax.ShapeDtypeStruct((rows, w), dt) for w, dt, rows, _ in outs],
        input_output_aliases=aliases,
        compiler_params=_params(("parallel",)),
        name="inproj_lat" if latent else "inproj_ctx",
    )(*args)


def _mlstm_chunks(items, bi, bfg, smask_ref):
    L = CHUNK
    r_i = lax.broadcasted_iota(jnp.int32, (L, L), 0)
    c_i = lax.broadcasted_iota(jnp.int32, (L, L), 1)
    lane256 = lax.broadcasted_iota(jnp.int32, (L, 256), 1)
    lane128 = lax.broadcasted_iota(jnp.int32, (1, LANE), 1)
    masks = {0: c_i <= r_i, 1: c_i >= r_i}
    tris = {d: jnp.where(m, 1.0, 0.0).astype(BF16) for d, m in masks.items()}
    heads = [(n, hd) for n in range(len(items)) for hd in range(H_M)]

    pre = []
    for z, s_prev, m_prev, d in items:
        q = (z[:, C_MQ:C_MQ + 256] * DK_M ** -0.5).astype(BF16)
        k = z[:, C_MK:C_MK + 256]
        v = z[:, C_MV:C_MV + 512].astype(BF16)
        gi = z[:, C_GI:C_GI + LANE] + bi
        xf = z[:, C_GF:C_GF + LANE] + bfg
        lf = jnp.minimum(xf, 0.0) - jnp.log(1.0 + jnp.exp(-jnp.abs(xf)))
        lf_hi = lf.astype(BF16)
        r1 = lf - lf_hi.astype(F32)
        lf_mid = r1.astype(BF16)
        lf_lo = (r1 - lf_mid.astype(F32)).astype(BF16)
        b = _dot(tris[d], lf_hi) + _dot(tris[d], lf_mid) + _dot(tris[d], lf_lo)
        g = gi - b
        pre.append(dict(q=q, k=k, kb=k.astype(BF16), v=v, b=b, g=g, g_t=g.T,
                        qs=_dot(q, s_prev.astype(BF16)), last=L - 1 if d == 0 else 0))

    gm, u_col = {}, {}
    for n, hd in heads:
        d = items[n][3]
        j = d * H_M + hd
        gm[n, hd] = jnp.where(masks[d], pre[n]["g_t"][j:j + 1, :], NEG)
        u_col[n, hd] = jnp.maximum(jnp.max(gm[n, hd], axis=-1, keepdims=True), items[n][2][:, j:j + 1])
    a = {}
    for n, hd in heads:
        q = pre[n]["q"]
        q_h = jnp.where((lane256 >= hd * DK_M) & (lane256 < (hd + 1) * DK_M), q, jnp.zeros_like(q))
        a[n, hd] = jnp.exp(gm[n, hd] - u_col[n, hd]) * _dot_nt(q_h, pre[n]["kb"])
    den_i, num_i = {}, {}
    for n, hd in heads:
        den_i[n, hd] = jnp.sum(a[n, hd], axis=-1, keepdims=True)
        num_i[n, hd] = _dot(a[n, hd].astype(BF16), pre[n]["v"][:, hd * DV_M:(hd + 1) * DV_M])

    results = []
    for n, (z, s_prev, m_prev, d) in enumerate(items):
        p = pre[n]
        h_out, w_cols, w_old = [], [], []
        m_new = m_prev
        for hd in range(H_M):
            j = d * H_M + hd
            u = u_col[n, hd]
            mp = m_prev[:, j:j + 1]
            b_col = p["b"][:, j:j + 1]
            w_int = jnp.exp(mp - u)
            num = w_int * p["qs"][:, hd * DV_M:(hd + 1) * DV_M] + num_i[n, hd]
            den = w_int * p["qs"][:, H_M * DV_M + j:H_M * DV_M + j + 1] + den_i[n, hd]
            h_out.append(num / jnp.maximum(jnp.abs(den), jnp.exp(-(b_col + u))))
            u_l = u[p["last"]:p["last"] + 1, :]
            b_l = b_col[p["last"]:p["last"] + 1, :]
            m_new = jnp.where(lane128 == j, b_l + u_l, m_new)
            w_cols.append(jnp.exp(p["g"][:, j:j + 1] - u_l))
            w_old.append(jnp.exp(mp - u_l))
        w_exp = jnp.where(lane256 < 64, w_cols[0],
                          jnp.where(lane256 < 128, w_cols[1], jnp.where(lane256 < 192, w_cols[2], w_cols[3])))
        k_w = (p["k"] * w_exp).astype(BF16)
        v_aug = jnp.concatenate([p["v"], jnp.ones((L, LANE), BF16)], axis=1)
        upd = _dot_tn(k_w, v_aug)
        n_scale = jnp.zeros((1, LANE), F32)
        for hd in range(H_M):
            n_scale = jnp.where(lane128 == d * H_M + hd, w_old[hd], n_scale)
        scale_row = jnp.concatenate([jnp.broadcast_to(w, (1, DV_M)) for w in w_old] + [n_scale], axis=1)
        results.append((jnp.concatenate(h_out, axis=1), scale_row * s_prev + smask_ref[d] * upd, m_new))
    return results


def _mlstm_kernel(has_state, *refs):
    zf_ref, zb_ref, bi_ref, bf_ref, smask_ref = refs[:5]
    refs = refs[5:]
    if has_state:
        c0_ref, n0_ref, m0_ref = refs[:3]
        refs = refs[3:]
    hf_ref, hb_ref, ct_ref, nt_ref, mt_ref, s_scr, m_scr = refs
    i = pl.program_id(1)
    hv = H_M * DV_M

    streams = s_scr.shape[0]

    @pl.when(i == 0)
    def _():
        if has_state:
            for g in range(streams):
                for d in range(2):
                    tiled = jnp.concatenate(
                        [c0_ref[g, d]] * H_M + [jnp.broadcast_to(n0_ref[g, d], (256, LANE))], axis=1)
                    s_scr[g, d] = smask_ref[d] * tiled
                    m_scr[g, d] = m0_ref[g]
        else:
            s_scr[...] = jnp.zeros_like(s_scr)
            m_scr[...] = jnp.zeros_like(m_scr)

    bi = bi_ref[...]
    bfg = bf_ref[...]
    slots = [(g, d) for g in range(streams) for d in range(2)]
    items = [((zf_ref, zb_ref)[d][g], s_scr[g, d], m_scr[g, d], d) for g, d in slots]
    for (g, d), (h, s_new, m_new) in zip(slots, _mlstm_chunks(items, bi, bfg, smask_ref)):
        (hf_ref, hb_ref)[d][g] = h
        s_scr[g, d] = s_new
        m_scr[g, d] = m_new

    @pl.when(i == pl.num_programs(1) - 1)
    def _():
        lane = lax.broadcasted_iota(jnp.int32, (1, LANE), 1)
        for g in range(streams):
            for d in range(2):
                s = s_scr[g, d]
                ct_ref[g, d] = ((s[:, 0:DV_M] + s[:, DV_M:2 * DV_M])
                                + (s[:, 2 * DV_M:3 * DV_M] + s[:, 3 * DV_M:hv]))
                nt_ref[g, d] = jnp.sum(s[:, hv:], axis=-1, keepdims=True)
            mt_ref[g] = jnp.where(lane < H_M, m_scr[g, 0], m_scr[g, 1])


def _mlstm(m_in, l, wts, smask, batch, n_tok, state=None):
    nc = n_tok // CHUNK
    hv = H_M * DV_M
    g = MLSTM_STREAMS if batch % MLSTM_STREAMS == 0 else 1
    z3 = m_in.reshape(batch, n_tok, M_COLS)
    fwd = lambda b, i: (b, i, 0)
    bwd = lambda b, i: (b, nc - 1 - i, 0)
    in_specs = [
        pl.BlockSpec((g, CHUNK, M_COLS), fwd),
        pl.BlockSpec((g, CHUNK, M_COLS), bwd),
        pl.BlockSpec((None, 1, LANE), lambda b, i: (l, 0, 0)),
        pl.BlockSpec((None, 1, LANE), lambda b, i: (l, 0, 0)),
        pl.BlockSpec((2, 256, S_COLS), lambda b, i: (0, 0, 0)),
    ]
    args = [z3, z3, wts["gate_bi"], wts["gate_bf"], smask]
    if state is not None:
        in_specs += [
            pl.BlockSpec((g, None, 2, 256, DV_M), lambda b, i: (b, l, 0, 0, 0)),
            pl.BlockSpec((g, None, 2, 256, 1), lambda b, i: (b, l, 0, 0, 0)),
            pl.BlockSpec((g, None, 1, LANE), lambda b, i: (b, l, 0, 0)),
        ]
        args += list(state)
    h_f, h_b, c_t, n_t, m_t = pl.pallas_call(
        functools.partial(_mlstm_kernel, state is not None),
        grid=(batch // g, nc),
        in_specs=in_specs,
        out_specs=[
            pl.BlockSpec((g, CHUNK, hv), fwd),
            pl.BlockSpec((g, CHUNK, hv), bwd),
            pl.BlockSpec((g, 2, 256, DV_M), lambda b, i: (b, 0, 0, 0)),
            pl.BlockSpec((g, 2, 256, 1), lambda b, i: (b, 0, 0, 0)),
            pl.BlockSpec((g, 1, LANE), lambda b, i: (b, 0, 0)),
        ],
        out_shape=[
            jax.ShapeDtypeStruct((batch, n_tok, hv), F32),
            jax.ShapeDtypeStruct((batch, n_tok, hv), F32),
            jax.ShapeDtypeStruct((batch, 2, 256, DV_M), F32),
            jax.ShapeDtypeStruct((batch, 2, 256, 1), F32),
            jax.ShapeDtypeStruct((batch, 1, LANE), F32),
        ],
        scratch_shapes=[pltpu.VMEM((g, 2, 256, S_COLS), F32), pltpu.VMEM((g, 2, 1, LANE), F32)],
        compiler_params=_params(("parallel", "arbitrary")),
        name="mlstm",
    )(*args)
    return h_f.reshape(batch * n_tok, hv), h_b.reshape(batch * n_tok, hv), c_t, n_t, m_t


def _groups_per_step(n_groups, n_maps, n_kv, tq):
    return n_groups if 2 * n_groups * n_maps * n_kv * tq * 4 <= SCORE_VMEM_BUDGET else 1


def _pipelined_attention(kernel_fn, inputs, extra_specs, batch, n_q, n_kv, n_groups, widths, n_maps, name):
    tq = min(Q_TILE, n_q)
    nq = n_q // tq
    gps = _groups_per_step(n_groups, n_maps, n_kv, tq)
    n_groups //= gps
    n_maps *= gps
    q_w, k_w, v_w, out_w = (w * gps for w in widths)
    kernel_fn = functools.partial(kernel_fn, gps)
    n_units = batch * n_groups * nq
    nxt = lambda s: jnp.minimum(s, n_units - 1)
    cur = lambda s: jnp.maximum(s - 1, 0)
    row_blk = lambda u: (u // (n_groups * nq)) * nq + u % nq
    grp = lambda u: (u // nq) % n_groups
    bat = lambda u: u // (n_groups * nq)
    score = pltpu.VMEM((n_maps, n_kv, tq), F32)
    cmax = pltpu.VMEM((n_maps, 1, tq), F32)
    return pl.pallas_call(
        functools.partial(kernel_fn, nq),
        grid=(n_units + 1,),
        in_specs=[
            pl.BlockSpec((tq, q_w), lambda s: (row_blk(nxt(s)), grp(nxt(s)))),
            pl.BlockSpec((n_kv, k_w), lambda s: (bat(nxt(s)), grp(nxt(s)))),
            pl.BlockSpec((n_kv, v_w), lambda s: (bat(cur(s)), grp(cur(s)))),
        ] + extra_specs(lambda s: grp(cur(s)), gps),
        out_specs=pl.BlockSpec((tq, out_w), lambda s: (row_blk(cur(s)), grp(cur(s)))),
        out_shape=jax.ShapeDtypeStruct((batch * n_q, n_groups * out_w), BF16),
        scratch_shapes=[score, cmax, score, cmax, pltpu.VMEM((v_w, n_kv), BF16)],
        compiler_params=_params(("arbitrary",)),
        name=name,
    )(*inputs)


def _pipeline_prologue(nq, v_ref, s_b, m_b, vt_scr):
    s = pl.program_id(0)

    @pl.when(s == 0)
    def _():
        s_b[...] = jnp.zeros_like(s_b)
        m_b[...] = jnp.zeros_like(m_b)

    @pl.when(lax.rem(jnp.maximum(s - 1, 0), nq) == 0)
    def _():
        vt_scr[...] = v_ref[...].T

    return lax.rem(s, 2)


def _key_chunks(n_kv):
    blocks = n_kv // LANE
    n = min(KEY_CHUNKS, blocks)
    edges = [(i * blocks // n) * LANE for i in range(n + 1)]
    return list(zip(edges[:-1], edges[1:]))


def _score_map(k_of, q, vt_of, s_n, m_n, s_c, m_c, a):
    mc = m_c[a]
    mx = l = acc = None
    for c0, c1 in _key_chunks(s_n.shape[1]):
        st = _dot_nt(k_of(c0, c1), q)
        s_n[a, c0:c1, :] = st
        cm = jnp.max(st, axis=0, keepdims=True)
        mx = cm if mx is None else jnp.maximum(mx, cm)
        p = jnp.exp2(s_c[a, c0:c1, :] - mc)
        ps = jnp.sum(p, axis=0, keepdims=True)
        l = ps if l is None else l + ps
        pv = _dot(vt_of(c0, c1), p.astype(BF16))
        acc = pv if acc is None else acc + pv
    m_n[a] = mx
    return acc, l


def _mla_kernel(gps, nq, q_ref, k_ref, v_ref, o_ref, s_a, m_a, s_b, m_b, vt_scr):
    parity = _pipeline_prologue(nq, v_ref, s_b, m_b, vt_scr)

    def body(s_n, m_n, s_c, m_c):
        outs = []
        for e in range(2 * gps):
            acc, l = _score_map(lambda c0, c1: k_ref[c0:c1, e * LANE:(e + 1) * LANE],
                                q_ref[:, e * LANE:(e + 1) * LANE],
                                lambda c0, c1: vt_scr[e * V_A:(e + 1) * V_A, c0:c1],
                                s_n, m_n, s_c, m_c, e)
            outs.append(acc / l)
        o_ref[...] = jnp.concatenate(outs, axis=0).T.astype(BF16)

    @pl.when(parity == 0)
    def _():
        body(s_a, m_a, s_b, m_b)

    @pl.when(parity == 1)
    def _():
        body(s_b, m_b, s_a, m_a)


def _mla_attention(q, k, v, batch, n_q, n_kv):
    return _pipelined_attention(_mla_kernel, (q, k, v), lambda cur_grp, gps: [], batch, n_q, n_kv, H_A // 2,
                                (2 * LANE, 2 * LANE, LANE, LANE), 2, "mla_attention")


def _diff_kernel(lam_init, gps, nq, q_ref, k_ref, v_ref, lam_ref, g_ref, o_ref, s_a, m_a, s_b, m_b, vt_scr):
    parity = _pipeline_prologue(nq, v_ref, s_b, m_b, vt_scr)

    def body(s_n, m_n, s_c, m_c):
        lane = lax.broadcasted_iota(jnp.int32, (1, LANE), 1)
        lv = lam_ref[...]
        lam = (jnp.exp(jnp.sum(lv[0:1] * lv[1:2], axis=-1, keepdims=True))
               - jnp.exp(jnp.sum(lv[2:3] * lv[3:4], axis=-1, keepdims=True)) + lam_init)
        outs = []
        for h in range(gps):
            q = q_ref[:, h * LANE:(h + 1) * LANE]
            zero = jnp.zeros_like(q)
            k_of = lambda c0, c1: k_ref[c0:c1, h * LANE:(h + 1) * LANE]
            vt_of = lambda c0, c1: vt_scr[h * LANE:(h + 1) * LANE, c0:c1]
            acc1, l1 = _score_map(k_of, jnp.where(lane < DK_D, q, zero), vt_of, s_n, m_n, s_c, m_c, 2 * h)
            acc2, l2 = _score_map(k_of, jnp.where(lane >= DK_D, q, zero), vt_of, s_n, m_n, s_c, m_c, 2 * h + 1)
            o = (acc1 / l1 - acc2 * (lam / l2)).T
            outs.append(_rms_rows(o, g_ref[:, h * LANE:(h + 1) * LANE]) * (1.0 - lam_init))
        o_ref[...] = (outs[0] if gps == 1 else jnp.concatenate(outs, axis=1)).astype(BF16)

    @pl.when(parity == 0)
    def _():
        body(s_a, m_a, s_b, m_b)

    @pl.when(parity == 1)
    def _():
        body(s_b, m_b, s_a, m_a)


def _diff_attention(q, k, v, l, wts, lam_init, batch, n_q, n_kv):
    extra = lambda cur_grp, gps: [pl.BlockSpec((None, 4, DK_D), lambda s: (l, 0, 0)),
                                  pl.BlockSpec((None, 1, gps * LANE), lambda s: (l, 0, cur_grp(s)))]
    return _pipelined_attention(functools.partial(_diff_kernel, lam_init),
                                (q, k, v, wts["diff_lambda"], wts["diff_norm_g"]), extra,
                                batch, n_q, n_kv, H_D, (LANE, LANE, LANE, LANE), 2, "diff_attention")


def _merge_kernel(final, x_ref, mod_ref, hf_ref, hb_ref, mo_ref, oa_ref, od_ref, gate_ref,
                  gm_ref, wbm_ref, wba_ref, wbd_ref, wout_ref, g2_ref, wff1_ref, wff2_ref, gfin_ref, o_ref):
    d = D_MODEL
    mod = mod_ref[...]
    hm = hf_ref[...] + hb_ref[...]
    gm = gm_ref[...]
    o_m = jnp.concatenate(
        [_rms_rows(hm[:, h * DV_M:(h + 1) * DV_M], gm[:, h * DV_M:(h + 1) * DV_M]) for h in range(H_M)], axis=1)
    o_m = (o_m * jax.nn.sigmoid(mo_ref[...])).astype(BF16)
    gate = jax.nn.sigmoid(gate_ref[...])
    y = (gate[:, 0:d] * _dot(o_m, wbm_ref[...]) + gate[:, d:2 * d] * _dot(oa_ref[...], wba_ref[...])
         + gate[:, 2 * d:3 * d] * _dot(od_ref[...], wbd_ref[...]))
    x = x_ref[...] + mod[:, 2 * d:3 * d] * _dot(y.astype(BF16), wout_ref[...])
    h2 = (_rms_rows(x, g2_ref[...]) * (1.0 + mod[:, 4 * d:5 * d]) + mod[:, 3 * d:4 * d]).astype(BF16)
    f = jnp.maximum(_dot(h2, wff1_ref[...]), 0.0)
    x = x + mod[:, 5 * d:6 * d] * _dot((f * f).astype(BF16), wff2_ref[...])
    if final:
        x = _rms_rows(x, gfin_ref[...])
    o_ref[...] = x


def _merge(x2, l, mod, wts, batch, n_tok, latent, m_in, h_f, h_b, o_a, o_d, gate, gfin):
    t = x2.shape[0]
    tm = ROW_TILE
    tpb = n_tok // tm
    row = lambda i: (i, 0)
    in_specs = [
        pl.BlockSpec((tm, D_MODEL), row),
        _mod_spec(l, 1, tpb) if latent else _mod_spec(l, 0, batch * tpb),
        pl.BlockSpec((tm, 512), row),
        pl.BlockSpec((tm, 512), row),
        pl.BlockSpec((tm, 512), lambda i: (i, C_MO // 512)),
        pl.BlockSpec((tm, 512), row),
        pl.BlockSpec((tm, 512), row),
        pl.BlockSpec((tm, G_COLS), row),
        _layer_spec(l, (1, 512)),
        _layer_spec(l, (512, D_MODEL)),
        _layer_spec(l, (512, D_MODEL)),
        _layer_spec(l, (512, D_MODEL)),
        _layer_spec(l, (D_MODEL, D_MODEL)),
        _layer_spec(l, (1, D_MODEL)),
        _layer_spec(l, (D_MODEL, D_FF)),
        _layer_spec(l, (D_FF, D_MODEL)),
        pl.BlockSpec((1, D_MODEL), lambda i: (0, 0)),
    ]
    return pl.pallas_call(
        functools.partial(_merge_kernel, l == DEPTH - 1),
        grid=(t // tm,),
        in_specs=in_specs,
        out_specs=pl.BlockSpec((tm, D_MODEL), row),
        out_shape=jax.ShapeDtypeStruct((t, D_MODEL), F32),
        compiler_params=_params(("parallel",)),
        name="merge_mlp",
    )(x2, mod, h_f, h_b, m_in, o_a, o_d, gate, wts["mlstm_norm_g"], wts["w_br_mlstm"], wts["w_br_mla"],
      wts["w_br_diff"], wts["w_out"], wts["norm2_g"], wts["w_ff1"], wts["w_ff2"], gfin)


def _prep_weights(w_in, mlstm_gate_b, norm1_g, mlstm_norm_g, mla_q_norm_g, mla_w_q_up, mla_kv_norm_g,
                  mla_w_kv_up, diff_lambda, diff_norm_g, w_br_mlstm, w_br_mla, w_br_diff, w_out,
                  norm2_g, w_ff1, w_ff2):
    depth = w_in.shape[0]
    z = lambda n: jnp.zeros((depth, D_MODEL, n), BF16)
    wb = w_in.astype(BF16)
    c_mg = 2 * H_M * DK_M + 2 * H_M * DV_M
    c_acq = c_mg + 4 * H_M
    c_akr = c_acq + Q_RANK + KV_RANK
    c_dq = c_akr + ROPE_A
    w_p = jnp.concatenate([
        wb[..., :c_mg], wb[..., c_mg:c_mg + 8], z(LANE - 8), wb[..., c_mg + 8:c_mg + 16], z(LANE - 8),
        wb[..., c_acq:c_akr], z(NOPE_A), wb[..., c_akr:c_dq], z(LANE - NOPE_A - ROPE_A), wb[..., c_dq:]], axis=2)
    pad8 = lambda a: jnp.pad(a, ((0, 0), (0, LANE - 8)))[:, None, :]
    wq = mla_w_q_up.astype(BF16).reshape(depth, Q_RANK, H_A, NOPE_A + ROPE_A)
    wq = jnp.pad(wq, ((0, 0), (0, 0), (0, 0), (0, LANE - NOPE_A - ROPE_A))).reshape(depth, Q_RANK, H_A * LANE)
    wkv = mla_w_kv_up.astype(BF16).reshape(depth, KV_RANK, H_A, NOPE_A + V_A)
    wk = jnp.pad(wkv[..., :NOPE_A], ((0, 0), (0, 0), (0, 0), (0, LANE - NOPE_A))).reshape(depth, KV_RANK, H_A * LANE)
    wv = wkv[..., NOPE_A:].reshape(depth, KV_RANK, H_A * V_A)
    row = lambda a: a[:, None, :]
    return {
        "w_in": w_p,
        "gate_bi": pad8(mlstm_gate_b[:, :8]), "gate_bf": pad8(mlstm_gate_b[:, 8:]),
        "norm1_g": row(norm1_g), "norm2_g": row(norm2_g), "mlstm_norm_g": row(mlstm_norm_g),
        "mla_q_norm_g": row(mla_q_norm_g), "mla_kv_norm_g": row(mla_kv_norm_g),
        "wq": wq, "wk": wk, "wv": wv,
        "diff_lambda": diff_lambda, "diff_norm_g": row(diff_norm_g),
        "w_br_mlstm": w_br_mlstm.astype(BF16), "w_br_mla": w_br_mla.astype(BF16),
        "w_br_diff": w_br_diff.astype(BF16), "w_out": w_out.astype(BF16),
        "w_ff1": w_ff1.astype(BF16), "w_ff2": w_ff2.astype(BF16),
    }


def _rope_tables(n_tokens):
    rows = n_tokens // GRID_W
    row = jnp.repeat(jnp.arange(rows, dtype=F32), GRID_W)
    col = jnp.tile(jnp.arange(GRID_W, dtype=F32), rows)

    def cs(dim):
        quarter = dim // 4
        inv = ROPE_BASE ** (-jnp.arange(quarter, dtype=F32) / quarter)
        ang = jnp.concatenate([row[:, None] * inv, col[:, None] * inv], axis=-1)
        return jnp.cos(ang), jnp.sin(ang)

    one = lambda n: jnp.ones((n_tokens, n), F32)
    zero = lambda n: jnp.zeros((n_tokens, n), F32)
    ca, sa = cs(ROPE_A)
    cd, sd = cs(DK_D)
    return jnp.concatenate([
        one(64), ca, ca, one(32),
        zero(80), sa, zero(32),
        zero(64), -sa, zero(48),
        cd, cd, cd, cd,
        zero(32), sd, zero(32), sd,
        -sd, zero(32), -sd, zero(32)], axis=1)


def _state_mask():
    r = jnp.arange(256)[:, None] // DK_M
    c = jnp.arange(S_COLS)[None, :]
    diag = (c < H_M * DV_M) & (c // DV_M == r)
    return jnp.stack([(diag | (c == H_M * DV_M + d * H_M + r)) for d in range(2)]).astype(F32)


def _layer(x2, l, mod, wts, smask, gfin, batch, n_tok, cache=None, tables=None, ctx_stacks=None):
    latent = cache is not None
    lam_init = 0.8 - 0.6 * math.exp(-0.3 * l)
    past = cache["past"] if latent else 0
    outs = _inproj(x2, l, mod, wts, batch, n_tok, past, tables, ctx_stacks)
    m_in, q_a, k_a, v_a, q_d, k_d, v_d, gate = outs[:8]
    if latent:
        k_a, v_a, k_d, v_d = _cache_kv(l, wts, cache, batch, n_tok, (k_a, v_a, k_d, v_d))
    h_f, h_b, c_t, n_t, m_t = _mlstm(m_in, l, wts, smask, batch, n_tok, cache["state"] if latent else None)
    o_a = _mla_attention(q_a, k_a, v_a, batch, n_tok, past + n_tok)
    o_d = _diff_attention(q_d, k_d, v_d, l, wts, lam_init, batch, n_tok, past + n_tok)
    x_new = _merge(x2, l, mod, wts, batch, n_tok, latent, m_in, h_f, h_b, o_a, o_d, gate, gfin)
    return x_new, tuple(outs[8:]), (c_t, n_t, m_t)


def _cache_kv_kernel(ckv_ref, kr_ref, kd_ref, vd_ref, wk_ref, wv_ref, *refs):
    ka_out, va_out, kd_out, vd_out = refs[4:]
    ckv = ckv_ref[...].astype(BF16)
    ka_out[...] = (_dot(ckv, wk_ref[...]) + jnp.concatenate([kr_ref[...]] * H_A, axis=1)).astype(BF16)
    va_out[...] = _dot(ckv, wv_ref[...]).astype(BF16)
    kd_out[...] = kd_ref[...].astype(BF16)
    vd_out[...] = vd_ref[...].astype(BF16)


def _cache_kv(l, wts, cache, batch, n_tok, bufs):
    past = cache["past"]
    tm = ROW_TILE
    ppb = past // tm
    tiles = (past + n_tok) // tm
    cached = lambda i: (i // ppb, l, i % ppb, 0)
    out_row = lambda i: ((i // ppb) * tiles + i % ppb, 0)
    widths = (H_A * LANE, H_A * V_A, 512, 512)
    return pl.pallas_call(
        _cache_kv_kernel,
        grid=(batch * ppb,),
        in_specs=[pl.BlockSpec((None, None, tm, KV_RANK), cached),
                  pl.BlockSpec((None, None, tm, LANE), cached),
                  pl.BlockSpec((None, None, tm, 512), cached),
                  pl.BlockSpec((None, None, tm, 512), cached),
                  _layer_spec(l, (KV_RANK, H_A * LANE)), _layer_spec(l, (KV_RANK, H_A * V_A))] + [_ANY] * 4,
        out_specs=[pl.BlockSpec((tm, w), out_row) for w in widths],
        out_shape=[jax.ShapeDtypeStruct(b.shape, b.dtype) for b in bufs],
        input_output_aliases={6 + n: n for n in range(4)},
        compiler_params=_params(("parallel",)),
        name="cache_kv",
    )(cache["ckv"], cache["krope"], cache["diff_k"], cache["diff_v"], wts["wk"], wts["wv"], *bufs)


def kernel(x_prompt, x_sample, c, cache_mla_ckv, cache_mla_krope, cache_diff_k, cache_diff_v, state_mlstm_C, state_mlstm_n, state_mlstm_m, c_ctx, w_mod, b_mod, norm1_g, w_in, mlstm_gate_b, mlstm_norm_g, mla_q_norm_g, mla_w_q_up, mla_kv_norm_g, mla_w_kv_up, diff_lambda, diff_norm_g, w_br_mlstm, w_br_mla, w_br_diff, w_out, norm2_g, w_ff1, w_ff2, final_norm_g):
    bp, sp, _ = x_prompt.shape
    bs, ss, _ = x_sample.shape
    past = cache_mla_ckv.shape[2]
    assert bs + 1 <= 8 and ss % GRID_W == 0
    assert sp % ROW_TILE == 0 and ss % ROW_TILE == 0 and past % ROW_TILE == 0

    cond8 = jnp.concatenate([c_ctx[None, :], c, jnp.zeros((8 - 1 - bs, D_MODEL), F32)], axis=0)
    mod = _modulation(cond8, w_mod, b_mod).reshape(DEPTH, 8, 1, 6 * D_MODEL)
    wts = _prep_weights(w_in, mlstm_gate_b, norm1_g, mlstm_norm_g, mla_q_norm_g, mla_w_q_up, mla_kv_norm_g,
                        mla_w_kv_up, diff_lambda, diff_norm_g, w_br_mlstm, w_br_mla, w_br_diff, w_out,
                        norm2_g, w_ff1, w_ff2)
    tables = _rope_tables(ss)
    smask = _state_mask()
    gfin = final_norm_g[None, :]
    rows = H_M * DK_M
    cache = {
        "past": past,
        "ckv": cache_mla_ckv,
        "krope": jnp.pad(cache_mla_krope, ((0, 0), (0, 0), (0, 0), (NOPE_A, LANE - NOPE_A - ROPE_A))),
        "diff_k": cache_diff_k.reshape(bs, DEPTH, past, H_D * 2 * DK_D),
        "diff_v": cache_diff_v.reshape(bs, DEPTH, past, H_D * DV_D),
        "state": (state_mlstm_C.reshape(bs, DEPTH, 2, rows, DV_M),
                  state_mlstm_n.reshape(bs, DEPTH, 2, rows, 1),
                  jnp.pad(state_mlstm_m.reshape(bs, DEPTH, 1, 2 * H_M), ((0, 0), (0, 0), (0, 0), (0, LANE - 2 * H_M)))),
    }

    y_p = x_prompt.reshape(bp * sp, D_MODEL)
    y_s = x_sample.reshape(bs * ss, D_MODEL)
    stacks = None
    states = []
    for l in range(DEPTH):
        y_p, stacks, state = _layer(y_p, l, mod, wts, smask, gfin, bp, sp, ctx_stacks=stacks)
        states.append(state)
        y_s, _, _ = _layer(y_s, l, mod, wts, smask, gfin, bs, ss, cache=cache, tables=tables)

    ckv, akr, kd, vd = stacks
    c_t = jnp.stack([s[0] for s in states], axis=1).reshape(bp, DEPTH, 2, H_M, DK_M, DV_M)
    n_t = jnp.stack([s[1] for s in states], axis=1).reshape(bp, DEPTH, 2, H_M, DK_M)
    m_t = jnp.stack([s[2][:, 0, :2 * H_M] for s in states], axis=1).reshape(bp, DEPTH, 2, H_M)
    return (y_p.reshape(bp, sp, D_MODEL), y_s.reshape(bs, ss, D_MODEL),
            ckv.reshape(bp, DEPTH, sp, KV_RANK),
            akr[:, NOPE_A:NOPE_A + ROPE_A].reshape(bp, DEPTH, sp, ROPE_A),
            kd.reshape(bp, DEPTH, sp, H_D, 2 * DK_D), vd.reshape(bp, DEPTH, sp, H_D, DV_D),
            c_t, n_t, m_t)
```

```python
import functools
import math

import jax
import jax.numpy as jnp
from jax import lax
from jax.experimental import pallas as pl
from jax.experimental.pallas import tpu as pltpu

F32 = jnp.float32
BF16 = jnp.bfloat16

D_MODEL = 1024
DEPTH = 2
GRID_W = 64
ROPE_BASE = 10000.0
EPS = 1e-6
H_M, DK_M, DV_M = 4, 64, 128
H_A, Q_RANK, KV_RANK, NOPE_A, ROPE_A, V_A = 8, 384, 256, 64, 32, 64
H_D, DK_D, DV_D = 4, 64, 128
D_FF = 4 * D_MODEL

LANE = 128
VMEM_LIMIT = 56 * 1024 * 1024
ROW_TILE = 256
Q_TILE = 256
CHUNK = 128
KEY_CHUNKS = 4
SCORE_VMEM_BUDGET = 24 * 1024 * 1024
MLSTM_STREAMS = 2
NEG = -1e30
LOG2E = 1.4426950408889634

C_MQ, C_MK, C_MV, C_MO, C_GI, C_GF = 0, 256, 512, 1024, 1536, 1664
M_COLS = 1792
C_A = 1792
A_COLS = 768
C_D = 2560
D_COLS = 1536
C_G = 4096
G_COLS = 3 * D_MODEL
IN_COLS_P = 7168
S_COLS = H_M * DV_M + LANE


def _dot(a, b):
    return jnp.dot(a, b, preferred_element_type=F32)


def _dot_nt(a, b):
    return lax.dot_general(a, b, (((1,), (1,)), ((), ())), preferred_element_type=F32)


def _dot_tn(a, b):
    return lax.dot_general(a, b, (((0,), (0,)), ((), ())), preferred_element_type=F32)


def _rms_rows(x, g):
    return x * lax.rsqrt(jnp.mean(x * x, axis=-1, keepdims=True) + EPS) * g


def _rope_blocks(x, cos, sa, sb, shift):
    outs = []
    for b in range(x.shape[1] // LANE):
        xb = x[:, b * LANE:(b + 1) * LANE]
        outs.append(xb * cos + pltpu.roll(xb, shift, 1) * sa + pltpu.roll(xb, LANE - shift, 1) * sb)
    return outs[0] if len(outs) == 1 else jnp.concatenate(outs, axis=1)


def _params(sem):
    return pltpu.CompilerParams(dimension_semantics=sem, vmem_limit_bytes=VMEM_LIMIT)


def _layer_spec(l, shape):
    nd = len(shape)
    return pl.BlockSpec((None,) + shape, lambda *_: (l,) + (0,) * nd, pipeline_mode=pl.Buffered(1))


def _mod_spec(l, first_row, tiles_per_cond):
    return pl.BlockSpec((None, None, 1, 6 * D_MODEL), lambda i: (l, first_row + i // tiles_per_cond, 0, 0))


_ANY = pl.BlockSpec(memory_space=pl.ANY)


def _mod_kernel(c_ref, w_ref, b_ref, o_ref):
    c = c_ref[...]
    s = (c * jax.nn.sigmoid(c)).astype(BF16)
    o_ref[...] = _dot(s, w_ref[...].astype(BF16)) + b_ref[...]


def _modulation(cond8, w_mod, b_mod):
    tn = 1536
    n6 = 6 * D_MODEL
    return pl.pallas_call(
        _mod_kernel,
        grid=(DEPTH, n6 // tn),
        in_specs=[
            pl.BlockSpec((8, D_MODEL), lambda l, j: (0, 0)),
            pl.BlockSpec((None, D_MODEL, tn), lambda l, j: (l, 0, j)),
            pl.BlockSpec((None, 1, tn), lambda l, j: (l, 0, j)),
        ],
        out_specs=pl.BlockSpec((None, 8, tn), lambda l, j: (l, 0, j)),
        out_shape=jax.ShapeDtypeStruct((DEPTH, 8, n6), F32),
        compiler_params=_params(("parallel", "parallel")),
        name="modulation",
    )(cond8, w_mod, b_mod.reshape(DEPTH, 1, n6))


def _store_slots(ref, val):
    if len(ref.shape) == 3:
        for s in range(ref.shape[0]):
            ref[s] = val
    else:
        ref[...] = val


def _inproj_kernel(latent, n_aliased, *refs):
    (x_ref, mod_ref, g1_ref, w_ref, gq_ref, gkv_ref, wq_ref, wk_ref, wv_ref) = refs[:9]
    refs = refs[9:]
    if latent:
        tab_ref = refs[0]
        (m_ref, q_ref, k_ref, va_ref, qd_ref, kd_ref, vd_ref, gate_ref) = refs[1:]
    else:
        (m_ref, q_ref, k_ref, va_ref, qd_ref, kd_ref, vd_ref, gate_ref,
         ckv_ref, akr_ref, kdraw_ref, vdraw_ref) = refs[n_aliased:]
    d = D_MODEL
    x = x_ref[...]
    mod = mod_ref[...]
    h = (_rms_rows(x, g1_ref[...]) * (1.0 + mod[:, d:2 * d]) + mod[:, 0:d]).astype(BF16)

    m_ref[...] = _dot(h, w_ref[:, 0:M_COLS])

    za = _dot(h, w_ref[:, C_A:C_A + A_COLS])
    acq = za[:, 0:Q_RANK]
    ackv = za[:, Q_RANK:Q_RANK + KV_RANK]
    akr = za[:, Q_RANK + KV_RANK:A_COLS]
    q = _dot(_rms_rows(acq, gq_ref[...]).astype(BF16), wq_ref[...])
    ckv = _rms_rows(ackv, gkv_ref[...])
    ckv_b = ckv.astype(BF16)
    kn = _dot(ckv_b, wk_ref[...])
    if not latent:
        _store_slots(ckv_ref, ckv)
        _store_slots(akr_ref, akr)
    else:
        tab = tab_ref[...]
        cq, saq, sbq = tab[:, 0:128], tab[:, 128:256], tab[:, 256:384]
        q = _rope_blocks(q, cq, saq, sbq, ROPE_A // 2)
        akr = _rope_blocks(akr, cq, saq, sbq, ROPE_A // 2)
    q_ref[...] = (q * (LOG2E * (NOPE_A + ROPE_A) ** -0.5)).astype(BF16)
    k_ref[...] = (kn + jnp.concatenate([akr] * H_A, axis=1)).astype(BF16)
    va_ref[...] = _dot(ckv_b, wv_ref[...]).astype(BF16)

    zd = _dot(h, w_ref[:, C_D:C_D + D_COLS])
    dq, dk, dv = zd[:, 0:512], zd[:, 512:1024], zd[:, 1024:1536]
    if not latent:
        _store_slots(kdraw_ref, dk)
        _store_slots(vdraw_ref, dv)
    else:
        cd, sad, sbd = tab[:, 384:512], tab[:, 512:640], tab[:, 640:768]
        dq = _rope_blocks(dq, cd, sad, sbd, DK_D // 2)
        dk = _rope_blocks(dk, cd, sad, sbd, DK_D // 2)
    qd_ref[...] = (dq * (LOG2E * DK_D ** -0.5)).astype(BF16)
    kd_ref[...] = dk.astype(BF16)
    vd_ref[...] = dv.astype(BF16)

    gate_ref[...] = _dot(h, w_ref[:, C_G:C_G + G_COLS])


def _inproj(x2, l, mod, wts, batch, n_tok, tables=None, ctx_stacks=None):
    latent = tables is not None
    t = batch * n_tok
    tm = ROW_TILE
    tpb = n_tok // tm
    row = lambda i: (i, 0)

    in_specs = [
        pl.BlockSpec((tm, D_MODEL), row),
        _mod_spec(l, 1, tpb) if latent else _mod_spec(l, 0, batch * tpb),
        _layer_spec(l, (1, D_MODEL)),
        _layer_spec(l, (D_MODEL, IN_COLS_P)),
        _layer_spec(l, (1, Q_RANK)),
        _layer_spec(l, (1, KV_RANK)),
        _layer_spec(l, (Q_RANK, H_A * LANE)),
        _layer_spec(l, (KV_RANK, H_A * LANE)),
        _layer_spec(l, (KV_RANK, H_A * V_A)),
    ]
    args = [x2, mod, wts["norm1_g"], wts["w_in"], wts["mla_q_norm_g"], wts["mla_kv_norm_g"],
            wts["wq"], wts["wk"], wts["wv"]]
    widths = [(M_COLS, F32), (H_A * LANE, BF16), (H_A * LANE, BF16), (H_A * V_A, BF16),
              (512, BF16), (512, BF16), (512, BF16), (G_COLS, F32)]
    out_specs = [pl.BlockSpec((tm, w), row) for w, _ in widths]
    out_shape = [jax.ShapeDtypeStruct((t, w), dt) for w, dt in widths]
    aliases = {}
    if latent:
        in_specs.append(pl.BlockSpec((tm, 768), lambda i: (i % tpb, 0)))
        args.append(tables)
    else:
        if ctx_stacks is None:
            stack_spec = lambda w: pl.BlockSpec((None, DEPTH, tm, w), lambda i: (i // tpb, 0, i % tpb, 0))
        else:
            stack_spec = lambda w: pl.BlockSpec((None, None, tm, w), lambda i: (i // tpb, l, i % tpb, 0))
            aliases = {len(args) + n: 8 + n for n in range(4)}
            in_specs += [_ANY] * 4
            args += list(ctx_stacks)
        for w in (KV_RANK, LANE, 512, 512):
            out_specs.append(stack_spec(w))
            out_shape.append(jax.ShapeDtypeStruct((batch, DEPTH, n_tok, w), F32))
    return pl.pallas_call(
        functools.partial(_inproj_kernel, latent, len(aliases)),
        grid=(t // tm,),
        in_specs=in_specs,
        out_specs=out_specs,
        out_shape=out_shape,
        input_output_aliases=aliases,
        compiler_params=_params(("parallel",)),
        name="inproj_lat" if latent else "inproj_ctx",
    )(*args)


def _mlstm_chunks(items, bi, bfg, smask_ref):
    L = CHUNK
    r_i = lax.broadcasted_iota(jnp.int32, (L, L), 0)
    c_i = lax.broadcasted_iota(jnp.int32, (L, L), 1)
    lane256 = lax.broadcasted_iota(jnp.int32, (L, 256), 1)
    lane128 = lax.broadcasted_iota(jnp.int32, (1, LANE), 1)
    masks = {0: c_i <= r_i, 1: c_i >= r_i}
    tris = {d: jnp.where(m, 1.0, 0.0).astype(BF16) for d, m in masks.items()}
    heads = [(n, hd) for n in range(len(items)) for hd in range(H_M)]

    pre = []
    for z, s_prev, m_prev, d in items:
        q = (z[:, C_MQ:C_MQ + 256] * DK_M ** -0.5).astype(BF16)
        k = z[:, C_MK:C_MK + 256]
        v = z[:, C_MV:C_MV + 512].astype(BF16)
        gi = z[:, C_GI:C_GI + LANE] + bi
        xf = z[:, C_GF:C_GF + LANE] + bfg
        lf = jnp.minimum(xf, 0.0) - jnp.log(1.0 + jnp.exp(-jnp.abs(xf)))
        lf_hi = lf.astype(BF16)
        r1 = lf - lf_hi.astype(F32)
        lf_mid = r1.astype(BF16)
        lf_lo = (r1 - lf_mid.astype(F32)).astype(BF16)
        b = _dot(tris[d], lf_hi) + _dot(tris[d], lf_mid) + _dot(tris[d], lf_lo)
        g = gi - b
        pre.append(dict(q=q, k=k, kb=k.astype(BF16), v=v, b=b, g=g, g_t=g.T,
                        qs=_dot(q, s_prev.astype(BF16)), last=L - 1 if d == 0 else 0))

    gm, u_col = {}, {}
    for n, hd in heads:
        d = items[n][3]
        j = d * H_M + hd
        gm[n, hd] = jnp.where(masks[d], pre[n]["g_t"][j:j + 1, :], NEG)
        u_col[n, hd] = jnp.maximum(jnp.max(gm[n, hd], axis=-1, keepdims=True), items[n][2][:, j:j + 1])
    a = {}
    for n, hd in heads:
        q = pre[n]["q"]
        q_h = jnp.where((lane256 >= hd * DK_M) & (lane256 < (hd + 1) * DK_M), q, jnp.zeros_like(q))
        a[n, hd] = jnp.exp(gm[n, hd] - u_col[n, hd]) * _dot_nt(q_h, pre[n]["kb"])
    den_i, num_i = {}, {}
    for n, hd in heads:
        den_i[n, hd] = jnp.sum(a[n, hd], axis=-1, keepdims=True)
        num_i[n, hd] = _dot(a[n, hd].astype(BF16), pre[n]["v"][:, hd * DV_M:(hd + 1) * DV_M])

    results = []
    for n, (z, s_prev, m_prev, d) in enumerate(items):
        p = pre[n]
        h_out, w_cols, w_old = [], [], []
        m_new = m_prev
        for hd in range(H_M):
            j = d * H_M + hd
            u = u_col[n, hd]
            mp = m_prev[:, j:j + 1]
            b_col = p["b"][:, j:j + 1]
            w_int = jnp.exp(mp - u)
            num = w_int * p["qs"][:, hd * DV_M:(hd + 1) * DV_M] + num_i[n, hd]
            den = w_int * p["qs"][:, H_M * DV_M + j:H_M * DV_M + j + 1] + den_i[n, hd]
            h_out.append(num / jnp.maximum(jnp.abs(den), jnp.exp(-(b_col + u))))
            u_l = u[p["last"]:p["last"] + 1, :]
            b_l = b_col[p["last"]:p["last"] + 1, :]
            m_new = jnp.where(lane128 == j, b_l + u_l, m_new)
            w_cols.append(jnp.exp(p["g"][:, j:j + 1] - u_l))
            w_old.append(jnp.exp(mp - u_l))
        w_exp = jnp.where(lane256 < 64, w_cols[0],
                          jnp.where(lane256 < 128, w_cols[1], jnp.where(lane256 < 192, w_cols[2], w_cols[3])))
        k_w = (p["k"] * w_exp).astype(BF16)
        v_aug = jnp.concatenate([p["v"], jnp.ones((L, LANE), BF16)], axis=1)
        upd = _dot_tn(k_w, v_aug)
        n_scale = jnp.zeros((1, LANE), F32)
        for hd in range(H_M):
            n_scale = jnp.where(lane128 == d * H_M + hd, w_old[hd], n_scale)
        scale_row = jnp.concatenate([jnp.broadcast_to(w, (1, DV_M)) for w in w_old] + [n_scale], axis=1)
        results.append((jnp.concatenate(h_out, axis=1), scale_row * s_prev + smask_ref[d] * upd, m_new))
    return results


def _mlstm_kernel(has_state, *refs):
    zf_ref, zb_ref, bi_ref, bf_ref, smask_ref = refs[:5]
    refs = refs[5:]
    if has_state:
        c0_ref, n0_ref, m0_ref = refs[:3]
        refs = refs[3:]
    hf_ref, hb_ref, ct_ref, nt_ref, mt_ref, s_scr, m_scr = refs
    i = pl.program_id(1)
    hv = H_M * DV_M

    streams = s_scr.shape[0]

    @pl.when(i == 0)
    def _():
        if has_state:
            for g in range(streams):
                for d in range(2):
                    tiled = jnp.concatenate(
                        [c0_ref[g, d]] * H_M + [jnp.broadcast_to(n0_ref[g, d], (256, LANE))], axis=1)
                    s_scr[g, d] = smask_ref[d] * tiled
                    m_scr[g, d] = m0_ref[g]
        else:
            s_scr[...] = jnp.zeros_like(s_scr)
            m_scr[...] = jnp.zeros_like(m_scr)

    bi = bi_ref[...]
    bfg = bf_ref[...]
    slots = [(g, d) for g in range(streams) for d in range(2)]
    items = [((zf_ref, zb_ref)[d][g], s_scr[g, d], m_scr[g, d], d) for g, d in slots]
    for (g, d), (h, s_new, m_new) in zip(slots, _mlstm_chunks(items, bi, bfg, smask_ref)):
        (hf_ref, hb_ref)[d][g] = h
        s_scr[g, d] = s_new
        m_scr[g, d] = m_new

    @pl.when(i == pl.num_programs(1) - 1)
    def _():
        lane = lax.broadcasted_iota(jnp.int32, (1, LANE), 1)
        for g in range(streams):
            for d in range(2):
                s = s_scr[g, d]
                ct_ref[g, d] = ((s[:, 0:DV_M] + s[:, DV_M:2 * DV_M])
                                + (s[:, 2 * DV_M:3 * DV_M] + s[:, 3 * DV_M:hv]))
                nt_ref[g, d] = jnp.sum(s[:, hv:], axis=-1, keepdims=True)
            mt_ref[g] = jnp.where(lane < H_M, m_scr[g, 0], m_scr[g, 1])


def _mlstm(m_in, l, wts, smask, batch, n_tok, state=None):
    nc = n_tok // CHUNK
    hv = H_M * DV_M
    g = MLSTM_STREAMS if batch % MLSTM_STREAMS == 0 else 1
    z3 = m_in.reshape(batch, n_tok, M_COLS)
    fwd = lambda b, i: (b, i, 0)
    bwd = lambda b, i: (b, nc - 1 - i, 0)
    in_specs = [
        pl.BlockSpec((g, CHUNK, M_COLS), fwd),
        pl.BlockSpec((g, CHUNK, M_COLS), bwd),
        pl.BlockSpec((None, 1, LANE), lambda b, i: (l, 0, 0)),
        pl.BlockSpec((None, 1, LANE), lambda b, i: (l, 0, 0)),
        pl.BlockSpec((2, 256, S_COLS), lambda b, i: (0, 0, 0)),
    ]
    args = [z3, z3, wts["gate_bi"], wts["gate_bf"], smask]
    if state is not None:
        in_specs += [
            pl.BlockSpec((g, None, 2, 256, DV_M), lambda b, i: (b, l, 0, 0, 0)),
            pl.BlockSpec((g, None, 2, 256, 1), lambda b, i: (b, l, 0, 0, 0)),
            pl.BlockSpec((g, None, 1, LANE), lambda b, i: (b, l, 0, 0)),
        ]
        args += list(state)
    h_f, h_b, c_t, n_t, m_t = pl.pallas_call(
        functools.partial(_mlstm_kernel, state is not None),
        grid=(batch // g, nc),
        in_specs=in_specs,
        out_specs=[
            pl.BlockSpec((g, CHUNK, hv), fwd),
            pl.BlockSpec((g, CHUNK, hv), bwd),
            pl.BlockSpec((g, 2, 256, DV_M), lambda b, i: (b, 0, 0, 0)),
            pl.BlockSpec((g, 2, 256, 1), lambda b, i: (b, 0, 0, 0)),
            pl.BlockSpec((g, 1, LANE), lambda b, i: (b, 0, 0)),
        ],
        out_shape=[
            jax.ShapeDtypeStruct((batch, n_tok, hv), F32),
            jax.ShapeDtypeStruct((batch, n_tok, hv), F32),
            jax.ShapeDtypeStruct((batch, 2, 256, DV_M), F32),
            jax.ShapeDtypeStruct((batch, 2, 256, 1), F32),
            jax.ShapeDtypeStruct((batch, 1, LANE), F32),
        ],
        scratch_shapes=[pltpu.VMEM((g, 2, 256, S_COLS), F32), pltpu.VMEM((g, 2, 1, LANE), F32)],
        compiler_params=_params(("parallel", "arbitrary")),
        name="mlstm",
    )(*args)
    return h_f.reshape(batch * n_tok, hv), h_b.reshape(batch * n_tok, hv), c_t, n_t, m_t


def _groups_per_step(n_groups, n_maps, n_kv, tq):
    return n_groups if 2 * n_groups * n_maps * n_kv * tq * 4 <= SCORE_VMEM_BUDGET else 1


def _pipelined_attention(kernel_fn, q, kv_new, kv_cache, extra_inputs, extra_specs, batch, n_q, n_groups,
                         widths, n_maps, name):
    past = 0 if kv_cache is None else kv_cache[0].shape[0] // batch
    n_kv = past + n_q
    tq = min(Q_TILE, n_q)
    nq = n_q // tq
    gps = _groups_per_step(n_groups, n_maps, n_kv, tq)
    n_groups //= gps
    n_maps *= gps
    q_w, k_w, v_w, out_w = (w * gps for w in widths)
    n_units = batch * n_groups * nq
    nxt = lambda s: jnp.minimum(s, n_units - 1)
    cur = lambda s: jnp.maximum(s - 1, 0)
    row_blk = lambda u: (u // (n_groups * nq)) * nq + u % nq
    grp = lambda u: (u // nq) % n_groups
    bat = lambda u: u // (n_groups * nq)
    k_map = lambda s: (bat(nxt(s)), grp(nxt(s)))
    v_map = lambda s: (bat(cur(s)), grp(cur(s)))
    in_specs = [pl.BlockSpec((tq, q_w), lambda s: (row_blk(nxt(s)), grp(nxt(s)))),
                pl.BlockSpec((n_q, k_w), k_map), pl.BlockSpec((n_q, v_w), v_map)]
    inputs = [q, *kv_new]
    if past:
        in_specs += [pl.BlockSpec((past, k_w), k_map), pl.BlockSpec((past, v_w), v_map)]
        inputs += list(kv_cache)
    score = pltpu.VMEM((n_maps, n_kv, tq), F32)
    cmax = pltpu.VMEM((n_maps, 1, tq), F32)
    return pl.pallas_call(
        functools.partial(kernel_fn, gps, nq, past),
        grid=(n_units + 1,),
        in_specs=in_specs + extra_specs(lambda s: grp(cur(s)), gps),
        out_specs=pl.BlockSpec((tq, out_w), lambda s: (row_blk(cur(s)), grp(cur(s)))),
        out_shape=jax.ShapeDtypeStruct((batch * n_q, n_groups * out_w), BF16),
        scratch_shapes=[score, cmax, score, cmax, pltpu.VMEM((v_w, n_kv), BF16)],
        compiler_params=_params(("arbitrary",)),
        name=name,
    )(*inputs, *extra_inputs)


def _attention_refs(past, refs, n_extra):
    q_ref, k_ref, v_ref = refs[:3]
    refs = refs[3:]
    kc_ref = vc_ref = None
    if past:
        kc_ref, vc_ref = refs[:2]
        refs = refs[2:]
    return (q_ref, k_ref, v_ref, kc_ref, vc_ref), refs[:n_extra], refs[n_extra:]


def _pipeline_prologue(nq, past, v_ref, vc_ref, s_b, m_b, vt_scr):
    s = pl.program_id(0)

    @pl.when(s == 0)
    def _():
        s_b[...] = jnp.zeros_like(s_b)
        m_b[...] = jnp.zeros_like(m_b)

    @pl.when(lax.rem(jnp.maximum(s - 1, 0), nq) == 0)
    def _():
        if past:
            vt_scr[:, 0:past] = vc_ref[...].T
        vt_scr[:, past:] = v_ref[...].T

    return lax.rem(s, 2)


def _key_chunks(n_kv, past):
    blocks = (n_kv - past) // LANE
    n = min(KEY_CHUNKS, blocks)
    edges = ([0] if past else []) + [past + (i * blocks // n) * LANE for i in range(n + 1)]
    return list(zip(edges[:-1], edges[1:]))


def _keys_of(k_ref, kc_ref, past, lanes):
    def k_of(c0, c1):
        return kc_ref[c0:c1, lanes] if c1 <= past else k_ref[c0 - past:c1 - past, lanes]
    return k_of


def _score_maps(maps, past, s_n, m_n, s_c, m_c):
    out = []
    for a, (k_of, q, vt_of) in enumerate(maps):
        mc = m_c[a]
        mx = l = acc = None
        for c0, c1 in _key_chunks(s_n.shape[1], past):
            st = _dot_nt(k_of(c0, c1), q)
            s_n[a, c0:c1, :] = st
            cm = jnp.max(st, axis=0, keepdims=True)
            mx = cm if mx is None else jnp.maximum(mx, cm)
            p = jnp.exp2(s_c[a, c0:c1, :] - mc)
            ps = jnp.sum(p, axis=0, keepdims=True)
            l = ps if l is None else l + ps
            pv = _dot(vt_of(c0, c1), p.astype(BF16))
            acc = pv if acc is None else acc + pv
        m_n[a] = mx
        out.append((acc, l))
    return out


def _mla_kernel(gps, nq, past, *refs):
    (q_ref, k_ref, v_ref, kc_ref, vc_ref), _, (o_ref, s_a, m_a, s_b, m_b, vt_scr) = _attention_refs(past, refs, 0)
    parity = _pipeline_prologue(nq, past, v_ref, vc_ref, s_b, m_b, vt_scr)

    def body(s_n, m_n, s_c, m_c):
        maps = []
        for e in range(2 * gps):
            lanes = slice(e * LANE, (e + 1) * LANE)
            maps.append((_keys_of(k_ref, kc_ref, past, lanes), q_ref[:, lanes],
                         lambda c0, c1, e=e: vt_scr[e * V_A:(e + 1) * V_A, c0:c1]))
        outs = [acc / l for acc, l in _score_maps(maps, past, s_n, m_n, s_c, m_c)]
        o_ref[...] = jnp.concatenate(outs, axis=0).T.astype(BF16)

    @pl.when(parity == 0)
    def _():
        body(s_a, m_a, s_b, m_b)

    @pl.when(parity == 1)
    def _():
        body(s_b, m_b, s_a, m_a)


def _mla_attention(q, kv_new, kv_cache, batch, n_q):
    return _pipelined_attention(_mla_kernel, q, kv_new, kv_cache, (), lambda cur_grp, gps: [], batch, n_q,
                                H_A // 2, (2 * LANE, 2 * LANE, LANE, LANE), 2, "mla_attention")


def _diff_kernel(lam_init, gps, nq, past, *refs):
    ((q_ref, k_ref, v_ref, kc_ref, vc_ref), (lam_ref, g_ref),
     (o_ref, s_a, m_a, s_b, m_b, vt_scr)) = _attention_refs(past, refs, 2)
    parity = _pipeline_prologue(nq, past, v_ref, vc_ref, s_b, m_b, vt_scr)

    def body(s_n, m_n, s_c, m_c):
        lane = lax.broadcasted_iota(jnp.int32, (1, LANE), 1)
        lv = lam_ref[...]
        lam = (jnp.exp(jnp.sum(lv[0:1] * lv[1:2], axis=-1, keepdims=True))
               - jnp.exp(jnp.sum(lv[2:3] * lv[3:4], axis=-1, keepdims=True)) + lam_init)
        maps = []
        for h in range(gps):
            lanes = slice(h * LANE, (h + 1) * LANE)
            q = q_ref[:, lanes]
            zero = jnp.zeros_like(q)
            k_of = _keys_of(k_ref, kc_ref, past, lanes)
            vt_of = lambda c0, c1, h=h: vt_scr[h * LANE:(h + 1) * LANE, c0:c1]
            maps += [(k_of, jnp.where(lane < DK_D, q, zero), vt_of), (k_of, jnp.where(lane >= DK_D, q, zero), vt_of)]
        res = _score_maps(maps, past, s_n, m_n, s_c, m_c)
        outs = []
        for h in range(gps):
            (acc1, l1), (acc2, l2) = res[2 * h], res[2 * h + 1]
            o = (acc1 / l1 - acc2 * (lam / l2)).T
            outs.append(_rms_rows(o, g_ref[:, h * LANE:(h + 1) * LANE]) * (1.0 - lam_init))
        o_ref[...] = (outs[0] if gps == 1 else jnp.concatenate(outs, axis=1)).astype(BF16)

    @pl.when(parity == 0)
    def _():
        body(s_a, m_a, s_b, m_b)

    @pl.when(parity == 1)
    def _():
        body(s_b, m_b, s_a, m_a)


def _diff_attention(q, kv_new, kv_cache, l, wts, lam_init, batch, n_q):
    extra = lambda cur_grp, gps: [pl.BlockSpec((None, 4, DK_D), lambda s: (l, 0, 0)),
                                  pl.BlockSpec((None, 1, gps * LANE), lambda s: (l, 0, cur_grp(s)))]
    return _pipelined_attention(functools.partial(_diff_kernel, lam_init), q, kv_new, kv_cache,
                                (wts["diff_lambda"], wts["diff_norm_g"]), extra, batch, n_q, H_D,
                                (LANE, LANE, LANE, LANE), 2, "diff_attention")


def _merge_kernel(final, x_ref, mod_ref, hf_ref, hb_ref, mo_ref, oa_ref, od_ref, gate_ref,
                  gm_ref, wbm_ref, wba_ref, wbd_ref, wout_ref, g2_ref, wff1_ref, wff2_ref, gfin_ref, o_ref):
    d = D_MODEL
    mod = mod_ref[...]
    hm = hf_ref[...] + hb_ref[...]
    gm = gm_ref[...]
    o_m = jnp.concatenate(
        [_rms_rows(hm[:, h * DV_M:(h + 1) * DV_M], gm[:, h * DV_M:(h + 1) * DV_M]) for h in range(H_M)], axis=1)
    o_m = (o_m * jax.nn.sigmoid(mo_ref[...])).astype(BF16)
    gate = jax.nn.sigmoid(gate_ref[...])
    y = (gate[:, 0:d] * _dot(o_m, wbm_ref[...]) + gate[:, d:2 * d] * _dot(oa_ref[...], wba_ref[...])
         + gate[:, 2 * d:3 * d] * _dot(od_ref[...], wbd_ref[...]))
    x = x_ref[...] + mod[:, 2 * d:3 * d] * _dot(y.astype(BF16), wout_ref[...])
    h2 = (_rms_rows(x, g2_ref[...]) * (1.0 + mod[:, 4 * d:5 * d]) + mod[:, 3 * d:4 * d]).astype(BF16)
    f = jnp.maximum(_dot(h2, wff1_ref[...]), 0.0)
    x = x + mod[:, 5 * d:6 * d] * _dot((f * f).astype(BF16), wff2_ref[...])
    if final:
        x = _rms_rows(x, gfin_ref[...])
    o_ref[...] = x


def _merge(x2, l, mod, wts, batch, n_tok, latent, m_in, h_f, h_b, o_a, o_d, gate, gfin):
    t = x2.shape[0]
    tm = ROW_TILE
    tpb = n_tok // tm
    row = lambda i: (i, 0)
    in_specs = [
        pl.BlockSpec((tm, D_MODEL), row),
        _mod_spec(l, 1, tpb) if latent else _mod_spec(l, 0, batch * tpb),
        pl.BlockSpec((tm, 512), row),
        pl.BlockSpec((tm, 512), row),
        pl.BlockSpec((tm, 512), lambda i: (i, C_MO // 512)),
        pl.BlockSpec((tm, 512), row),
        pl.BlockSpec((tm, 512), row),
        pl.BlockSpec((tm, G_COLS), row),
        _layer_spec(l, (1, 512)),
        _layer_spec(l, (512, D_MODEL)),
        _layer_spec(l, (512, D_MODEL)),
        _layer_spec(l, (512, D_MODEL)),
        _layer_spec(l, (D_MODEL, D_MODEL)),
        _layer_spec(l, (1, D_MODEL)),
        _layer_spec(l, (D_MODEL, D_FF)),
        _layer_spec(l, (D_FF, D_MODEL)),
        pl.BlockSpec((1, D_MODEL), lambda i: (0, 0)),
    ]
    return pl.pallas_call(
        functools.partial(_merge_kernel, l == DEPTH - 1),
        grid=(t // tm,),
        in_specs=in_specs,
        out_specs=pl.BlockSpec((tm, D_MODEL), row),
        out_shape=jax.ShapeDtypeStruct((t, D_MODEL), F32),
        compiler_params=_params(("parallel",)),
        name="merge_mlp",
    )(x2, mod, h_f, h_b, m_in, o_a, o_d, gate, wts["mlstm_norm_g"], wts["w_br_mlstm"], wts["w_br_mla"],
      wts["w_br_diff"], wts["w_out"], wts["norm2_g"], wts["w_ff1"], wts["w_ff2"], gfin)


def _prep_weights(w_in, mlstm_gate_b, norm1_g, mlstm_norm_g, mla_q_norm_g, mla_w_q_up, mla_kv_norm_g,
                  mla_w_kv_up, diff_lambda, diff_norm_g, w_br_mlstm, w_br_mla, w_br_diff, w_out,
                  norm2_g, w_ff1, w_ff2):
    depth = w_in.shape[0]
    z = lambda n: jnp.zeros((depth, D_MODEL, n), BF16)
    wb = w_in.astype(BF16)
    c_mg = 2 * H_M * DK_M + 2 * H_M * DV_M
    c_acq = c_mg + 4 * H_M
    c_akr = c_acq + Q_RANK + KV_RANK
    c_dq = c_akr + ROPE_A
    w_p = jnp.concatenate([
        wb[..., :c_mg], wb[..., c_mg:c_mg + 8], z(LANE - 8), wb[..., c_mg + 8:c_mg + 16], z(LANE - 8),
        wb[..., c_acq:c_akr], z(NOPE_A), wb[..., c_akr:c_dq], z(LANE - NOPE_A - ROPE_A), wb[..., c_dq:]], axis=2)
    pad8 = lambda a: jnp.pad(a, ((0, 0), (0, LANE - 8)))[:, None, :]
    wq = mla_w_q_up.astype(BF16).reshape(depth, Q_RANK, H_A, NOPE_A + ROPE_A)
    wq = jnp.pad(wq, ((0, 0), (0, 0), (0, 0), (0, LANE - NOPE_A - ROPE_A))).reshape(depth, Q_RANK, H_A * LANE)
    wkv = mla_w_kv_up.astype(BF16).reshape(depth, KV_RANK, H_A, NOPE_A + V_A)
    wk = jnp.pad(wkv[..., :NOPE_A], ((0, 0), (0, 0), (0, 0), (0, LANE - NOPE_A))).reshape(depth, KV_RANK, H_A * LANE)
    wv = wkv[..., NOPE_A:].reshape(depth, KV_RANK, H_A * V_A)
    row = lambda a: a[:, None, :]
    return {
        "w_in": w_p,
        "gate_bi": pad8(mlstm_gate_b[:, :8]), "gate_bf": pad8(mlstm_gate_b[:, 8:]),
        "norm1_g": row(norm1_g), "norm2_g": row(norm2_g), "mlstm_norm_g": row(mlstm_norm_g),
        "mla_q_norm_g": row(mla_q_norm_g), "mla_kv_norm_g": row(mla_kv_norm_g),
        "wq": wq, "wk": wk, "wv": wv,
        "diff_lambda": diff_lambda, "diff_norm_g": row(diff_norm_g),
        "w_br_mlstm": w_br_mlstm.astype(BF16), "w_br_mla": w_br_mla.astype(BF16),
        "w_br_diff": w_br_diff.astype(BF16), "w_out": w_out.astype(BF16),
        "w_ff1": w_ff1.astype(BF16), "w_ff2": w_ff2.astype(BF16),
    }


def _rope_tables(n_tokens):
    def freqs(dim):
        quarter = dim // 4
        inv = (ROPE_BASE ** (-jnp.arange(quarter, dtype=F32) / quarter))
        z = jnp.zeros((quarter,), F32)
        return jnp.concatenate([inv, z]), jnp.concatenate([z, inv])

    def lanes(pieces):
        return jnp.concatenate([jnp.zeros((p,), F32) if isinstance(p, int) else p for p in pieces])

    ra, ca = freqs(ROPE_A)
    rd, cd = freqs(DK_D)
    blocks = [
        ([64, ra, ra, 32], [64, ca, ca, 32], 1.0),
        ([80, ra, 32], [80, ca, 32], 2.0),
        ([64, ra, 48], [64, ca, 48], -2.0),
        ([rd, rd, rd, rd], [cd, cd, cd, cd], 1.0),
        ([32, rd, 32, rd], [32, cd, 32, cd], 2.0),
        ([rd, 32, rd, 32], [cd, 32, cd, 32], -2.0),
    ]
    f_row = jnp.concatenate([lanes(b[0]) for b in blocks])[None, :]
    f_col = jnp.concatenate([lanes(b[1]) for b in blocks])[None, :]
    kind = jnp.concatenate([jnp.full((LANE,), b[2], F32) for b in blocks])[None, :]
    tok = jnp.arange(n_tokens, dtype=jnp.int32)[:, None]
    ang = (tok // GRID_W).astype(F32) * f_row + (tok % GRID_W).astype(F32) * f_col
    return jnp.where(kind == 1.0, jnp.cos(ang), jnp.sin(ang) * (0.5 * kind))


def _state_mask():
    r = jnp.arange(256)[:, None] // DK_M
    c = jnp.arange(S_COLS)[None, :]
    diag = (c < H_M * DV_M) & (c // DV_M == r)
    return jnp.stack([(diag | (c == H_M * DV_M + d * H_M + r)) for d in range(2)]).astype(F32)


def _layer(x2, l, mod, wts, smask, gfin, batch, n_tok, cache=None, tables=None, ctx_stacks=None):
    latent = cache is not None
    lam_init = 0.8 - 0.6 * math.exp(-0.3 * l)
    outs = _inproj(x2, l, mod, wts, batch, n_tok, tables, ctx_stacks)
    m_in, q_a, k_a, v_a, q_d, k_d, v_d, gate = outs[:8]
    kv_a = kv_d = None
    if latent:
        ck_a, cv_a, ck_d, cv_d = _cache_kv(l, wts, cache, batch)
        kv_a, kv_d = (ck_a, cv_a), (ck_d, cv_d)
    h_f, h_b, c_t, n_t, m_t = _mlstm(m_in, l, wts, smask, batch, n_tok, cache["state"] if latent else None)
    o_a = _mla_attention(q_a, (k_a, v_a), kv_a, batch, n_tok)
    o_d = _diff_attention(q_d, (k_d, v_d), kv_d, l, wts, lam_init, batch, n_tok)
    x_new = _merge(x2, l, mod, wts, batch, n_tok, latent, m_in, h_f, h_b, o_a, o_d, gate, gfin)
    return x_new, tuple(outs[8:]), (c_t, n_t, m_t)


def _cache_kv_kernel(ckv_ref, kr_ref, kd_ref, vd_ref, wk_ref, wv_ref, ka_out, va_out, kd_out, vd_out):
    ckv = ckv_ref[...].astype(BF16)
    ka_out[...] = (_dot(ckv, wk_ref[...]) + jnp.concatenate([kr_ref[...]] * H_A, axis=1)).astype(BF16)
    va_out[...] = _dot(ckv, wv_ref[...]).astype(BF16)
    kd_out[...] = kd_ref[...].astype(BF16)
    vd_out[...] = vd_ref[...].astype(BF16)


def _cache_kv(l, wts, cache, batch):
    past = cache["past"]
    tm = ROW_TILE
    ppb = past // tm
    cached = lambda i: (i // ppb, l, i % ppb, 0)
    widths = (H_A * LANE, H_A * V_A, 512, 512)
    return pl.pallas_call(
        _cache_kv_kernel,
        grid=(batch * ppb,),
        in_specs=[pl.BlockSpec((None, None, tm, KV_RANK), cached),
                  pl.BlockSpec((None, None, tm, LANE), cached),
                  pl.BlockSpec((None, None, tm, 512), cached),
                  pl.BlockSpec((None, None, tm, 512), cached),
                  _layer_spec(l, (KV_RANK, H_A * LANE)), _layer_spec(l, (KV_RANK, H_A * V_A))],
        out_specs=[pl.BlockSpec((tm, w), lambda i: (i, 0)) for w in widths],
        out_shape=[jax.ShapeDtypeStruct((batch * past, w), BF16) for w in widths],
        compiler_params=_params(("parallel",)),
        name="cache_kv",
    )(cache["ckv"], cache["krope"], cache["diff_k"], cache["diff_v"], wts["wk"], wts["wv"])


def kernel(x_prompt, x_sample, c, cache_mla_ckv, cache_mla_krope, cache_diff_k, cache_diff_v, state_mlstm_C, state_mlstm_n, state_mlstm_m, c_ctx, w_mod, b_mod, norm1_g, w_in, mlstm_gate_b, mlstm_norm_g, mla_q_norm_g, mla_w_q_up, mla_kv_norm_g, mla_w_kv_up, diff_lambda, diff_norm_g, w_br_mlstm, w_br_mla, w_br_diff, w_out, norm2_g, w_ff1, w_ff2, final_norm_g):
    bp, sp, _ = x_prompt.shape
    bs, ss, _ = x_sample.shape
    past = cache_mla_ckv.shape[2]
    assert bs + 1 <= 8 and ss % GRID_W == 0
    assert sp % ROW_TILE == 0 and ss % ROW_TILE == 0 and past % ROW_TILE == 0

    cond8 = jnp.concatenate([c_ctx[None, :], c, jnp.zeros((8 - 1 - bs, D_MODEL), F32)], axis=0)
    mod = _modulation(cond8, w_mod, b_mod).reshape(DEPTH, 8, 1, 6 * D_MODEL)
    wts = _prep_weights(w_in, mlstm_gate_b, norm1_g, mlstm_norm_g, mla_q_norm_g, mla_w_q_up, mla_kv_norm_g,
                        mla_w_kv_up, diff_lambda, diff_norm_g, w_br_mlstm, w_br_mla, w_br_diff, w_out,
                        norm2_g, w_ff1, w_ff2)
    tables = _rope_tables(ss)
    smask = _state_mask()
    gfin = final_norm_g[None, :]
    rows = H_M * DK_M
    cache = {
        "past": past,
        "ckv": cache_mla_ckv,
        "krope": jnp.pad(cache_mla_krope, ((0, 0), (0, 0), (0, 0), (NOPE_A, LANE - NOPE_A - ROPE_A))),
        "diff_k": cache_diff_k.reshape(bs, DEPTH, past, H_D * 2 * DK_D),
        "diff_v": cache_diff_v.reshape(bs, DEPTH, past, H_D * DV_D),
        "state": (state_mlstm_C.reshape(bs, DEPTH, 2, rows, DV_M),
                  state_mlstm_n.reshape(bs, DEPTH, 2, rows, 1),
                  jnp.pad(state_mlstm_m.reshape(bs, DEPTH, 1, 2 * H_M), ((0, 0), (0, 0), (0, 0), (0, LANE - 2 * H_M)))),
    }

    y_p = x_prompt.reshape(bp * sp, D_MODEL)
    y_s = x_sample.reshape(bs * ss, D_MODEL)
    stacks = None
    states = []
    for l in range(DEPTH):
        y_p, stacks, state = _layer(y_p, l, mod, wts, smask, gfin, bp, sp, ctx_stacks=stacks)
        states.append(state)
        y_s, _, _ = _layer(y_s, l, mod, wts, smask, gfin, bs, ss, cache=cache, tables=tables)

    ckv, akr, kd, vd = stacks
    c_t = jnp.stack([s[0] for s in states], axis=1).reshape(bp, DEPTH, 2, H_M, DK_M, DV_M)
    n_t = jnp.stack([s[1] for s in states], axis=1).reshape(bp, DEPTH, 2, H_M, DK_M)
    m_t = jnp.stack([s[2][:, 0, :2 * H_M] for s in states], axis=1).reshape(bp, DEPTH, 2, H_M)
    return (y_p.reshape(bp, sp, D_MODEL), y_s.reshape(bs, ss, D_MODEL),
            ckv, akr[..., NOPE_A:NOPE_A + ROPE_A],
            kd.reshape(bp, DEPTH, sp, H_D, 2 * DK_D), vd.reshape(bp, DEPTH, sp, H_D, DV_D),
            c_t, n_t, m_t)
```

```python
import functools
import math

import jax
import jax.numpy as jnp
import numpy as np
from jax import lax
from jax.experimental import pallas as pl
from jax.experimental.pallas import tpu as pltpu

F32 = jnp.float32
BF16 = jnp.bfloat16

D_MODEL = 1024
DEPTH = 2
GRID_W = 64
ROPE_BASE = 10000.0
EPS = 1e-6
H_M, DK_M, DV_M = 4, 64, 128
H_A, Q_RANK, KV_RANK, NOPE_A, ROPE_A, V_A = 8, 384, 256, 64, 32, 64
H_D, DK_D, DV_D = 4, 64, 128
D_FF = 4 * D_MODEL

LANE = 128
VMEM_LIMIT = 56 * 1024 * 1024
ROW_TILE = 256
Q_TILE = 256
CHUNK = 128
KEY_CHUNKS = 4
SCORE_VMEM_BUDGET = 24 * 1024 * 1024
HEAD_GROUP = 4
MLSTM_STREAMS = 2
NEG = -1e30
LOG2E = 1.4426950408889634

C_MQ, C_MK, C_MV, C_MO, C_GI, C_GF = 0, 256, 512, 1024, 1536, 1664
M_COLS = 1792
C_A = 1792
A_COLS = 768
C_D = 2560
D_COLS = 1536
C_G = 4096
G_COLS = 3 * D_MODEL
IN_COLS_P = 7168
S_COLS = H_M * DV_M + LANE


def _dot(a, b):
    return jnp.dot(a, b, preferred_element_type=F32)


def _dot_nt(a, b):
    return lax.dot_general(a, b, (((1,), (1,)), ((), ())), preferred_element_type=F32)


def _dot_tn(a, b):
    return lax.dot_general(a, b, (((0,), (0,)), ((), ())), preferred_element_type=F32)


def _rms_rows(x, g):
    return x * lax.rsqrt(jnp.mean(x * x, axis=-1, keepdims=True) + EPS) * g


def _rope_blocks(x, cos, sa, sb, shift):
    outs = []
    for b in range(x.shape[1] // LANE):
        xb = x[:, b * LANE:(b + 1) * LANE]
        outs.append(xb * cos + pltpu.roll(xb, shift, 1) * sa + pltpu.roll(xb, LANE - shift, 1) * sb)
    return outs[0] if len(outs) == 1 else jnp.concatenate(outs, axis=1)


def _params(sem):
    return pltpu.CompilerParams(dimension_semantics=sem, vmem_limit_bytes=VMEM_LIMIT)


def _layer_spec(l, shape):
    nd = len(shape)
    return pl.BlockSpec((None,) + shape, lambda *_: (l,) + (0,) * nd, pipeline_mode=pl.Buffered(1))


def _mod_spec(l, first_row, tiles_per_cond):
    return pl.BlockSpec((None, None, 1, 6 * D_MODEL), lambda i: (l, first_row + i // tiles_per_cond, 0, 0))


_ANY = pl.BlockSpec(memory_space=pl.ANY)


def _mod_kernel(c_ref, w_ref, b_ref, o_ref):
    c = c_ref[...]
    s = (c * jax.nn.sigmoid(c)).astype(BF16)
    o_ref[...] = _dot(s, w_ref[...].astype(BF16)) + b_ref[...]


def _modulation(cond8, w_mod, b_mod):
    tn = 1536
    n6 = 6 * D_MODEL
    return pl.pallas_call(
        _mod_kernel,
        grid=(DEPTH, n6 // tn),
        in_specs=[
            pl.BlockSpec((8, D_MODEL), lambda l, j: (0, 0)),
            pl.BlockSpec((None, D_MODEL, tn), lambda l, j: (l, 0, j)),
            pl.BlockSpec((None, 1, tn), lambda l, j: (l, 0, j)),
        ],
        out_specs=pl.BlockSpec((None, 8, tn), lambda l, j: (l, 0, j)),
        out_shape=jax.ShapeDtypeStruct((DEPTH, 8, n6), F32),
        compiler_params=_params(("parallel", "parallel")),
        name="modulation",
    )(cond8, w_mod, b_mod.reshape(DEPTH, 1, n6))


def _store_slots(ref, val):
    if len(ref.shape) == 3:
        for s in range(ref.shape[0]):
            ref[s] = val
    else:
        ref[...] = val


def _inproj_kernel(latent, n_aliased, *refs):
    (x_ref, mod_ref, g1_ref, w_ref, gq_ref, gkv_ref, wq_ref, wk_ref, wv_ref) = refs[:9]
    refs = refs[9:]
    if latent:
        tab_ref = refs[0]
        (m_ref, q_ref, k_ref, va_ref, qd_ref, kd_ref, vd_ref, gate_ref) = refs[1:]
    else:
        (m_ref, q_ref, k_ref, va_ref, qd_ref, kd_ref, vd_ref, gate_ref,
         ckv_ref, akr_ref, kdraw_ref, vdraw_ref) = refs[n_aliased:]
    d = D_MODEL
    x = x_ref[...]
    mod = mod_ref[...]
    h = (_rms_rows(x, g1_ref[...]) * (1.0 + mod[:, d:2 * d]) + mod[:, 0:d]).astype(BF16)

    m_ref[...] = _dot(h, w_ref[:, 0:M_COLS])

    za = _dot(h, w_ref[:, C_A:C_A + A_COLS])
    acq = za[:, 0:Q_RANK]
    ackv = za[:, Q_RANK:Q_RANK + KV_RANK]
    akr = za[:, Q_RANK + KV_RANK:A_COLS]
    q = _dot(_rms_rows(acq, gq_ref[...]).astype(BF16), wq_ref[...])
    ckv = _rms_rows(ackv, gkv_ref[...])
    ckv_b = ckv.astype(BF16)
    kn = _dot(ckv_b, wk_ref[...])
    if not latent:
        _store_slots(ckv_ref, ckv)
        _store_slots(akr_ref, akr)
    else:
        tab = tab_ref[...]
        cq, saq, sbq = tab[:, 0:128], tab[:, 128:256], tab[:, 256:384]
        q = _rope_blocks(q, cq, saq, sbq, ROPE_A // 2)
        akr = _rope_blocks(akr, cq, saq, sbq, ROPE_A // 2)
    q_ref[...] = (q * (LOG2E * (NOPE_A + ROPE_A) ** -0.5)).astype(BF16)
    k_ref[...] = (kn + jnp.concatenate([akr] * H_A, axis=1)).astype(BF16)
    va_ref[...] = _dot(ckv_b, wv_ref[...]).astype(BF16)

    zd = _dot(h, w_ref[:, C_D:C_D + D_COLS])
    dq, dk, dv = zd[:, 0:512], zd[:, 512:1024], zd[:, 1024:1536]
    if not latent:
        _store_slots(kdraw_ref, dk)
        _store_slots(vdraw_ref, dv)
    else:
        cd, sad, sbd = tab[:, 384:512], tab[:, 512:640], tab[:, 640:768]
        dq = _rope_blocks(dq, cd, sad, sbd, DK_D // 2)
        dk = _rope_blocks(dk, cd, sad, sbd, DK_D // 2)
    qd_ref[...] = (dq * (LOG2E * DK_D ** -0.5)).astype(BF16)
    kd_ref[...] = dk.astype(BF16)
    vd_ref[...] = dv.astype(BF16)

    gate_ref[...] = _dot(h, w_ref[:, C_G:C_G + G_COLS])


def _inproj(x2, l, mod, wts, batch, n_tok, tables=None, ctx_stacks=None):
    latent = tables is not None
    t = batch * n_tok
    tm = ROW_TILE
    tpb = n_tok // tm
    row = lambda i: (i, 0)

    in_specs = [
        pl.BlockSpec((tm, D_MODEL), row),
        _mod_spec(l, 1, tpb) if latent else _mod_spec(l, 0, batch * tpb),
        _layer_spec(l, (1, D_MODEL)),
        _layer_spec(l, (D_MODEL, IN_COLS_P)),
        _layer_spec(l, (1, Q_RANK)),
        _layer_spec(l, (1, KV_RANK)),
        _layer_spec(l, (Q_RANK, H_A * LANE)),
        _layer_spec(l, (KV_RANK, H_A * LANE)),
        _layer_spec(l, (KV_RANK, H_A * V_A)),
    ]
    args = [x2, mod, wts["norm1_g"], wts["w_in"], wts["mla_q_norm_g"], wts["mla_kv_norm_g"],
            wts["wq"], wts["wk"], wts["wv"]]
    widths = [(M_COLS, F32), (H_A * LANE, BF16), (H_A * LANE, BF16), (H_A * V_A, BF16),
              (512, BF16), (512, BF16), (512, BF16), (G_COLS, F32)]
    out_specs = [pl.BlockSpec((tm, w), row) for w, _ in widths]
    out_shape = [jax.ShapeDtypeStruct((t, w), dt) for w, dt in widths]
    aliases = {}
    if latent:
        in_specs.append(pl.BlockSpec((tm, 768), lambda i: (i % tpb, 0)))
        args.append(tables)
    else:
        if ctx_stacks is None:
            stack_spec = lambda w: pl.BlockSpec((None, DEPTH, tm, w), lambda i: (i // tpb, 0, i % tpb, 0))
        else:
            stack_spec = lambda w: pl.BlockSpec((None, None, tm, w), lambda i: (i // tpb, l, i % tpb, 0))
            aliases = {len(args) + n: 8 + n for n in range(4)}
            in_specs += [_ANY] * 4
            args += list(ctx_stacks)
        for w in (KV_RANK, LANE, 512, 512):
            out_specs.append(stack_spec(w))
            out_shape.append(jax.ShapeDtypeStruct((batch, DEPTH, n_tok, w), F32))
    return pl.pallas_call(
        functools.partial(_inproj_kernel, latent, len(aliases)),
        grid=(t // tm,),
        in_specs=in_specs,
        out_specs=out_specs,
        out_shape=out_shape,
        input_output_aliases=aliases,
        compiler_params=_params(("parallel",)),
        name="inproj_lat" if latent else "inproj_ctx",
    )(*args)


def _split3(x):
    hi = x.astype(BF16)
    r1 = x - hi.astype(F32)
    mid = r1.astype(BF16)
    return hi, mid, (r1 - mid.astype(F32)).astype(BF16)


def _mlstm_chunks(items, bi, bfg, smask_ref):
    L = CHUNK
    r_i = lax.broadcasted_iota(jnp.int32, (L, L), 0)
    c_i = lax.broadcasted_iota(jnp.int32, (L, L), 1)
    lane256 = lax.broadcasted_iota(jnp.int32, (L, 256), 1)
    lane128 = lax.broadcasted_iota(jnp.int32, (1, LANE), 1)
    masks = {0: c_i <= r_i, 1: c_i >= r_i}
    tris = {d: jnp.where(m, 1.0, 0.0).astype(BF16) for d, m in masks.items()}
    heads = [(n, hd) for n in range(len(items)) for hd in range(H_M)]

    pre = []
    for z, s_prev, m_prev, d in items:
        q = (z[:, C_MQ:C_MQ + 256] * DK_M ** -0.5).astype(BF16)
        k = z[:, C_MK:C_MK + 256]
        v = z[:, C_MV:C_MV + 512].astype(BF16)
        gi = z[:, C_GI:C_GI + LANE] + bi
        xf = z[:, C_GF:C_GF + LANE] + bfg
        lf = jnp.minimum(xf, 0.0) - jnp.log(1.0 + jnp.exp(-jnp.abs(xf)))
        b = sum(_dot(tris[d], part) for part in _split3(lf))
        g = gi - b
        qs = _dot(q, s_prev.astype(BF16))
        pre.append(dict(q=q, k=k, kb=k.astype(BF16), v=v, g_t=g.T, qs=qs, b=b, g=g,
                        last=L - 1 if d == 0 else 0))

    blk = lambda x, hd: x[:, hd * LANE:(hd + 1) * LANE]
    rep = lambda x, j: jnp.broadcast_to(x[:, j:j + 1], (L, LANE))
    u, av = {}, {}
    for g0 in range(0, len(heads), HEAD_GROUP):
        group = heads[g0:g0 + HEAD_GROUP]
        gm = {}
        for n, hd in group:
            d = items[n][3]
            j = d * H_M + hd
            gm[n, hd] = jnp.where(masks[d], pre[n]["g_t"][j:j + 1, :], NEG)
            row_max = jnp.broadcast_to(jnp.max(gm[n, hd], axis=-1, keepdims=True), (L, LANE))
            u[n, hd] = jnp.maximum(row_max, items[n][2][:, j:j + 1])
        a = {}
        for n, hd in group:
            q = pre[n]["q"]
            q_h = jnp.where((lane256 >= hd * DK_M) & (lane256 < (hd + 1) * DK_M), q, jnp.zeros_like(q))
            a[n, hd] = jnp.exp(gm[n, hd] - u[n, hd]) * _dot_nt(q_h, pre[n]["kb"])
        for n, hd in group:
            v_ones = jnp.concatenate([blk(pre[n]["v"], hd), jnp.ones((L, LANE), BF16)], axis=1)
            av[n, hd] = _dot(a[n, hd].astype(BF16), v_ones)

    lane_l = lax.broadcasted_iota(jnp.int32, (L, LANE), 1)
    results = []
    for n, (z, s_prev, m_prev, d) in enumerate(items):
        p = pre[n]
        h_out, w_rep, w_old = [], [], []
        m_new = m_prev
        for hd in range(H_M):
            j = d * H_M + hd
            uh = u[n, hd]
            mp = m_prev[:, j:j + 1]
            b_h = rep(p["b"], j)
            w_int = jnp.exp(mp - uh)
            num = w_int * blk(p["qs"], hd) + av[n, hd][:, :DV_M]
            den = w_int * rep(p["qs"], H_M * DV_M + j) + av[n, hd][:, DV_M:]
            h_out.append(num / jnp.maximum(jnp.abs(den), jnp.exp(-(b_h + uh))))
            u_l = uh[p["last"]:p["last"] + 1, :]
            m_new = jnp.where(lane128 == j, b_h[p["last"]:p["last"] + 1, :] + u_l, m_new)
            w_rep.append(jnp.exp(rep(p["g"], j) - u_l))
            w_old.append(jnp.exp(mp - u_l))
        w_exp = jnp.concatenate([jnp.where(lane_l < DK_M, w_rep[0], w_rep[1]),
                                 jnp.where(lane_l < DK_M, w_rep[2], w_rep[3])], axis=1)
        k_w = (p["k"] * w_exp).astype(BF16)
        v_aug = jnp.concatenate([p["v"], jnp.ones((L, LANE), BF16)], axis=1)
        upd = _dot_tn(k_w, v_aug)
        n_scale = jnp.zeros((1, LANE), F32)
        for hd in range(H_M):
            n_scale = jnp.where(lane128 == d * H_M + hd, w_old[hd], n_scale)
        scale_row = jnp.concatenate(w_old + [n_scale], axis=1)
        results.append((jnp.concatenate(h_out, axis=1), scale_row * s_prev + smask_ref[d] * upd, m_new))
    return results


def _mlstm_kernel(has_state, *refs):
    zf_ref, zb_ref, bi_ref, bf_ref, smask_ref = refs[:5]
    refs = refs[5:]
    if has_state:
        c0_ref, n0_ref, m0_ref = refs[:3]
        refs = refs[3:]
    hf_ref, hb_ref, ct_ref, nt_ref, mt_ref, s_scr, m_scr = refs
    i = pl.program_id(1)
    hv = H_M * DV_M

    streams = s_scr.shape[0]

    @pl.when(i == 0)
    def _():
        if has_state:
            for g in range(streams):
                for d in range(2):
                    tiled = jnp.concatenate(
                        [c0_ref[g, d]] * H_M + [jnp.broadcast_to(n0_ref[g, d], (256, LANE))], axis=1)
                    s_scr[g, d] = smask_ref[d] * tiled
                    m_scr[g, d] = m0_ref[g]
        else:
            s_scr[...] = jnp.zeros_like(s_scr)
            m_scr[...] = jnp.zeros_like(m_scr)

    bi = bi_ref[...]
    bfg = bf_ref[...]
    slots = [(g, d) for g in range(streams) for d in range(2)]
    items = [((zf_ref, zb_ref)[d][g], s_scr[g, d], m_scr[g, d], d) for g, d in slots]
    for (g, d), (h, s_new, m_new) in zip(slots, _mlstm_chunks(items, bi, bfg, smask_ref)):
        (hf_ref, hb_ref)[d][g] = h
        s_scr[g, d] = s_new
        m_scr[g, d] = m_new

    @pl.when(i == pl.num_programs(1) - 1)
    def _():
        lane = lax.broadcasted_iota(jnp.int32, (1, LANE), 1)
        for g in range(streams):
            for d in range(2):
                s = s_scr[g, d]
                ct_ref[g, d] = ((s[:, 0:DV_M] + s[:, DV_M:2 * DV_M])
                                + (s[:, 2 * DV_M:3 * DV_M] + s[:, 3 * DV_M:hv]))
                nt_ref[g, d] = jnp.sum(s[:, hv:], axis=-1, keepdims=True)
            mt_ref[g] = jnp.where(lane < H_M, m_scr[g, 0], m_scr[g, 1])


def _mlstm(m_in, l, wts, smask, batch, n_tok, state=None):
    nc = n_tok // CHUNK
    hv = H_M * DV_M
    g = MLSTM_STREAMS if batch % MLSTM_STREAMS == 0 else 1
    z3 = m_in.reshape(batch, n_tok, M_COLS)
    fwd = lambda b, i: (b, i, 0)
    bwd = lambda b, i: (b, nc - 1 - i, 0)
    in_specs = [
        pl.BlockSpec((g, CHUNK, M_COLS), fwd),
        pl.BlockSpec((g, CHUNK, M_COLS), bwd),
        pl.BlockSpec((None, 1, LANE), lambda b, i: (l, 0, 0)),
        pl.BlockSpec((None, 1, LANE), lambda b, i: (l, 0, 0)),
        pl.BlockSpec((2, 256, S_COLS), lambda b, i: (0, 0, 0)),
    ]
    args = [z3, z3, wts["gate_bi"], wts["gate_bf"], smask]
    if state is not None:
        in_specs += [
            pl.BlockSpec((g, None, 2, 256, DV_M), lambda b, i: (b, l, 0, 0, 0)),
            pl.BlockSpec((g, None, 2, 256, 1), lambda b, i: (b, l, 0, 0, 0)),
            pl.BlockSpec((g, None, 1, LANE), lambda b, i: (b, l, 0, 0)),
        ]
        args += list(state)
    h_f, h_b, c_t, n_t, m_t = pl.pallas_call(
        functools.partial(_mlstm_kernel, state is not None),
        grid=(batch // g, nc),
        in_specs=in_specs,
        out_specs=[
            pl.BlockSpec((g, CHUNK, hv), fwd),
            pl.BlockSpec((g, CHUNK, hv), bwd),
            pl.BlockSpec((g, 2, 256, DV_M), lambda b, i: (b, 0, 0, 0)),
            pl.BlockSpec((g, 2, 256, 1), lambda b, i: (b, 0, 0, 0)),
            pl.BlockSpec((g, 1, LANE), lambda b, i: (b, 0, 0)),
        ],
        out_shape=[
            jax.ShapeDtypeStruct((batch, n_tok, hv), F32),
            jax.ShapeDtypeStruct((batch, n_tok, hv), F32),
            jax.ShapeDtypeStruct((batch, 2, 256, DV_M), F32),
            jax.ShapeDtypeStruct((batch, 2, 256, 1), F32),
            jax.ShapeDtypeStruct((batch, 1, LANE), F32),
        ],
        scratch_shapes=[pltpu.VMEM((g, 2, 256, S_COLS), F32), pltpu.VMEM((g, 2, 1, LANE), F32)],
        compiler_params=_params(("parallel", "arbitrary")),
        name="mlstm",
    )(*args)
    return h_f.reshape(batch * n_tok, hv), h_b.reshape(batch * n_tok, hv), c_t, n_t, m_t


def _groups_per_step(n_groups, n_maps, n_kv, tq):
    return n_groups if 2 * n_groups * n_maps * n_kv * tq * 4 <= SCORE_VMEM_BUDGET else 1


def _pipelined_attention(kernel_fn, q, kv_new, kv_cache, extra_inputs, extra_specs, batch, n_q, n_groups,
                         widths, n_maps, name):
    past = 0 if kv_cache is None else kv_cache[0].shape[0] // batch
    n_kv = past + n_q
    tq = min(Q_TILE, n_q)
    nq = n_q // tq
    gps = _groups_per_step(n_groups, n_maps, n_kv, tq)
    n_groups //= gps
    n_maps *= gps
    q_w, k_w, v_w, out_w = (w * gps for w in widths)
    n_units = batch * n_groups * nq
    nxt = lambda s: jnp.minimum(s, n_units - 1)
    cur = lambda s: jnp.maximum(s - 1, 0)
    row_blk = lambda u: (u // (n_groups * nq)) * nq + u % nq
    grp = lambda u: (u // nq) % n_groups
    bat = lambda u: u // (n_groups * nq)
    k_map = lambda s: (bat(nxt(s)), grp(nxt(s)))
    v_map = lambda s: (bat(cur(s)), grp(cur(s)))
    in_specs = [pl.BlockSpec((tq, q_w), lambda s: (row_blk(nxt(s)), grp(nxt(s)))),
                pl.BlockSpec((n_q, k_w), k_map), pl.BlockSpec((n_q, v_w), v_map)]
    inputs = [q, *kv_new]
    if past:
        in_specs += [pl.BlockSpec((past, k_w), k_map), pl.BlockSpec((past, v_w), v_map)]
        inputs += list(kv_cache)
    score = pltpu.VMEM((n_maps, n_kv, tq), F32)
    cmax = pltpu.VMEM((n_maps, 1, tq), F32)
    return pl.pallas_call(
        functools.partial(kernel_fn, gps, nq, past),
        grid=(n_units + 1,),
        in_specs=in_specs + extra_specs(lambda s: grp(cur(s)), gps),
        out_specs=pl.BlockSpec((tq, out_w), lambda s: (row_blk(cur(s)), grp(cur(s)))),
        out_shape=jax.ShapeDtypeStruct((batch * n_q, n_groups * out_w), BF16),
        scratch_shapes=[score, cmax, score, cmax, pltpu.VMEM((v_w, n_kv), BF16)],
        compiler_params=_params(("arbitrary",)),
        name=name,
    )(*inputs, *extra_inputs)


def _attention_refs(past, refs, n_extra):
    q_ref, k_ref, v_ref = refs[:3]
    refs = refs[3:]
    kc_ref = vc_ref = None
    if past:
        kc_ref, vc_ref = refs[:2]
        refs = refs[2:]
    return (q_ref, k_ref, v_ref, kc_ref, vc_ref), refs[:n_extra], refs[n_extra:]


def _pipeline_prologue(nq, past, v_ref, vc_ref, s_b, m_b, vt_scr):
    s = pl.program_id(0)

    @pl.when(s == 0)
    def _():
        s_b[...] = jnp.zeros_like(s_b)
        m_b[...] = jnp.zeros_like(m_b)

    @pl.when(lax.rem(jnp.maximum(s - 1, 0), nq) == 0)
    def _():
        if past:
            vt_scr[:, 0:past] = vc_ref[...].T
        vt_scr[:, past:] = v_ref[...].T

    return lax.rem(s, 2)


def _key_chunks(n_kv, past):
    blocks = (n_kv - past) // LANE
    n = min(KEY_CHUNKS, blocks)
    edges = ([0] if past else []) + [past + (i * blocks // n) * LANE for i in range(n + 1)]
    return list(zip(edges[:-1], edges[1:]))


def _keys_of(k_ref, kc_ref, past, lanes):
    def k_of(c0, c1):
        return kc_ref[c0:c1, lanes] if c1 <= past else k_ref[c0 - past:c1 - past, lanes]
    return k_of


def _score_maps(maps, past, s_n, m_n, s_c, m_c):
    out = []
    for a, (k_of, q, vt_of) in enumerate(maps):
        mc = m_c[a]
        mx = l = acc = None
        for c0, c1 in _key_chunks(s_n.shape[1], past):
            st = _dot_nt(k_of(c0, c1), q)
            s_n[a, c0:c1, :] = st
            cm = jnp.max(st, axis=0, keepdims=True)
            mx = cm if mx is None else jnp.maximum(mx, cm)
            p = jnp.exp2(s_c[a, c0:c1, :] - mc)
            ps = jnp.sum(p, axis=0, keepdims=True)
            l = ps if l is None else l + ps
            pv = _dot(vt_of(c0, c1), p.astype(BF16))
            acc = pv if acc is None else acc + pv
        m_n[a] = mx
        out.append((acc, l))
    return out


def _mla_kernel(gps, nq, past, *refs):
    (q_ref, k_ref, v_ref, kc_ref, vc_ref), _, (o_ref, s_a, m_a, s_b, m_b, vt_scr) = _attention_refs(past, refs, 0)
    parity = _pipeline_prologue(nq, past, v_ref, vc_ref, s_b, m_b, vt_scr)

    def body(s_n, m_n, s_c, m_c):
        maps = []
        for e in range(2 * gps):
            lanes = slice(e * LANE, (e + 1) * LANE)
            maps.append((_keys_of(k_ref, kc_ref, past, lanes), q_ref[:, lanes],
                         lambda c0, c1, e=e: vt_scr[e * V_A:(e + 1) * V_A, c0:c1]))
        outs = [acc / l for acc, l in _score_maps(maps, past, s_n, m_n, s_c, m_c)]
        o_ref[...] = jnp.concatenate(outs, axis=0).T.astype(BF16)

    @pl.when(parity == 0)
    def _():
        body(s_a, m_a, s_b, m_b)

    @pl.when(parity == 1)
    def _():
        body(s_b, m_b, s_a, m_a)


def _mla_attention(q, kv_new, kv_cache, batch, n_q):
    return _pipelined_attention(_mla_kernel, q, kv_new, kv_cache, (), lambda cur_grp, gps: [], batch, n_q,
                                H_A // 2, (2 * LANE, 2 * LANE, LANE, LANE), 2, "mla_attention")


def _diff_kernel(lam_init, gps, nq, past, *refs):
    ((q_ref, k_ref, v_ref, kc_ref, vc_ref), (lam_ref, g_ref),
     (o_ref, s_a, m_a, s_b, m_b, vt_scr)) = _attention_refs(past, refs, 2)
    parity = _pipeline_prologue(nq, past, v_ref, vc_ref, s_b, m_b, vt_scr)

    def body(s_n, m_n, s_c, m_c):
        lane = lax.broadcasted_iota(jnp.int32, (1, LANE), 1)
        lv = lam_ref[...]
        lam = (jnp.exp(jnp.sum(lv[0:1] * lv[1:2], axis=-1, keepdims=True))
               - jnp.exp(jnp.sum(lv[2:3] * lv[3:4], axis=-1, keepdims=True)) + lam_init)
        maps = []
        for h in range(gps):
            lanes = slice(h * LANE, (h + 1) * LANE)
            q = q_ref[:, lanes]
            zero = jnp.zeros_like(q)
            k_of = _keys_of(k_ref, kc_ref, past, lanes)
            vt_of = lambda c0, c1, h=h: vt_scr[h * LANE:(h + 1) * LANE, c0:c1]
            maps += [(k_of, jnp.where(lane < DK_D, q, zero), vt_of), (k_of, jnp.where(lane >= DK_D, q, zero), vt_of)]
        res = _score_maps(maps, past, s_n, m_n, s_c, m_c)
        outs = []
        for h in range(gps):
            (acc1, l1), (acc2, l2) = res[2 * h], res[2 * h + 1]
            o = (acc1 / l1 - acc2 * (lam / l2)).T
            outs.append(_rms_rows(o, g_ref[:, h * LANE:(h + 1) * LANE]) * (1.0 - lam_init))
        o_ref[...] = (outs[0] if gps == 1 else jnp.concatenate(outs, axis=1)).astype(BF16)

    @pl.when(parity == 0)
    def _():
        body(s_a, m_a, s_b, m_b)

    @pl.when(parity == 1)
    def _():
        body(s_b, m_b, s_a, m_a)


def _diff_attention(q, kv_new, kv_cache, l, wts, lam_init, batch, n_q):
    extra = lambda cur_grp, gps: [pl.BlockSpec((None, 4, DK_D), lambda s: (l, 0, 0)),
                                  pl.BlockSpec((None, 1, gps * LANE), lambda s: (l, 0, cur_grp(s)))]
    return _pipelined_attention(functools.partial(_diff_kernel, lam_init), q, kv_new, kv_cache,
                                (wts["diff_lambda"], wts["diff_norm_g"]), extra, batch, n_q, H_D,
                                (LANE, LANE, LANE, LANE), 2, "diff_attention")


def _merge_kernel(final, x_ref, mod_ref, hf_ref, hb_ref, mo_ref, oa_ref, od_ref, gate_ref,
                  gm_ref, wbm_ref, wba_ref, wbd_ref, wout_ref, g2_ref, wff1_ref, wff2_ref, gfin_ref, o_ref):
    d = D_MODEL
    mod = mod_ref[...]
    hm = hf_ref[...] + hb_ref[...]
    gm = gm_ref[...]
    o_m = jnp.concatenate(
        [_rms_rows(hm[:, h * DV_M:(h + 1) * DV_M], gm[:, h * DV_M:(h + 1) * DV_M]) for h in range(H_M)], axis=1)
    o_m = (o_m * jax.nn.sigmoid(mo_ref[...])).astype(BF16)
    gate = jax.nn.sigmoid(gate_ref[...])
    y = (gate[:, 0:d] * _dot(o_m, wbm_ref[...]) + gate[:, d:2 * d] * _dot(oa_ref[...], wba_ref[...])
         + gate[:, 2 * d:3 * d] * _dot(od_ref[...], wbd_ref[...]))
    x = x_ref[...] + mod[:, 2 * d:3 * d] * _dot(y.astype(BF16), wout_ref[...])
    h2 = (_rms_rows(x, g2_ref[...]) * (1.0 + mod[:, 4 * d:5 * d]) + mod[:, 3 * d:4 * d]).astype(BF16)
    f = jnp.maximum(_dot(h2, wff1_ref[...]), 0.0)
    x = x + mod[:, 5 * d:6 * d] * _dot((f * f).astype(BF16), wff2_ref[...])
    if final:
        x = _rms_rows(x, gfin_ref[...])
    o_ref[...] = x


def _merge(x2, l, mod, wts, batch, n_tok, latent, m_in, h_f, h_b, o_a, o_d, gate, gfin):
    t = x2.shape[0]
    tm = ROW_TILE
    tpb = n_tok // tm
    row = lambda i: (i, 0)
    in_specs = [
        pl.BlockSpec((tm, D_MODEL), row),
        _mod_spec(l, 1, tpb) if latent else _mod_spec(l, 0, batch * tpb),
        pl.BlockSpec((tm, 512), row),
        pl.BlockSpec((tm, 512), row),
        pl.BlockSpec((tm, 512), lambda i: (i, C_MO // 512)),
        pl.BlockSpec((tm, 512), row),
        pl.BlockSpec((tm, 512), row),
        pl.BlockSpec((tm, G_COLS), row),
        _layer_spec(l, (1, 512)),
        _layer_spec(l, (512, D_MODEL)),
        _layer_spec(l, (512, D_MODEL)),
        _layer_spec(l, (512, D_MODEL)),
        _layer_spec(l, (D_MODEL, D_MODEL)),
        _layer_spec(l, (1, D_MODEL)),
        _layer_spec(l, (D_MODEL, D_FF)),
        _layer_spec(l, (D_FF, D_MODEL)),
        pl.BlockSpec((1, D_MODEL), lambda i: (0, 0)),
    ]
    return pl.pallas_call(
        functools.partial(_merge_kernel, l == DEPTH - 1),
        grid=(t // tm,),
        in_specs=in_specs,
        out_specs=pl.BlockSpec((tm, D_MODEL), row),
        out_shape=jax.ShapeDtypeStruct((t, D_MODEL), F32),
        compiler_params=_params(("parallel",)),
        name="merge_mlp",
    )(x2, mod, h_f, h_b, m_in, o_a, o_d, gate, wts["mlstm_norm_g"], wts["w_br_mlstm"], wts["w_br_mla"],
      wts["w_br_diff"], wts["w_out"], wts["norm2_g"], wts["w_ff1"], wts["w_ff2"], gfin)


def _prep_weights(w_in, mlstm_gate_b, norm1_g, mlstm_norm_g, mla_q_norm_g, mla_w_q_up, mla_kv_norm_g,
                  mla_w_kv_up, diff_lambda, diff_norm_g, w_br_mlstm, w_br_mla, w_br_diff, w_out,
                  norm2_g, w_ff1, w_ff2):
    depth = w_in.shape[0]
    wb = w_in.astype(BF16)
    c_mg = 2 * H_M * DK_M + 2 * H_M * DV_M
    c_acq = c_mg + 4 * H_M
    c_akr = c_acq + Q_RANK + KV_RANK
    c_dq = c_akr + ROPE_A
    moves = [(0, c_mg + 8, 0), (c_mg + 8, c_acq, C_GF), (c_acq, c_akr, C_A),
             (c_akr, c_dq, C_A + Q_RANK + KV_RANK + NOPE_A), (c_dq, w_in.shape[2], C_D)]
    w_p = jnp.zeros((depth, D_MODEL, IN_COLS_P), BF16)
    for s0, s1, d0 in moves:
        w_p = lax.dynamic_update_slice(w_p, wb[..., s0:s1], (0, 0, d0))
    pad8 = lambda a: jnp.pad(a, ((0, 0), (0, LANE - 8)))[:, None, :]
    wq = mla_w_q_up.astype(BF16).reshape(depth, Q_RANK, H_A, NOPE_A + ROPE_A)
    wq = jnp.pad(wq, ((0, 0), (0, 0), (0, 0), (0, LANE - NOPE_A - ROPE_A))).reshape(depth, Q_RANK, H_A * LANE)
    wkv = mla_w_kv_up.astype(BF16).reshape(depth, KV_RANK, H_A, NOPE_A + V_A)
    wk = jnp.pad(wkv[..., :NOPE_A], ((0, 0), (0, 0), (0, 0), (0, LANE - NOPE_A))).reshape(depth, KV_RANK, H_A * LANE)
    wv = wkv[..., NOPE_A:].reshape(depth, KV_RANK, H_A * V_A)
    row = lambda a: a[:, None, :]
    return {
        "w_in": w_p,
        "gate_bi": pad8(mlstm_gate_b[:, :8]), "gate_bf": pad8(mlstm_gate_b[:, 8:]),
        "norm1_g": row(norm1_g), "norm2_g": row(norm2_g), "mlstm_norm_g": row(mlstm_norm_g),
        "mla_q_norm_g": row(mla_q_norm_g), "mla_kv_norm_g": row(mla_kv_norm_g),
        "wq": wq, "wk": wk, "wv": wv,
        "diff_lambda": diff_lambda, "diff_norm_g": row(diff_norm_g),
        "w_br_mlstm": w_br_mlstm.astype(BF16), "w_br_mla": w_br_mla.astype(BF16),
        "w_br_diff": w_br_diff.astype(BF16), "w_out": w_out.astype(BF16),
        "w_ff1": w_ff1.astype(BF16), "w_ff2": w_ff2.astype(BF16),
    }


def _rope_tables(n_tokens):
    rows = n_tokens // GRID_W
    row = jnp.repeat(jnp.arange(rows, dtype=F32), GRID_W)
    col = jnp.tile(jnp.arange(GRID_W, dtype=F32), rows)

    def cs(dim):
        quarter = dim // 4
        inv = ROPE_BASE ** (-jnp.arange(quarter, dtype=F32) / quarter)
        ang = jnp.concatenate([row[:, None] * inv, col[:, None] * inv], axis=-1)
        return jnp.cos(ang), jnp.sin(ang)

    ca, sa = cs(ROPE_A)
    cd, sd = cs(DK_D)
    src = jnp.concatenate([ca, sa, cd, sd, jnp.ones((n_tokens, 1), F32)], axis=1)
    i_ca, i_sa, i_cd, i_sd, i_one = 0, 16, 32, 64, 96
    place = np.zeros((97, 6 * LANE), np.float32)

    def put(block, lane0, src0, n, sign=1.0):
        for t in range(n):
            place[src0 + t, block * LANE + lane0 + t] = sign

    for lane in list(range(64)) + list(range(96, 128)):
        place[i_one, lane] = 1.0
    put(0, 64, i_ca, 16), put(0, 80, i_ca, 16)
    put(1, 80, i_sa, 16), put(2, 64, i_sa, 16, -1.0)
    for off in (0, 32, 64, 96):
        put(3, off, i_cd, 32)
    put(4, 32, i_sd, 32), put(4, 96, i_sd, 32)
    put(5, 0, i_sd, 32, -1.0), put(5, 64, i_sd, 32, -1.0)
    return jnp.dot(src, jnp.asarray(place), precision=lax.Precision.HIGHEST)


def _state_mask():
    r = jnp.arange(256)[:, None] // DK_M
    c = jnp.arange(S_COLS)[None, :]
    diag = (c < H_M * DV_M) & (c // DV_M == r)
    return jnp.stack([(diag | (c == H_M * DV_M + d * H_M + r)) for d in range(2)]).astype(F32)


def _layer(x2, l, mod, wts, smask, gfin, batch, n_tok, cache=None, tables=None, ctx_stacks=None):
    latent = cache is not None
    lam_init = 0.8 - 0.6 * math.exp(-0.3 * l)
    outs = _inproj(x2, l, mod, wts, batch, n_tok, tables, ctx_stacks)
    m_in, q_a, k_a, v_a, q_d, k_d, v_d, gate = outs[:8]
    kv_a = kv_d = None
    if latent:
        ck_a, cv_a, ck_d, cv_d = _cache_kv(l, wts, cache, batch)
        kv_a, kv_d = (ck_a, cv_a), (ck_d, cv_d)
    h_f, h_b, c_t, n_t, m_t = _mlstm(m_in, l, wts, smask, batch, n_tok, cache["state"] if latent else None)
    o_a = _mla_attention(q_a, (k_a, v_a), kv_a, batch, n_tok)
    o_d = _diff_attention(q_d, (k_d, v_d), kv_d, l, wts, lam_init, batch, n_tok)
    x_new = _merge(x2, l, mod, wts, batch, n_tok, latent, m_in, h_f, h_b, o_a, o_d, gate, gfin)
    return x_new, tuple(outs[8:]), (c_t, n_t, m_t)


def _cache_kv_kernel(ckv_ref, kr_ref, kd_ref, vd_ref, wk_ref, wv_ref, ka_out, va_out, kd_out, vd_out):
    ckv = ckv_ref[...].astype(BF16)
    ka_out[...] = (_dot(ckv, wk_ref[...]) + jnp.concatenate([kr_ref[...]] * H_A, axis=1)).astype(BF16)
    va_out[...] = _dot(ckv, wv_ref[...]).astype(BF16)
    kd_out[...] = kd_ref[...].astype(BF16)
    vd_out[...] = vd_ref[...].astype(BF16)


def _cache_kv(l, wts, cache, batch):
    past = cache["past"]
    tm = ROW_TILE
    ppb = past // tm
    cached = lambda i: (i // ppb, l, i % ppb, 0)
    widths = (H_A * LANE, H_A * V_A, 512, 512)
    return pl.pallas_call(
        _cache_kv_kernel,
        grid=(batch * ppb,),
        in_specs=[pl.BlockSpec((None, None, tm, KV_RANK), cached),
                  pl.BlockSpec((None, None, tm, LANE), cached),
                  pl.BlockSpec((None, None, tm, 512), cached),
                  pl.BlockSpec((None, None, tm, 512), cached),
                  _layer_spec(l, (KV_RANK, H_A * LANE)), _layer_spec(l, (KV_RANK, H_A * V_A))],
        out_specs=[pl.BlockSpec((tm, w), lambda i: (i, 0)) for w in widths],
        out_shape=[jax.ShapeDtypeStruct((batch * past, w), BF16) for w in widths],
        compiler_params=_params(("parallel",)),
        name="cache_kv",
    )(cache["ckv"], cache["krope"], cache["diff_k"], cache["diff_v"], wts["wk"], wts["wv"])


def kernel(x_prompt, x_sample, c, cache_mla_ckv, cache_mla_krope, cache_diff_k, cache_diff_v, state_mlstm_C, state_mlstm_n, state_mlstm_m, c_ctx, w_mod, b_mod, norm1_g, w_in, mlstm_gate_b, mlstm_norm_g, mla_q_norm_g, mla_w_q_up, mla_kv_norm_g, mla_w_kv_up, diff_lambda, diff_norm_g, w_br_mlstm, w_br_mla, w_br_diff, w_out, norm2_g, w_ff1, w_ff2, final_norm_g):
    bp, sp, _ = x_prompt.shape
    bs, ss, _ = x_sample.shape
    past = cache_mla_ckv.shape[2]
    assert bs + 1 <= 8 and ss % GRID_W == 0
    assert sp % ROW_TILE == 0 and ss % ROW_TILE == 0 and past % ROW_TILE == 0

    cond8 = jnp.concatenate([c_ctx[None, :], c, jnp.zeros((8 - 1 - bs, D_MODEL), F32)], axis=0)
    mod = _modulation(cond8, w_mod, b_mod).reshape(DEPTH, 8, 1, 6 * D_MODEL)
    wts = _prep_weights(w_in, mlstm_gate_b, norm1_g, mlstm_norm_g, mla_q_norm_g, mla_w_q_up, mla_kv_norm_g,
                        mla_w_kv_up, diff_lambda, diff_norm_g, w_br_mlstm, w_br_mla, w_br_diff, w_out,
                        norm2_g, w_ff1, w_ff2)
    tables = _rope_tables(ss)
    smask = _state_mask()
    gfin = final_norm_g[None, :]
    rows = H_M * DK_M
    cache = {
        "past": past,
        "ckv": cache_mla_ckv,
        "krope": jnp.pad(cache_mla_krope, ((0, 0), (0, 0), (0, 0), (NOPE_A, LANE - NOPE_A - ROPE_A))),
        "diff_k": cache_diff_k.reshape(bs, DEPTH, past, H_D * 2 * DK_D),
        "diff_v": cache_diff_v.reshape(bs, DEPTH, past, H_D * DV_D),
        "state": (state_mlstm_C.reshape(bs, DEPTH, 2, rows, DV_M),
                  state_mlstm_n.reshape(bs, DEPTH, 2, rows, 1),
                  jnp.pad(state_mlstm_m.reshape(bs, DEPTH, 1, 2 * H_M), ((0, 0), (0, 0), (0, 0), (0, LANE - 2 * H_M)))),
    }

    y_p = x_prompt.reshape(bp * sp, D_MODEL)
    y_s = x_sample.reshape(bs * ss, D_MODEL)
    stacks = None
    states = []
    for l in range(DEPTH):
        y_p, stacks, state = _layer(y_p, l, mod, wts, smask, gfin, bp, sp, ctx_stacks=stacks)
        states.append(state)
        y_s, _, _ = _layer(y_s, l, mod, wts, smask, gfin, bs, ss, cache=cache, tables=tables)

    ckv, akr, kd, vd = stacks
    c_t = jnp.stack([s[0] for s in states], axis=1).reshape(bp, DEPTH, 2, H_M, DK_M, DV_M)
    n_t = jnp.stack([s[1] for s in states], axis=1).reshape(bp, DEPTH, 2, H_M, DK_M)
    m_t = jnp.stack([s[2][:, 0, :2 * H_M] for s in states], axis=1).reshape(bp, DEPTH, 2, H_M)
    return (y_p.reshape(bp, sp, D_MODEL), y_s.reshape(bs, ss, D_MODEL),
            ckv, akr[..., NOPE_A:NOPE_A + ROPE_A],
            kd.reshape(bp, DEPTH, sp, H_D, 2 * DK_D), vd.reshape(bp, DEPTH, sp, H_D, DV_D),
            c_t, n_t, m_t)
```

```python
import functools
import math

import jax
import jax.numpy as jnp
import numpy as np
from jax import lax
from jax.experimental import pallas as pl
from jax.experimental.pallas import tpu as pltpu

F32 = jnp.float32
BF16 = jnp.bfloat16

D_MODEL = 1024
DEPTH = 2
GRID_W = 64
ROPE_BASE = 10000.0
EPS = 1e-6
H_M, DK_M, DV_M = 4, 64, 128
H_A, Q_RANK, KV_RANK, NOPE_A, ROPE_A, V_A = 8, 384, 256, 64, 32, 64
H_D, DK_D, DV_D = 4, 64, 128
D_FF = 4 * D_MODEL

LANE = 128
VMEM_LIMIT = 56 * 1024 * 1024
ROW_TILE = 256
Q_TILE = 256
CHUNK = 128
KEY_CHUNKS = 4
SCORE_VMEM_BUDGET = 24 * 1024 * 1024
HEAD_GROUP = 4
MLSTM_STREAMS = 2
NEG = -1e30
LOG2E = 1.4426950408889634

C_MQ, C_MK, C_MV, C_MO, C_GI, C_GF = 0, 256, 512, 1024, 1536, 1664
M_COLS = 1792
C_A = 1792
A_COLS = 768
C_D = 2560
D_COLS = 1536
C_G = 4096
G_COLS = 3 * D_MODEL
IN_COLS_P = 7168
S_COLS = H_M * DV_M + LANE


def _dot(a, b):
    return jnp.dot(a, b, preferred_element_type=F32)


def _dot_nt(a, b):
    return lax.dot_general(a, b, (((1,), (1,)), ((), ())), preferred_element_type=F32)


def _dot_tn(a, b):
    return lax.dot_general(a, b, (((0,), (0,)), ((), ())), preferred_element_type=F32)


def _rms_rows(x, g):
    return x * lax.rsqrt(jnp.mean(x * x, axis=-1, keepdims=True) + EPS) * g


def _rope_blocks(x, cos, sa, sb, shift):
    outs = []
    for b in range(x.shape[1] // LANE):
        xb = x[:, b * LANE:(b + 1) * LANE]
        outs.append(xb * cos + pltpu.roll(xb, shift, 1) * sa + pltpu.roll(xb, LANE - shift, 1) * sb)
    return outs[0] if len(outs) == 1 else jnp.concatenate(outs, axis=1)


def _params(sem):
    return pltpu.CompilerParams(dimension_semantics=sem, vmem_limit_bytes=VMEM_LIMIT)


def _layer_spec(l, shape):
    nd = len(shape)
    return pl.BlockSpec((None,) + shape, lambda *_: (l,) + (0,) * nd, pipeline_mode=pl.Buffered(1))


def _mod_spec(l, first_row, tiles_per_cond):
    return pl.BlockSpec((None, None, 1, 6 * D_MODEL), lambda i: (l, first_row + i // tiles_per_cond, 0, 0))


_ANY = pl.BlockSpec(memory_space=pl.ANY)


def _mod_kernel(c_ref, w_ref, b_ref, o_ref):
    c = c_ref[...]
    s = (c * jax.nn.sigmoid(c)).astype(BF16)
    o_ref[...] = _dot(s, w_ref[...].astype(BF16)) + b_ref[...]


def _modulation(cond8, w_mod, b_mod):
    tn = 1536
    n6 = 6 * D_MODEL
    return pl.pallas_call(
        _mod_kernel,
        grid=(DEPTH, n6 // tn),
        in_specs=[
            pl.BlockSpec((8, D_MODEL), lambda l, j: (0, 0)),
            pl.BlockSpec((None, D_MODEL, tn), lambda l, j: (l, 0, j)),
            pl.BlockSpec((None, 1, tn), lambda l, j: (l, 0, j)),
        ],
        out_specs=pl.BlockSpec((None, 8, tn), lambda l, j: (l, 0, j)),
        out_shape=jax.ShapeDtypeStruct((DEPTH, 8, n6), F32),
        compiler_params=_params(("parallel", "parallel")),
        name="modulation",
    )(cond8, w_mod, b_mod.reshape(DEPTH, 1, n6))


def _store_slots(ref, val):
    if len(ref.shape) == 3:
        for s in range(ref.shape[0]):
            ref[s] = val
    else:
        ref[...] = val


def _inproj_kernel(latent, n_aliased, *refs):
    (x_ref, mod_ref, g1_ref, w_ref, gq_ref, gkv_ref, wq_ref, wk_ref, wv_ref) = refs[:9]
    refs = refs[9:]
    if latent:
        tab_ref = refs[0]
        (m_ref, q_ref, k_ref, va_ref, qd_ref, kd_ref, vd_ref, gate_ref) = refs[1:]
    else:
        (m_ref, q_ref, k_ref, va_ref, qd_ref, kd_ref, vd_ref, gate_ref,
         ckv_ref, akr_ref, kdraw_ref, vdraw_ref) = refs[n_aliased:]
    d = D_MODEL
    x = x_ref[...]
    mod = mod_ref[...]
    h = (_rms_rows(x, g1_ref[...]) * (1.0 + mod[:, d:2 * d]) + mod[:, 0:d]).astype(BF16)

    m_ref[...] = _dot(h, w_ref[:, 0:M_COLS])

    za = _dot(h, w_ref[:, C_A:C_A + A_COLS])
    acq = za[:, 0:Q_RANK]
    ackv = za[:, Q_RANK:Q_RANK + KV_RANK]
    akr = za[:, Q_RANK + KV_RANK:A_COLS]
    q = _dot(_rms_rows(acq, gq_ref[...]).astype(BF16), wq_ref[...])
    ckv = _rms_rows(ackv, gkv_ref[...])
    ckv_b = ckv.astype(BF16)
    kn = _dot(ckv_b, wk_ref[...])
    if not latent:
        _store_slots(ckv_ref, ckv)
        _store_slots(akr_ref, akr)
    else:
        tab = tab_ref[...]
        cq, saq, sbq = tab[:, 0:128], tab[:, 128:256], tab[:, 256:384]
        q = _rope_blocks(q, cq, saq, sbq, ROPE_A // 2)
        akr = _rope_blocks(akr, cq, saq, sbq, ROPE_A // 2)
    q_ref[...] = (q * (LOG2E * (NOPE_A + ROPE_A) ** -0.5)).astype(BF16)
    k_ref[...] = (kn + jnp.concatenate([akr] * H_A, axis=1)).astype(BF16)
    va_ref[...] = _dot(ckv_b, wv_ref[...]).astype(BF16)

    zd = _dot(h, w_ref[:, C_D:C_D + D_COLS])
    dq, dk, dv = zd[:, 0:512], zd[:, 512:1024], zd[:, 1024:1536]
    if not latent:
        _store_slots(kdraw_ref, dk)
        _store_slots(vdraw_ref, dv)
    else:
        cd, sad, sbd = tab[:, 384:512], tab[:, 512:640], tab[:, 640:768]
        dq = _rope_blocks(dq, cd, sad, sbd, DK_D // 2)
        dk = _rope_blocks(dk, cd, sad, sbd, DK_D // 2)
    qd_ref[...] = (dq * (LOG2E * DK_D ** -0.5)).astype(BF16)
    kd_ref[...] = dk.astype(BF16)
    vd_ref[...] = dv.astype(BF16)

    gate_ref[...] = _dot(h, w_ref[:, C_G:C_G + G_COLS])


def _inproj(x2, l, mod, wts, batch, n_tok, tables=None, ctx_stacks=None):
    latent = tables is not None
    t = batch * n_tok
    tm = ROW_TILE
    tpb = n_tok // tm
    row = lambda i: (i, 0)

    in_specs = [
        pl.BlockSpec((tm, D_MODEL), row),
        _mod_spec(l, 1, tpb) if latent else _mod_spec(l, 0, batch * tpb),
        _layer_spec(l, (1, D_MODEL)),
        _layer_spec(l, (D_MODEL, IN_COLS_P)),
        _layer_spec(l, (1, Q_RANK)),
        _layer_spec(l, (1, KV_RANK)),
        _layer_spec(l, (Q_RANK, H_A * LANE)),
        _layer_spec(l, (KV_RANK, H_A * LANE)),
        _layer_spec(l, (KV_RANK, H_A * V_A)),
    ]
    args = [x2, mod, wts["norm1_g"], wts["w_in"], wts["mla_q_norm_g"], wts["mla_kv_norm_g"],
            wts["wq"], wts["wk"], wts["wv"]]
    widths = [(M_COLS, F32), (H_A * LANE, BF16), (H_A * LANE, BF16), (H_A * V_A, BF16),
              (512, BF16), (512, BF16), (512, BF16), (G_COLS, F32)]
    out_specs = [pl.BlockSpec((tm, w), row) for w, _ in widths]
    out_shape = [jax.ShapeDtypeStruct((t, w), dt) for w, dt in widths]
    aliases = {}
    if latent:
        in_specs.append(pl.BlockSpec((tm, 6 * LANE), lambda i: (i % tpb, 0)))
        args.append(tables)
    else:
        if ctx_stacks is None:
            stack_spec = lambda w: pl.BlockSpec((None, DEPTH, tm, w), lambda i: (i // tpb, 0, i % tpb, 0))
        else:
            stack_spec = lambda w: pl.BlockSpec((None, None, tm, w), lambda i: (i // tpb, l, i % tpb, 0))
            aliases = {len(args) + n: 8 + n for n in range(4)}
            in_specs += [_ANY] * 4
            args += list(ctx_stacks)
        for w in (KV_RANK, LANE, 512, 512):
            out_specs.append(stack_spec(w))
            out_shape.append(jax.ShapeDtypeStruct((batch, DEPTH, n_tok, w), F32))
    return pl.pallas_call(
        functools.partial(_inproj_kernel, latent, len(aliases)),
        grid=(t // tm,),
        in_specs=in_specs,
        out_specs=out_specs,
        out_shape=out_shape,
        input_output_aliases=aliases,
        compiler_params=_params(("parallel",)),
        name="inproj_lat" if latent else "inproj_ctx",
    )(*args)


def _split3(x):
    hi = x.astype(BF16)
    r1 = x - hi.astype(F32)
    mid = r1.astype(BF16)
    return hi, mid, (r1 - mid.astype(F32)).astype(BF16)


def _mlstm_chunks(items, bi, bfg, smask_ref):
    L = CHUNK
    r_i = lax.broadcasted_iota(jnp.int32, (L, L), 0)
    c_i = lax.broadcasted_iota(jnp.int32, (L, L), 1)
    lane256 = lax.broadcasted_iota(jnp.int32, (L, 256), 1)
    lane128 = lax.broadcasted_iota(jnp.int32, (1, LANE), 1)
    masks = {0: c_i <= r_i, 1: c_i >= r_i}
    tris = {d: jnp.where(m, 1.0, 0.0).astype(BF16) for d, m in masks.items()}
    heads = [(n, hd) for n in range(len(items)) for hd in range(H_M)]

    pre = []
    for z_of, s_of, _, m_prev, d, _ in items:
        q = (z_of(C_MQ, 256) * DK_M ** -0.5).astype(BF16)
        gi = z_of(C_GI, LANE) + bi
        xf = z_of(C_GF, LANE) + bfg
        lf = jnp.minimum(xf, 0.0) - jnp.log(1.0 + jnp.exp(-jnp.abs(xf)))
        b = sum(_dot(tris[d], part) for part in _split3(lf))
        g = gi - b
        pre.append(dict(q=q, kb=z_of(C_MK, 256).astype(BF16), g_t=g.T, b=b, g=g, last=L - 1 if d == 0 else 0))

    blk = lambda x, hd: x[:, hd * LANE:(hd + 1) * LANE]
    rep = lambda x, j: jnp.broadcast_to(x[:, j:j + 1], (L, LANE))
    u, av = {}, {}
    for g0 in range(0, len(heads), HEAD_GROUP):
        group = heads[g0:g0 + HEAD_GROUP]
        gm = {}
        for n, hd in group:
            d = items[n][4]
            j = d * H_M + hd
            gm[n, hd] = jnp.where(masks[d], pre[n]["g_t"][j:j + 1, :], NEG)
            row_max = jnp.broadcast_to(jnp.max(gm[n, hd], axis=-1, keepdims=True), (L, LANE))
            u[n, hd] = jnp.maximum(row_max, items[n][3][:, j:j + 1])
        a = {}
        for n, hd in group:
            q = pre[n]["q"]
            q_h = jnp.where((lane256 >= hd * DK_M) & (lane256 < (hd + 1) * DK_M), q, jnp.zeros_like(q))
            a[n, hd] = jnp.exp(gm[n, hd] - u[n, hd]) * _dot_nt(q_h, pre[n]["kb"])
        for n, hd in group:
            v_h = items[n][0](C_MV + hd * DV_M, DV_M).astype(BF16)
            av[n, hd] = _dot(a[n, hd].astype(BF16), jnp.concatenate([v_h, jnp.ones((L, LANE), BF16)], axis=1))

    lane_l = lax.broadcasted_iota(jnp.int32, (L, LANE), 1)
    results = []
    for n, (z_of, s_of, s_set, m_prev, d, h_set) in enumerate(items):
        p = pre[n]
        s_prev = s_of()
        p["qs"] = _dot(p["q"], s_prev.astype(BF16))
        w_rep, w_old = [], []
        m_new = m_prev
        for hd in range(H_M):
            j = d * H_M + hd
            uh = u[n, hd]
            mp = m_prev[:, j:j + 1]
            b_h = rep(p["b"], j)
            w_int = jnp.exp(mp - uh)
            num = w_int * blk(p["qs"], hd) + av[n, hd][:, :DV_M]
            den = w_int * rep(p["qs"], H_M * DV_M + j) + av[n, hd][:, DV_M:]
            h_set(hd, num / jnp.maximum(jnp.abs(den), jnp.exp(-(b_h + uh))))
            u_l = uh[p["last"]:p["last"] + 1, :]
            m_new = jnp.where(lane128 == j, b_h[p["last"]:p["last"] + 1, :] + u_l, m_new)
            w_rep.append(jnp.exp(rep(p["g"], j) - u_l))
            w_old.append(jnp.exp(mp - u_l))
        w_exp = jnp.concatenate([jnp.where(lane_l < DK_M, w_rep[0], w_rep[1]),
                                 jnp.where(lane_l < DK_M, w_rep[2], w_rep[3])], axis=1)
        k_w = (z_of(C_MK, 256) * w_exp).astype(BF16)
        v_aug = jnp.concatenate([z_of(C_MV, H_M * DV_M).astype(BF16), jnp.ones((L, LANE), BF16)], axis=1)
        upd = _dot_tn(k_w, v_aug)
        n_scale = jnp.zeros((1, LANE), F32)
        for hd in range(H_M):
            n_scale = jnp.where(lane128 == d * H_M + hd, w_old[hd], n_scale)
        scale_row = jnp.concatenate(w_old + [n_scale], axis=1)
        s_set(scale_row * s_prev + smask_ref[d] * upd)
        results.append(m_new)
    return results


def _mlstm_kernel(has_state, *refs):
    zf_ref, zb_ref, bi_ref, bf_ref, smask_ref = refs[:5]
    refs = refs[5:]
    if has_state:
        c0_ref, n0_ref, m0_ref = refs[:3]
        refs = refs[3:]
    hf_ref, hb_ref, ct_ref, nt_ref, mt_ref, s_scr, m_scr = refs
    i = pl.program_id(1)
    hv = H_M * DV_M

    streams = s_scr.shape[0]

    @pl.when(i == 0)
    def _():
        if has_state:
            for g in range(streams):
                for d in range(2):
                    tiled = jnp.concatenate(
                        [c0_ref[g, d]] * H_M + [jnp.broadcast_to(n0_ref[g, d], (256, LANE))], axis=1)
                    s_scr[g, d] = smask_ref[d] * tiled
                    m_scr[g, d] = m0_ref[g]
        else:
            s_scr[...] = jnp.zeros_like(s_scr)
            m_scr[...] = jnp.zeros_like(m_scr)

    bi = bi_ref[...]
    bfg = bf_ref[...]
    slots = [(g, d) for g in range(streams) for d in range(2)]

    def item(g, d):
        z_ref, h_ref = (zf_ref, hf_ref) if d == 0 else (zb_ref, hb_ref)

        def h_set(hd, val):
            h_ref[g, :, hd * DV_M:(hd + 1) * DV_M] = val

        def s_set(val):
            s_scr[g, d] = val

        return (lambda c, w: z_ref[g, :, c:c + w], lambda: s_scr[g, d], s_set, m_scr[g, d], d, h_set)

    for (g, d), m_new in zip(slots, _mlstm_chunks([item(g, d) for g, d in slots], bi, bfg, smask_ref)):
        m_scr[g, d] = m_new

    @pl.when(i == pl.num_programs(1) - 1)
    def _():
        lane = lax.broadcasted_iota(jnp.int32, (1, LANE), 1)
        for g in range(streams):
            for d in range(2):
                s = s_scr[g, d]
                ct_ref[g, d] = ((s[:, 0:DV_M] + s[:, DV_M:2 * DV_M])
                                + (s[:, 2 * DV_M:3 * DV_M] + s[:, 3 * DV_M:hv]))
                nt_ref[g, d] = jnp.sum(s[:, hv:], axis=-1, keepdims=True)
            mt_ref[g] = jnp.where(lane < H_M, m_scr[g, 0], m_scr[g, 1])


def _mlstm(m_in, l, wts, smask, batch, n_tok, state=None):
    nc = n_tok // CHUNK
    hv = H_M * DV_M
    g = math.gcd(batch, MLSTM_STREAMS)
    z3 = m_in.reshape(batch, n_tok, M_COLS)
    fwd = lambda b, i: (b, i, 0)
    bwd = lambda b, i: (b, nc - 1 - i, 0)
    in_specs = [
        pl.BlockSpec((g, CHUNK, M_COLS), fwd),
        pl.BlockSpec((g, CHUNK, M_COLS), bwd),
        pl.BlockSpec((None, 1, LANE), lambda b, i: (l, 0, 0)),
        pl.BlockSpec((None, 1, LANE), lambda b, i: (l, 0, 0)),
        pl.BlockSpec((2, 256, S_COLS), lambda b, i: (0, 0, 0)),
    ]
    args = [z3, z3, wts["gate_bi"], wts["gate_bf"], smask]
    if state is not None:
        in_specs += [
            pl.BlockSpec((g, None, 2, 256, DV_M), lambda b, i: (b, l, 0, 0, 0)),
            pl.BlockSpec((g, None, 2, 256, 1), lambda b, i: (b, l, 0, 0, 0)),
            pl.BlockSpec((g, None, 1, LANE), lambda b, i: (b, l, 0, 0)),
        ]
        args += list(state)
    h_f, h_b, c_t, n_t, m_t = pl.pallas_call(
        functools.partial(_mlstm_kernel, state is not None),
        grid=(batch // g, nc),
        in_specs=in_specs,
        out_specs=[
            pl.BlockSpec((g, CHUNK, hv), fwd),
            pl.BlockSpec((g, CHUNK, hv), bwd),
            pl.BlockSpec((g, 2, 256, DV_M), lambda b, i: (b, 0, 0, 0)),
            pl.BlockSpec((g, 2, 256, 1), lambda b, i: (b, 0, 0, 0)),
            pl.BlockSpec((g, 1, LANE), lambda b, i: (b, 0, 0)),
        ],
        out_shape=[
            jax.ShapeDtypeStruct((batch, n_tok, hv), F32),
            jax.ShapeDtypeStruct((batch, n_tok, hv), F32),
            jax.ShapeDtypeStruct((batch, 2, 256, DV_M), F32),
            jax.ShapeDtypeStruct((batch, 2, 256, 1), F32),
            jax.ShapeDtypeStruct((batch, 1, LANE), F32),
        ],
        scratch_shapes=[pltpu.VMEM((g, 2, 256, S_COLS), F32), pltpu.VMEM((g, 2, 1, LANE), F32)],
        compiler_params=_params(("parallel", "arbitrary")),
        name="mlstm",
    )(*args)
    return h_f.reshape(batch * n_tok, hv), h_b.reshape(batch * n_tok, hv), c_t, n_t, m_t


def _groups_per_step(n_groups, n_maps, n_kv, tq):
    return n_groups if 2 * n_groups * n_maps * n_kv * tq * 4 <= SCORE_VMEM_BUDGET else 1


def _pipelined_attention(kernel_fn, q, kv_new, kv_cache, extra_inputs, extra_specs, batch, n_q, n_groups,
                         widths, n_maps, name):
    past = 0 if kv_cache is None else kv_cache[0].shape[0] // batch
    n_kv = past + n_q
    tq = min(Q_TILE, n_q)
    nq = n_q // tq
    gps = _groups_per_step(n_groups, n_maps, n_kv, tq)
    n_groups //= gps
    n_maps *= gps
    q_w, k_w, v_w, out_w = (w * gps for w in widths)
    n_units = batch * n_groups * nq
    nxt = lambda s: jnp.minimum(s, n_units - 1)
    cur = lambda s: jnp.maximum(s - 1, 0)
    row_blk = lambda u: (u // (n_groups * nq)) * nq + u % nq
    grp = lambda u: (u // nq) % n_groups
    bat = lambda u: u // (n_groups * nq)
    k_map = lambda s: (bat(nxt(s)), grp(nxt(s)))
    v_map = lambda s: (bat(cur(s)), grp(cur(s)))
    in_specs = [pl.BlockSpec((tq, q_w), lambda s: (row_blk(nxt(s)), grp(nxt(s)))),
                pl.BlockSpec((n_q, k_w), k_map), pl.BlockSpec((n_q, v_w), v_map)]
    inputs = [q, *kv_new]
    if past:
        in_specs += [pl.BlockSpec((past, k_w), k_map), pl.BlockSpec((past, v_w), v_map)]
        inputs += list(kv_cache)
    score = pltpu.VMEM((n_maps, n_kv, tq), F32)
    cmax = pltpu.VMEM((n_maps, 1, tq), F32)
    return pl.pallas_call(
        functools.partial(kernel_fn, gps, nq, past),
        grid=(n_units + 1,),
        in_specs=in_specs + extra_specs(lambda s: grp(cur(s)), gps),
        out_specs=pl.BlockSpec((tq, out_w), lambda s: (row_blk(cur(s)), grp(cur(s)))),
        out_shape=jax.ShapeDtypeStruct((batch * n_q, n_groups * out_w), BF16),
        scratch_shapes=[score, cmax, score, cmax, pltpu.VMEM((v_w, n_kv), BF16)],
        compiler_params=_params(("arbitrary",)),
        name=name,
    )(*inputs, *extra_inputs)


def _attention_refs(past, refs, n_extra):
    q_ref, k_ref, v_ref = refs[:3]
    refs = refs[3:]
    kc_ref = vc_ref = None
    if past:
        kc_ref, vc_ref = refs[:2]
        refs = refs[2:]
    return (q_ref, k_ref, v_ref, kc_ref, vc_ref), refs[:n_extra], refs[n_extra:]


def _pipeline_prologue(nq, past, v_ref, vc_ref, s_b, m_b, vt_scr):
    s = pl.program_id(0)

    @pl.when(s == 0)
    def _():
        s_b[...] = jnp.zeros_like(s_b)
        m_b[...] = jnp.zeros_like(m_b)

    @pl.when(lax.rem(jnp.maximum(s - 1, 0), nq) == 0)
    def _():
        if past:
            vt_scr[:, 0:past] = vc_ref[...].T
        vt_scr[:, past:] = v_ref[...].T

    return lax.rem(s, 2)


def _key_chunks(n_kv, past):
    blocks = (n_kv - past) // LANE
    n = min(KEY_CHUNKS, blocks)
    edges = ([0] if past else []) + [past + (i * blocks // n) * LANE for i in range(n + 1)]
    return list(zip(edges[:-1], edges[1:]))


def _keys_of(k_ref, kc_ref, past, lanes):
    def k_of(c0, c1):
        return kc_ref[c0:c1, lanes] if c1 <= past else k_ref[c0 - past:c1 - past, lanes]
    return k_of


def _score_maps(maps, past, s_n, m_n, s_c, m_c):
    out = []
    for a, (k_of, q, vt_of) in enumerate(maps):
        mc = m_c[a]
        mx = l = acc = None
        for c0, c1 in _key_chunks(s_n.shape[1], past):
            st = _dot_nt(k_of(c0, c1), q)
            s_n[a, c0:c1, :] = st
            cm = jnp.max(st, axis=0, keepdims=True)
            mx = cm if mx is None else jnp.maximum(mx, cm)
            p = jnp.exp2(s_c[a, c0:c1, :] - mc)
            ps = jnp.sum(p, axis=0, keepdims=True)
            l = ps if l is None else l + ps
            pv = _dot(vt_of(c0, c1), p.astype(BF16))
            acc = pv if acc is None else acc + pv
        m_n[a] = mx
        out.append((acc, l))
    return out


def _mla_kernel(gps, nq, past, *refs):
    (q_ref, k_ref, v_ref, kc_ref, vc_ref), _, (o_ref, s_a, m_a, s_b, m_b, vt_scr) = _attention_refs(past, refs, 0)
    parity = _pipeline_prologue(nq, past, v_ref, vc_ref, s_b, m_b, vt_scr)

    def body(s_n, m_n, s_c, m_c):
        maps = []
        for e in range(2 * gps):
            lanes = slice(e * LANE, (e + 1) * LANE)
            maps.append((_keys_of(k_ref, kc_ref, past, lanes), q_ref[:, lanes],
                         lambda c0, c1, e=e: vt_scr[e * V_A:(e + 1) * V_A, c0:c1]))
        outs = [acc / l for acc, l in _score_maps(maps, past, s_n, m_n, s_c, m_c)]
        o_ref[...] = jnp.concatenate(outs, axis=0).T.astype(BF16)

    @pl.when(parity == 0)
    def _():
        body(s_a, m_a, s_b, m_b)

    @pl.when(parity == 1)
    def _():
        body(s_b, m_b, s_a, m_a)


def _mla_attention(q, kv_new, kv_cache, batch, n_q):
    return _pipelined_attention(_mla_kernel, q, kv_new, kv_cache, (), lambda cur_grp, gps: [], batch, n_q,
                                H_A // 2, (2 * LANE, 2 * LANE, LANE, LANE), 2, "mla_attention")


def _diff_kernel(lam_init, gps, nq, past, *refs):
    ((q_ref, k_ref, v_ref, kc_ref, vc_ref), (lam_ref, g_ref),
     (o_ref, s_a, m_a, s_b, m_b, vt_scr)) = _attention_refs(past, refs, 2)
    parity = _pipeline_prologue(nq, past, v_ref, vc_ref, s_b, m_b, vt_scr)

    def body(s_n, m_n, s_c, m_c):
        lane = lax.broadcasted_iota(jnp.int32, (1, LANE), 1)
        lv = lam_ref[...]
        lam = (jnp.exp(jnp.sum(lv[0:1] * lv[1:2], axis=-1, keepdims=True))
               - jnp.exp(jnp.sum(lv[2:3] * lv[3:4], axis=-1, keepdims=True)) + lam_init)
        maps = []
        for h in range(gps):
            lanes = slice(h * LANE, (h + 1) * LANE)
            q = q_ref[:, lanes]
            zero = jnp.zeros_like(q)
            k_of = _keys_of(k_ref, kc_ref, past, lanes)
            vt_of = lambda c0, c1, h=h: vt_scr[h * LANE:(h + 1) * LANE, c0:c1]
            maps += [(k_of, jnp.where(lane < DK_D, q, zero), vt_of), (k_of, jnp.where(lane >= DK_D, q, zero), vt_of)]
        res = _score_maps(maps, past, s_n, m_n, s_c, m_c)
        outs = []
        for h in range(gps):
            (acc1, l1), (acc2, l2) = res[2 * h], res[2 * h + 1]
            o = (acc1 / l1 - acc2 * (lam / l2)).T
            outs.append(_rms_rows(o, g_ref[:, h * LANE:(h + 1) * LANE]) * (1.0 - lam_init))
        o_ref[...] = (outs[0] if gps == 1 else jnp.concatenate(outs, axis=1)).astype(BF16)

    @pl.when(parity == 0)
    def _():
        body(s_a, m_a, s_b, m_b)

    @pl.when(parity == 1)
    def _():
        body(s_b, m_b, s_a, m_a)


def _diff_attention(q, kv_new, kv_cache, l, wts, lam_init, batch, n_q):
    extra = lambda cur_grp, gps: [pl.BlockSpec((None, 4, DK_D), lambda s: (l, 0, 0)),
                                  pl.BlockSpec((None, 1, gps * LANE), lambda s: (l, 0, cur_grp(s)))]
    return _pipelined_attention(functools.partial(_diff_kernel, lam_init), q, kv_new, kv_cache,
                                (wts["diff_lambda"], wts["diff_norm_g"]), extra, batch, n_q, H_D,
                                (LANE, LANE, LANE, LANE), 2, "diff_attention")


def _merge_kernel(final, x_ref, mod_ref, hf_ref, hb_ref, mo_ref, oa_ref, od_ref, gate_ref,
                  gm_ref, wbm_ref, wba_ref, wbd_ref, wout_ref, g2_ref, wff1_ref, wff2_ref, gfin_ref, o_ref):
    d = D_MODEL
    mod = mod_ref[...]
    hm = hf_ref[...] + hb_ref[...]
    gm = gm_ref[...]
    o_m = jnp.concatenate(
        [_rms_rows(hm[:, h * DV_M:(h + 1) * DV_M], gm[:, h * DV_M:(h + 1) * DV_M]) for h in range(H_M)], axis=1)
    o_m = (o_m * jax.nn.sigmoid(mo_ref[...])).astype(BF16)
    gate = jax.nn.sigmoid(gate_ref[...])
    y = (gate[:, 0:d] * _dot(o_m, wbm_ref[...]) + gate[:, d:2 * d] * _dot(oa_ref[...], wba_ref[...])
         + gate[:, 2 * d:3 * d] * _dot(od_ref[...], wbd_ref[...]))
    x = x_ref[...] + mod[:, 2 * d:3 * d] * _dot(y.astype(BF16), wout_ref[...])
    h2 = (_rms_rows(x, g2_ref[...]) * (1.0 + mod[:, 4 * d:5 * d]) + mod[:, 3 * d:4 * d]).astype(BF16)
    f = jnp.maximum(_dot(h2, wff1_ref[...]), 0.0)
    x = x + mod[:, 5 * d:6 * d] * _dot((f * f).astype(BF16), wff2_ref[...])
    if final:
        x = _rms_rows(x, gfin_ref[...])
    o_ref[...] = x


def _merge(x2, l, mod, wts, batch, n_tok, latent, m_in, h_f, h_b, o_a, o_d, gate, gfin):
    t = x2.shape[0]
    tm = ROW_TILE
    tpb = n_tok // tm
    row = lambda i: (i, 0)
    in_specs = [
        pl.BlockSpec((tm, D_MODEL), row),
        _mod_spec(l, 1, tpb) if latent else _mod_spec(l, 0, batch * tpb),
        pl.BlockSpec((tm, 512), row),
        pl.BlockSpec((tm, 512), row),
        pl.BlockSpec((tm, 512), lambda i: (i, C_MO // 512)),
        pl.BlockSpec((tm, 512), row),
        pl.BlockSpec((tm, 512), row),
        pl.BlockSpec((tm, G_COLS), row),
        _layer_spec(l, (1, 512)),
        _layer_spec(l, (512, D_MODEL)),
        _layer_spec(l, (512, D_MODEL)),
        _layer_spec(l, (512, D_MODEL)),
        _layer_spec(l, (D_MODEL, D_MODEL)),
        _layer_spec(l, (1, D_MODEL)),
        _layer_spec(l, (D_MODEL, D_FF)),
        _layer_spec(l, (D_FF, D_MODEL)),
        pl.BlockSpec((1, D_MODEL), lambda i: (0, 0)),
    ]
    return pl.pallas_call(
        functools.partial(_merge_kernel, l == DEPTH - 1),
        grid=(t // tm,),
        in_specs=in_specs,
        out_specs=pl.BlockSpec((tm, D_MODEL), row),
        out_shape=jax.ShapeDtypeStruct((t, D_MODEL), F32),
        compiler_params=_params(("parallel",)),
        name="merge_mlp",
    )(x2, mod, h_f, h_b, m_in, o_a, o_d, gate, wts["mlstm_norm_g"], wts["w_br_mlstm"], wts["w_br_mla"],
      wts["w_br_diff"], wts["w_out"], wts["norm2_g"], wts["w_ff1"], wts["w_ff2"], gfin)


def _repack_kernel(w_ref, o_ref):
    w = w_ref[...].astype(BF16)
    rows = w.shape[0]
    c_mg = 2 * H_M * DK_M + 2 * H_M * DV_M
    c_acq = c_mg + 4 * H_M
    c_akr = c_acq + Q_RANK + KV_RANK
    c_dq = c_akr + ROPE_A
    z = lambda n: jnp.zeros((rows, n), BF16)
    o_ref[...] = jnp.concatenate([
        w[:, :c_mg + 8], z(LANE - 8), w[:, c_mg + 8:c_acq], z(LANE - 8),
        w[:, c_acq:c_akr], z(NOPE_A), w[:, c_akr:c_dq], z(LANE - NOPE_A - ROPE_A), w[:, c_dq:]], axis=1)


def _repack_w_in(w_in):
    depth, d, cols = w_in.shape
    tr = 128
    return pl.pallas_call(
        _repack_kernel,
        grid=(depth, d // tr),
        in_specs=[pl.BlockSpec((None, tr, cols), lambda l, i: (l, i, 0))],
        out_specs=pl.BlockSpec((None, tr, IN_COLS_P), lambda l, i: (l, i, 0)),
        out_shape=jax.ShapeDtypeStruct((depth, d, IN_COLS_P), BF16),
        compiler_params=_params(("parallel", "parallel")),
        name="repack_w_in",
    )(w_in)


def _prep_weights(w_in, mlstm_gate_b, norm1_g, mlstm_norm_g, mla_q_norm_g, mla_w_q_up, mla_kv_norm_g,
                  mla_w_kv_up, diff_lambda, diff_norm_g, w_br_mlstm, w_br_mla, w_br_diff, w_out,
                  norm2_g, w_ff1, w_ff2):
    depth = w_in.shape[0]
    w_p = _repack_w_in(w_in)
    pad8 = lambda a: jnp.pad(a, ((0, 0), (0, LANE - 8)))[:, None, :]
    wq = mla_w_q_up.astype(BF16).reshape(depth, Q_RANK, H_A, NOPE_A + ROPE_A)
    wq = jnp.pad(wq, ((0, 0), (0, 0), (0, 0), (0, LANE - NOPE_A - ROPE_A))).reshape(depth, Q_RANK, H_A * LANE)
    wkv = mla_w_kv_up.astype(BF16).reshape(depth, KV_RANK, H_A, NOPE_A + V_A)
    wk = jnp.pad(wkv[..., :NOPE_A], ((0, 0), (0, 0), (0, 0), (0, LANE - NOPE_A))).reshape(depth, KV_RANK, H_A * LANE)
    wv = wkv[..., NOPE_A:].reshape(depth, KV_RANK, H_A * V_A)
    row = lambda a: a[:, None, :]
    return {
        "w_in": w_p,
        "gate_bi": pad8(mlstm_gate_b[:, :8]), "gate_bf": pad8(mlstm_gate_b[:, 8:]),
        "norm1_g": row(norm1_g), "norm2_g": row(norm2_g), "mlstm_norm_g": row(mlstm_norm_g),
        "mla_q_norm_g": row(mla_q_norm_g), "mla_kv_norm_g": row(mla_kv_norm_g),
        "wq": wq, "wk": wk, "wv": wv,
        "diff_lambda": diff_lambda, "diff_norm_g": row(diff_norm_g),
        "w_br_mlstm": w_br_mlstm.astype(BF16), "w_br_mla": w_br_mla.astype(BF16),
        "w_br_diff": w_br_diff.astype(BF16), "w_out": w_out.astype(BF16),
        "w_ff1": w_ff1.astype(BF16), "w_ff2": w_ff2.astype(BF16),
    }


def _rope_tables(n_tokens):
    rows = n_tokens // GRID_W
    row = jnp.repeat(jnp.arange(rows, dtype=F32), GRID_W)
    col = jnp.tile(jnp.arange(GRID_W, dtype=F32), rows)

    def cs(dim):
        quarter = dim // 4
        inv = ROPE_BASE ** (-jnp.arange(quarter, dtype=F32) / quarter)
        ang = jnp.concatenate([row[:, None] * inv, col[:, None] * inv], axis=-1)
        return jnp.cos(ang), jnp.sin(ang)

    ca, sa = cs(ROPE_A)
    cd, sd = cs(DK_D)
    src = jnp.concatenate([ca, sa, cd, sd, jnp.ones((n_tokens, 1), F32), jnp.zeros((n_tokens, LANE - 97), F32)],
                          axis=1)
    i_ca, i_sa, i_cd, i_sd, i_one = 0, 16, 32, 64, 96
    place = np.zeros((LANE, 6 * LANE), np.float32)

    def put(block, lane0, src0, n, sign=1.0):
        for t in range(n):
            place[src0 + t, block * LANE + lane0 + t] = sign

    for lane in list(range(64)) + list(range(96, 128)):
        place[i_one, lane] = 1.0
    put(0, 64, i_ca, 16), put(0, 80, i_ca, 16)
    put(1, 80, i_sa, 16), put(2, 64, i_sa, 16, -1.0)
    for off in (0, 32, 64, 96):
        put(3, off, i_cd, 32)
    put(4, 32, i_sd, 32), put(4, 96, i_sd, 32)
    put(5, 0, i_sd, 32, -1.0), put(5, 64, i_sd, 32, -1.0)
    return jnp.dot(src, jnp.asarray(place), precision=lax.Precision.HIGHEST)


def _state_mask():
    r = jnp.arange(256)[:, None] // DK_M
    c = jnp.arange(S_COLS)[None, :]
    diag = (c < H_M * DV_M) & (c // DV_M == r)
    return jnp.stack([(diag | (c == H_M * DV_M + d * H_M + r)) for d in range(2)]).astype(F32)


def _layer(x2, l, mod, wts, smask, gfin, batch, n_tok, cache=None, tables=None, ctx_stacks=None):
    latent = cache is not None
    lam_init = 0.8 - 0.6 * math.exp(-0.3 * l)
    outs = _inproj(x2, l, mod, wts, batch, n_tok, tables, ctx_stacks)
    m_in, q_a, k_a, v_a, q_d, k_d, v_d, gate = outs[:8]
    kv_a = kv_d = None
    if latent:
        ck_a, cv_a, ck_d, cv_d = _cache_kv(l, wts, cache, batch)
        kv_a, kv_d = (ck_a, cv_a), (ck_d, cv_d)
    h_f, h_b, c_t, n_t, m_t = _mlstm(m_in, l, wts, smask, batch, n_tok, cache["state"] if latent else None)
    o_a = _mla_attention(q_a, (k_a, v_a), kv_a, batch, n_tok)
    o_d = _diff_attention(q_d, (k_d, v_d), kv_d, l, wts, lam_init, batch, n_tok)
    x_new = _merge(x2, l, mod, wts, batch, n_tok, latent, m_in, h_f, h_b, o_a, o_d, gate, gfin)
    return x_new, tuple(outs[8:]), (c_t, n_t, m_t)


def _cache_kv_kernel(ckv_ref, kr_ref, kd_ref, vd_ref, wk_ref, wv_ref, ka_out, va_out, kd_out, vd_out):
    ckv = ckv_ref[...].astype(BF16)
    ka_out[...] = (_dot(ckv, wk_ref[...]) + jnp.concatenate([kr_ref[...]] * H_A, axis=1)).astype(BF16)
    va_out[...] = _dot(ckv, wv_ref[...]).astype(BF16)
    kd_out[...] = kd_ref[...].astype(BF16)
    vd_out[...] = vd_ref[...].astype(BF16)


def _cache_kv(l, wts, cache, batch):
    past = cache["past"]
    tm = ROW_TILE
    ppb = past // tm
    cached = lambda i: (i // ppb, l, i % ppb, 0)
    widths = (H_A * LANE, H_A * V_A, 512, 512)
    return pl.pallas_call(
        _cache_kv_kernel,
        grid=(batch * ppb,),
        in_specs=[pl.BlockSpec((None, None, tm, KV_RANK), cached),
                  pl.BlockSpec((None, None, tm, LANE), cached),
                  pl.BlockSpec((None, None, tm, 512), cached),
                  pl.BlockSpec((None, None, tm, 512), cached),
                  _layer_spec(l, (KV_RANK, H_A * LANE)), _layer_spec(l, (KV_RANK, H_A * V_A))],
        out_specs=[pl.BlockSpec((tm, w), lambda i: (i, 0)) for w in widths],
        out_shape=[jax.ShapeDtypeStruct((batch * past, w), BF16) for w in widths],
        compiler_params=_params(("parallel",)),
        name="cache_kv",
    )(cache["ckv"], cache["krope"], cache["diff_k"], cache["diff_v"], wts["wk"], wts["wv"])


def kernel(x_prompt, x_sample, c, cache_mla_ckv, cache_mla_krope, cache_diff_k, cache_diff_v, state_mlstm_C, state_mlstm_n, state_mlstm_m, c_ctx, w_mod, b_mod, norm1_g, w_in, mlstm_gate_b, mlstm_norm_g, mla_q_norm_g, mla_w_q_up, mla_kv_norm_g, mla_w_kv_up, diff_lambda, diff_norm_g, w_br_mlstm, w_br_mla, w_br_diff, w_out, norm2_g, w_ff1, w_ff2, final_norm_g):
    bp, sp, _ = x_prompt.shape
    bs, ss, _ = x_sample.shape
    past = cache_mla_ckv.shape[2]
    assert bs + 1 <= 8 and ss % GRID_W == 0
    assert sp % ROW_TILE == 0 and ss % ROW_TILE == 0 and past % ROW_TILE == 0

    cond8 = jnp.concatenate([c_ctx[None, :], c, jnp.zeros((8 - 1 - bs, D_MODEL), F32)], axis=0)
    mod = _modulation(cond8, w_mod, b_mod).reshape(DEPTH, 8, 1, 6 * D_MODEL)
    wts = _prep_weights(w_in, mlstm_gate_b, norm1_g, mlstm_norm_g, mla_q_norm_g, mla_w_q_up, mla_kv_norm_g,
                        mla_w_kv_up, diff_lambda, diff_norm_g, w_br_mlstm, w_br_mla, w_br_diff, w_out,
                        norm2_g, w_ff1, w_ff2)
    tables = _rope_tables(ss)
    smask = _state_mask()
    gfin = final_norm_g[None, :]
    rows = H_M * DK_M
    cache = {
        "past": past,
        "ckv": cache_mla_ckv,
        "krope": jnp.pad(cache_mla_krope, ((0, 0), (0, 0), (0, 0), (NOPE_A, LANE - NOPE_A - ROPE_A))),
        "diff_k": cache_diff_k.reshape(bs, DEPTH, past, H_D * 2 * DK_D),
        "diff_v": cache_diff_v.reshape(bs, DEPTH, past, H_D * DV_D),
        "state": (state_mlstm_C.reshape(bs, DEPTH, 2, rows, DV_M),
                  state_mlstm_n.reshape(bs, DEPTH, 2, rows, 1),
                  jnp.pad(state_mlstm_m.reshape(bs, DEPTH, 1, 2 * H_M), ((0, 0), (0, 0), (0, 0), (0, LANE - 2 * H_M)))),
    }

    y_p = x_prompt.reshape(bp * sp, D_MODEL)
    y_s = x_sample.reshape(bs * ss, D_MODEL)
    stacks = None
    states = []
    for l in range(DEPTH):
        y_p, stacks, state = _layer(y_p, l, mod, wts, smask, gfin, bp, sp, ctx_stacks=stacks)
        states.append(state)
        y_s, _, _ = _layer(y_s, l, mod, wts, smask, gfin, bs, ss, cache=cache, tables=tables)

    ckv, akr, kd, vd = stacks
    c_t = jnp.stack([s[0] for s in states], axis=1).reshape(bp, DEPTH, 2, H_M, DK_M, DV_M)
    n_t = jnp.stack([s[1] for s in states], axis=1).reshape(bp, DEPTH, 2, H_M, DK_M)
    m_t = jnp.stack([s[2][:, 0, :2 * H_M] for s in states], axis=1).reshape(bp, DEPTH, 2, H_M)
    return (y_p.reshape(bp, sp, D_MODEL), y_s.reshape(bs, ss, D_MODEL),
            ckv, akr[..., NOPE_A:NOPE_A + ROPE_A],
            kd.reshape(bp, DEPTH, sp, H_D, 2 * DK_D), vd.reshape(bp, DEPTH, sp, H_D, DV_D),
            c_t, n_t, m_t)
```

```python
import functools
import math

import jax
import jax.numpy as jnp
import numpy as np
from jax import lax
from jax.experimental import pallas as pl
from jax.experimental.pallas import tpu as pltpu

F32 = jnp.float32
BF16 = jnp.bfloat16

D_MODEL = 1024
DEPTH = 2
GRID_W = 64
ROPE_BASE = 10000.0
EPS = 1e-6
H_M, DK_M, DV_M = 4, 64, 128
H_A, Q_RANK, KV_RANK, NOPE_A, ROPE_A, V_A = 8, 384, 256, 64, 32, 64
H_D, DK_D, DV_D = 4, 64, 128
D_FF = 4 * D_MODEL

LANE = 128
VMEM_LIMIT = 56 * 1024 * 1024
ROW_TILE = 256
Q_TILE = 256
CHUNK = 128
KEY_CHUNKS = 4
SCORE_VMEM_BUDGET = 24 * 1024 * 1024
HEAD_GROUP = 4
MLSTM_STREAMS = 2
NEG = -1e30
LOG2E = 1.4426950408889634

C_MQ, C_MK, C_MV, C_MO, C_GI, C_GF = 0, 256, 512, 1024, 1536, 1664
M_COLS = 1792
C_A = 1792
A_COLS = 768
C_D = 2560
D_COLS = 1536
C_G = 4096
G_COLS = 3 * D_MODEL
IN_COLS_P = 7168
S_COLS = H_M * DV_M + LANE


def _dot(a, b):
    return jnp.dot(a, b, preferred_element_type=F32)


def _dot_nt(a, b):
    return lax.dot_general(a, b, (((1,), (1,)), ((), ())), preferred_element_type=F32)


def _dot_tn(a, b):
    return lax.dot_general(a, b, (((0,), (0,)), ((), ())), preferred_element_type=F32)


def _rms_rows(x, g):
    return x * lax.rsqrt(jnp.mean(x * x, axis=-1, keepdims=True) + EPS) * g


def _rope_blocks(x, cos, sa, sb, shift):
    outs = []
    for b in range(x.shape[1] // LANE):
        xb = x[:, b * LANE:(b + 1) * LANE]
        outs.append(xb * cos + pltpu.roll(xb, shift, 1) * sa + pltpu.roll(xb, LANE - shift, 1) * sb)
    return outs[0] if len(outs) == 1 else jnp.concatenate(outs, axis=1)


def _params(sem):
    return pltpu.CompilerParams(dimension_semantics=sem, vmem_limit_bytes=VMEM_LIMIT)


def _layer_spec(l, shape):
    nd = len(shape)
    return pl.BlockSpec((None,) + shape, lambda *_: (l,) + (0,) * nd, pipeline_mode=pl.Buffered(1))


def _mod_spec(l, first_row, tiles_per_cond):
    return pl.BlockSpec((None, None, 1, 6 * D_MODEL), lambda i: (l, first_row + i // tiles_per_cond, 0, 0))


_ANY = pl.BlockSpec(memory_space=pl.ANY)


def _mod_kernel(c_ref, w_ref, b_ref, o_ref):
    c = c_ref[...]
    s = (c * jax.nn.sigmoid(c)).astype(BF16)
    o_ref[...] = _dot(s, w_ref[...].astype(BF16)) + b_ref[...]


def _modulation(cond8, w_mod, b_mod):
    tn = 1536
    n6 = 6 * D_MODEL
    return pl.pallas_call(
        _mod_kernel,
        grid=(DEPTH, n6 // tn),
        in_specs=[
            pl.BlockSpec((8, D_MODEL), lambda l, j: (0, 0)),
            pl.BlockSpec((None, D_MODEL, tn), lambda l, j: (l, 0, j)),
            pl.BlockSpec((None, 1, tn), lambda l, j: (l, 0, j)),
        ],
        out_specs=pl.BlockSpec((None, 8, tn), lambda l, j: (l, 0, j)),
        out_shape=jax.ShapeDtypeStruct((DEPTH, 8, n6), F32),
        compiler_params=_params(("parallel", "parallel")),
        name="modulation",
    )(cond8, w_mod, b_mod.reshape(DEPTH, 1, n6))


def _store_slots(ref, val):
    if len(ref.shape) == 3:
        for s in range(ref.shape[0]):
            ref[s] = val
    else:
        ref[...] = val


def _inproj_kernel(latent, n_aliased, *refs):
    (x_ref, mod_ref, g1_ref, w_ref, gq_ref, gkv_ref, wq_ref, wk_ref, wv_ref) = refs[:9]
    refs = refs[9:]
    if latent:
        tab_ref = refs[0]
        (m_ref, q_ref, k_ref, va_ref, qd_ref, kd_ref, vd_ref, gate_ref) = refs[1:]
    else:
        (m_ref, q_ref, k_ref, va_ref, qd_ref, kd_ref, vd_ref, gate_ref,
         ckv_ref, akr_ref, kdraw_ref, vdraw_ref) = refs[n_aliased:]
    d = D_MODEL
    x = x_ref[...]
    mod = mod_ref[...]
    h = (_rms_rows(x, g1_ref[...]) * (1.0 + mod[:, d:2 * d]) + mod[:, 0:d]).astype(BF16)

    m_ref[...] = _dot(h, w_ref[:, 0:M_COLS])

    za = _dot(h, w_ref[:, C_A:C_A + A_COLS])
    acq = za[:, 0:Q_RANK]
    ackv = za[:, Q_RANK:Q_RANK + KV_RANK]
    akr = za[:, Q_RANK + KV_RANK:A_COLS]
    q = _dot(_rms_rows(acq, gq_ref[...]).astype(BF16), wq_ref[...])
    ckv = _rms_rows(ackv, gkv_ref[...])
    ckv_b = ckv.astype(BF16)
    kn = _dot(ckv_b, wk_ref[...])
    if not latent:
        _store_slots(ckv_ref, ckv)
        _store_slots(akr_ref, akr)
    else:
        tab = tab_ref[...]
        cq, saq, sbq = tab[:, 0:128], tab[:, 128:256], tab[:, 256:384]
        q = _rope_blocks(q, cq, saq, sbq, ROPE_A // 2)
        akr = _rope_blocks(akr, cq, saq, sbq, ROPE_A // 2)
    q_ref[...] = (q * (LOG2E * (NOPE_A + ROPE_A) ** -0.5)).astype(BF16)
    k_ref[...] = (kn + jnp.concatenate([akr] * H_A, axis=1)).astype(BF16)
    va_ref[...] = _dot(ckv_b, wv_ref[...]).astype(BF16)

    zd = _dot(h, w_ref[:, C_D:C_D + D_COLS])
    dq, dk, dv = zd[:, 0:512], zd[:, 512:1024], zd[:, 1024:1536]
    if not latent:
        _store_slots(kdraw_ref, dk)
        _store_slots(vdraw_ref, dv)
    else:
        cd, sad, sbd = tab[:, 384:512], tab[:, 512:640], tab[:, 640:768]
        dq = _rope_blocks(dq, cd, sad, sbd, DK_D // 2)
        dk = _rope_blocks(dk, cd, sad, sbd, DK_D // 2)
    qd_ref[...] = (dq * (LOG2E * DK_D ** -0.5)).astype(BF16)
    kd_ref[...] = dk.astype(BF16)
    vd_ref[...] = dv.astype(BF16)

    gate_ref[...] = _dot(h, w_ref[:, C_G:C_G + G_COLS])


def _inproj(x2, l, mod, wts, batch, n_tok, tables=None, ctx_stacks=None):
    latent = tables is not None
    t = batch * n_tok
    tm = ROW_TILE
    tpb = n_tok // tm
    row = lambda i: (i, 0)

    in_specs = [
        pl.BlockSpec((tm, D_MODEL), row),
        _mod_spec(l, 1, tpb) if latent else _mod_spec(l, 0, batch * tpb),
        _layer_spec(l, (1, D_MODEL)),
        _layer_spec(l, (D_MODEL, IN_COLS_P)),
        _layer_spec(l, (1, Q_RANK)),
        _layer_spec(l, (1, KV_RANK)),
        _layer_spec(l, (Q_RANK, H_A * LANE)),
        _layer_spec(l, (KV_RANK, H_A * LANE)),
        _layer_spec(l, (KV_RANK, H_A * V_A)),
    ]
    args = [x2, mod, wts["norm1_g"], wts["w_in"], wts["mla_q_norm_g"], wts["mla_kv_norm_g"],
            wts["wq"], wts["wk"], wts["wv"]]
    widths = [(M_COLS, F32), (H_A * LANE, BF16), (H_A * LANE, BF16), (H_A * V_A, BF16),
              (512, BF16), (512, BF16), (512, BF16), (G_COLS, F32)]
    out_specs = [pl.BlockSpec((tm, w), row) for w, _ in widths]
    out_shape = [jax.ShapeDtypeStruct((t, w), dt) for w, dt in widths]
    aliases = {}
    if latent:
        in_specs.append(pl.BlockSpec((tm, 6 * LANE), lambda i: (i % tpb, 0)))
        args.append(tables)
    else:
        if ctx_stacks is None:
            stack_spec = lambda w: pl.BlockSpec((None, DEPTH, tm, w), lambda i: (i // tpb, 0, i % tpb, 0))
        else:
            stack_spec = lambda w: pl.BlockSpec((None, None, tm, w), lambda i: (i // tpb, l, i % tpb, 0))
            aliases = {len(args) + n: 8 + n for n in range(4)}
            in_specs += [_ANY] * 4
            args += list(ctx_stacks)
        for w in (KV_RANK, LANE, 512, 512):
            out_specs.append(stack_spec(w))
            out_shape.append(jax.ShapeDtypeStruct((batch, DEPTH, n_tok, w), F32))
    return pl.pallas_call(
        functools.partial(_inproj_kernel, latent, len(aliases)),
        grid=(t // tm,),
        in_specs=in_specs,
        out_specs=out_specs,
        out_shape=out_shape,
        input_output_aliases=aliases,
        compiler_params=_params(("parallel",)),
        name="inproj_lat" if latent else "inproj_ctx",
    )(*args)


def _split3(x):
    hi = x.astype(BF16)
    r1 = x - hi.astype(F32)
    mid = r1.astype(BF16)
    return hi, mid, (r1 - mid.astype(F32)).astype(BF16)


def _mlstm_chunks(items, bi, bfg, smask_ref):
    L = CHUNK
    r_i = lax.broadcasted_iota(jnp.int32, (L, L), 0)
    c_i = lax.broadcasted_iota(jnp.int32, (L, L), 1)
    lane256 = lax.broadcasted_iota(jnp.int32, (L, 256), 1)
    lane128 = lax.broadcasted_iota(jnp.int32, (1, LANE), 1)
    masks = {0: c_i <= r_i, 1: c_i >= r_i}
    tris = {d: jnp.where(m, 1.0, 0.0).astype(BF16) for d, m in masks.items()}
    heads = [(n, hd) for n in range(len(items)) for hd in range(H_M)]

    pre = []
    for z_of, s_of, _, m_prev, d, _ in items:
        q = (z_of(C_MQ, 256) * DK_M ** -0.5).astype(BF16)
        gi = z_of(C_GI, LANE) + bi
        xf = z_of(C_GF, LANE) + bfg
        lf = jnp.minimum(xf, 0.0) - jnp.log(1.0 + jnp.exp(-jnp.abs(xf)))
        b = sum(_dot(tris[d], part) for part in _split3(lf))
        g = gi - b
        pre.append(dict(q=q, kb=z_of(C_MK, 256).astype(BF16), g_t=g.T, b=b, g=g, last=L - 1 if d == 0 else 0))

    blk = lambda x, hd: x[:, hd * LANE:(hd + 1) * LANE]
    rep = lambda x, j: jnp.broadcast_to(x[:, j:j + 1], (L, LANE))
    u, av = {}, {}
    for g0 in range(0, len(heads), HEAD_GROUP):
        group = heads[g0:g0 + HEAD_GROUP]
        gm = {}
        for n, hd in group:
            d = items[n][4]
            j = d * H_M + hd
            gm[n, hd] = jnp.where(masks[d], pre[n]["g_t"][j:j + 1, :], NEG)
            row_max = jnp.broadcast_to(jnp.max(gm[n, hd], axis=-1, keepdims=True), (L, LANE))
            u[n, hd] = jnp.maximum(row_max, items[n][3][:, j:j + 1])
        a = {}
        for n, hd in group:
            q = pre[n]["q"]
            q_h = jnp.where((lane256 >= hd * DK_M) & (lane256 < (hd + 1) * DK_M), q, jnp.zeros_like(q))
            a[n, hd] = jnp.exp(gm[n, hd] - u[n, hd]) * _dot_nt(q_h, pre[n]["kb"])
        for n, hd in group:
            v_h = items[n][0](C_MV + hd * DV_M, DV_M).astype(BF16)
            av[n, hd] = _dot(a[n, hd].astype(BF16), jnp.concatenate([v_h, jnp.ones((L, LANE), BF16)], axis=1))

    lane_l = lax.broadcasted_iota(jnp.int32, (L, LANE), 1)
    results = []
    for n, (z_of, s_of, s_set, m_prev, d, h_set) in enumerate(items):
        p = pre[n]
        s_prev = s_of()
        p["qs"] = _dot(p["q"], s_prev.astype(BF16))
        w_rep, w_old = [], []
        m_new = m_prev
        for hd in range(H_M):
            j = d * H_M + hd
            uh = u[n, hd]
            mp = m_prev[:, j:j + 1]
            b_h = rep(p["b"], j)
            w_int = jnp.exp(mp - uh)
            num = w_int * blk(p["qs"], hd) + av[n, hd][:, :DV_M]
            den = w_int * rep(p["qs"], H_M * DV_M + j) + av[n, hd][:, DV_M:]
            h_set(hd, num / jnp.maximum(jnp.abs(den), jnp.exp(-(b_h + uh))))
            u_l = uh[p["last"]:p["last"] + 1, :]
            m_new = jnp.where(lane128 == j, b_h[p["last"]:p["last"] + 1, :] + u_l, m_new)
            w_rep.append(jnp.exp(rep(p["g"], j) - u_l))
            w_old.append(jnp.exp(mp - u_l))
        w_exp = jnp.concatenate([jnp.where(lane_l < DK_M, w_rep[0], w_rep[1]),
                                 jnp.where(lane_l < DK_M, w_rep[2], w_rep[3])], axis=1)
        k_w = (z_of(C_MK, 256) * w_exp).astype(BF16)
        v_aug = jnp.concatenate([z_of(C_MV, H_M * DV_M).astype(BF16), jnp.ones((L, LANE), BF16)], axis=1)
        upd = _dot_tn(k_w, v_aug)
        n_scale = jnp.zeros((1, LANE), F32)
        for hd in range(H_M):
            n_scale = jnp.where(lane128 == d * H_M + hd, w_old[hd], n_scale)
        scale_row = jnp.concatenate(w_old + [n_scale], axis=1)
        s_set(scale_row * s_prev + smask_ref[d] * upd)
        results.append(m_new)
    return results


def _mlstm_kernel(has_state, *refs):
    zf_ref, zb_ref, bi_ref, bf_ref, smask_ref = refs[:5]
    refs = refs[5:]
    if has_state:
        c0_ref, n0_ref, m0_ref = refs[:3]
        refs = refs[3:]
    hf_ref, hb_ref, ct_ref, nt_ref, mt_ref, s_scr, m_scr = refs
    i = pl.program_id(1)
    hv = H_M * DV_M

    streams = s_scr.shape[0]

    @pl.when(i == 0)
    def _():
        if has_state:
            for g in range(streams):
                for d in range(2):
                    tiled = jnp.concatenate(
                        [c0_ref[g, d]] * H_M + [jnp.broadcast_to(n0_ref[g, d], (256, LANE))], axis=1)
                    s_scr[g, d] = smask_ref[d] * tiled
                    m_scr[g, d] = m0_ref[g]
        else:
            s_scr[...] = jnp.zeros_like(s_scr)
            m_scr[...] = jnp.zeros_like(m_scr)

    bi = bi_ref[...]
    bfg = bf_ref[...]
    slots = [(g, d) for g in range(streams) for d in range(2)]

    def item(g, d):
        z_ref, h_ref = (zf_ref, hf_ref) if d == 0 else (zb_ref, hb_ref)

        def h_set(hd, val):
            h_ref[g, :, hd * DV_M:(hd + 1) * DV_M] = val

        def s_set(val):
            s_scr[g, d] = val

        return (lambda c, w: z_ref[g, :, c:c + w], lambda: s_scr[g, d], s_set, m_scr[g, d], d, h_set)

    for (g, d), m_new in zip(slots, _mlstm_chunks([item(g, d) for g, d in slots], bi, bfg, smask_ref)):
        m_scr[g, d] = m_new

    @pl.when(i == pl.num_programs(1) - 1)
    def _():
        lane = lax.broadcasted_iota(jnp.int32, (1, LANE), 1)
        for g in range(streams):
            for d in range(2):
                s = s_scr[g, d]
                ct_ref[g, d] = ((s[:, 0:DV_M] + s[:, DV_M:2 * DV_M])
                                + (s[:, 2 * DV_M:3 * DV_M] + s[:, 3 * DV_M:hv]))
                nt_ref[g, d] = jnp.sum(s[:, hv:], axis=-1, keepdims=True)
            mt_ref[g] = jnp.where(lane < H_M, m_scr[g, 0], m_scr[g, 1])


def _mlstm(m_in, l, wts, smask, batch, n_tok, state=None):
    nc = n_tok // CHUNK
    hv = H_M * DV_M
    g = math.gcd(batch, MLSTM_STREAMS)
    z3 = m_in.reshape(batch, n_tok, M_COLS)
    fwd = lambda b, i: (b, i, 0)
    bwd = lambda b, i: (b, nc - 1 - i, 0)
    in_specs = [
        pl.BlockSpec((g, CHUNK, M_COLS), fwd),
        pl.BlockSpec((g, CHUNK, M_COLS), bwd),
        pl.BlockSpec((None, 1, LANE), lambda b, i: (l, 0, 0)),
        pl.BlockSpec((None, 1, LANE), lambda b, i: (l, 0, 0)),
        pl.BlockSpec((2, 256, S_COLS), lambda b, i: (0, 0, 0)),
    ]
    args = [z3, z3, wts["gate_bi"], wts["gate_bf"], smask]
    if state is not None:
        in_specs += [
            pl.BlockSpec((g, None, 2, 256, DV_M), lambda b, i: (b, l, 0, 0, 0)),
            pl.BlockSpec((g, None, 2, 256, 1), lambda b, i: (b, l, 0, 0, 0)),
            pl.BlockSpec((g, None, 1, LANE), lambda b, i: (b, l, 0, 0)),
        ]
        args += list(state)
    h_f, h_b, c_t, n_t, m_t = pl.pallas_call(
        functools.partial(_mlstm_kernel, state is not None),
        grid=(batch // g, nc),
        in_specs=in_specs,
        out_specs=[
            pl.BlockSpec((g, CHUNK, hv), fwd),
            pl.BlockSpec((g, CHUNK, hv), bwd),
            pl.BlockSpec((g, 2, 256, DV_M), lambda b, i: (b, 0, 0, 0)),
            pl.BlockSpec((g, 2, 256, 1), lambda b, i: (b, 0, 0, 0)),
            pl.BlockSpec((g, 1, LANE), lambda b, i: (b, 0, 0)),
        ],
        out_shape=[
            jax.ShapeDtypeStruct((batch, n_tok, hv), F32),
            jax.ShapeDtypeStruct((batch, n_tok, hv), F32),
            jax.ShapeDtypeStruct((batch, 2, 256, DV_M), F32),
            jax.ShapeDtypeStruct((batch, 2, 256, 1), F32),
            jax.ShapeDtypeStruct((batch, 1, LANE), F32),
        ],
        scratch_shapes=[pltpu.VMEM((g, 2, 256, S_COLS), F32), pltpu.VMEM((g, 2, 1, LANE), F32)],
        compiler_params=_params(("parallel", "arbitrary")),
        name="mlstm",
    )(*args)
    return h_f.reshape(batch * n_tok, hv), h_b.reshape(batch * n_tok, hv), c_t, n_t, m_t


def _groups_per_step(n_groups, n_maps, n_kv, tq):
    return n_groups if 2 * n_groups * n_maps * n_kv * tq * 4 <= SCORE_VMEM_BUDGET else 1


def _pipelined_attention(kernel_fn, q, kv_new, kv_cache, extra_inputs, extra_specs, batch, n_q, n_groups,
                         widths, n_maps, name):
    past = 0 if kv_cache is None else kv_cache[0].shape[0] // batch
    n_kv = past + n_q
    tq = min(Q_TILE, n_q)
    nq = n_q // tq
    gps = _groups_per_step(n_groups, n_maps, n_kv, tq)
    n_groups //= gps
    n_maps *= gps
    q_w, k_w, v_w, out_w = (w * gps for w in widths)
    n_units = batch * n_groups * nq
    nxt = lambda s: jnp.minimum(s, n_units - 1)
    cur = lambda s: jnp.maximum(s - 1, 0)
    row_blk = lambda u: (u // (n_groups * nq)) * nq + u % nq
    grp = lambda u: (u // nq) % n_groups
    bat = lambda u: u // (n_groups * nq)
    k_map = lambda s: (bat(nxt(s)), grp(nxt(s)))
    v_map = lambda s: (bat(cur(s)), grp(cur(s)))
    in_specs = [pl.BlockSpec((tq, q_w), lambda s: (row_blk(nxt(s)), grp(nxt(s)))),
                pl.BlockSpec((n_q, k_w), k_map), pl.BlockSpec((n_q, v_w), v_map)]
    inputs = [q, *kv_new]
    if past:
        in_specs += [pl.BlockSpec((past, k_w), k_map), pl.BlockSpec((past, v_w), v_map)]
        inputs += list(kv_cache)
    score = pltpu.VMEM((n_maps, n_kv, tq), F32)
    cmax = pltpu.VMEM((n_maps, 1, tq), F32)
    return pl.pallas_call(
        functools.partial(kernel_fn, gps, nq, past),
        grid=(n_units + 1,),
        in_specs=in_specs + extra_specs(lambda s: grp(cur(s)), gps),
        out_specs=pl.BlockSpec((tq, out_w), lambda s: (row_blk(cur(s)), grp(cur(s)))),
        out_shape=jax.ShapeDtypeStruct((batch * n_q, n_groups * out_w), BF16),
        scratch_shapes=[score, cmax, score, cmax, pltpu.VMEM((v_w, n_kv), BF16)],
        compiler_params=_params(("arbitrary",)),
        name=name,
    )(*inputs, *extra_inputs)


def _attention_refs(past, refs, n_extra):
    q_ref, k_ref, v_ref = refs[:3]
    refs = refs[3:]
    kc_ref = vc_ref = None
    if past:
        kc_ref, vc_ref = refs[:2]
        refs = refs[2:]
    return (q_ref, k_ref, v_ref, kc_ref, vc_ref), refs[:n_extra], refs[n_extra:]


def _pipeline_prologue(nq, past, v_ref, vc_ref, s_b, m_b, vt_scr):
    s = pl.program_id(0)

    @pl.when(s == 0)
    def _():
        s_b[...] = jnp.zeros_like(s_b)
        m_b[...] = jnp.zeros_like(m_b)

    @pl.when(lax.rem(jnp.maximum(s - 1, 0), nq) == 0)
    def _():
        if past:
            vt_scr[:, 0:past] = vc_ref[...].T
        vt_scr[:, past:] = v_ref[...].T

    return lax.rem(s, 2)


def _key_chunks(n_kv, past):
    blocks = (n_kv - past) // LANE
    n = min(KEY_CHUNKS, blocks)
    edges = ([0] if past else []) + [past + (i * blocks // n) * LANE for i in range(n + 1)]
    return list(zip(edges[:-1], edges[1:]))


def _keys_of(k_ref, kc_ref, past, lanes):
    def k_of(c0, c1):
        return kc_ref[c0:c1, lanes] if c1 <= past else k_ref[c0 - past:c1 - past, lanes]
    return k_of


def _score_maps(maps, past, s_n, m_n, s_c, m_c):
    out = []
    for a, (k_of, q, vt_of) in enumerate(maps):
        mc = m_c[a]
        mx = l = acc = None
        for c0, c1 in _key_chunks(s_n.shape[1], past):
            st = _dot_nt(k_of(c0, c1), q)
            s_n[a, c0:c1, :] = st
            cm = jnp.max(st, axis=0, keepdims=True)
            mx = cm if mx is None else jnp.maximum(mx, cm)
            p = jnp.exp2(s_c[a, c0:c1, :] - mc)
            ps = jnp.sum(p, axis=0, keepdims=True)
            l = ps if l is None else l + ps
            pv = _dot(vt_of(c0, c1), p.astype(BF16))
            acc = pv if acc is None else acc + pv
        m_n[a] = mx
        out.append((acc, l))
    return out


def _mla_kernel(gps, nq, past, *refs):
    (q_ref, k_ref, v_ref, kc_ref, vc_ref), _, (o_ref, s_a, m_a, s_b, m_b, vt_scr) = _attention_refs(past, refs, 0)
    parity = _pipeline_prologue(nq, past, v_ref, vc_ref, s_b, m_b, vt_scr)

    def body(s_n, m_n, s_c, m_c):
        maps = []
        for e in range(2 * gps):
            lanes = slice(e * LANE, (e + 1) * LANE)
            maps.append((_keys_of(k_ref, kc_ref, past, lanes), q_ref[:, lanes],
                         lambda c0, c1, e=e: vt_scr[e * V_A:(e + 1) * V_A, c0:c1]))
        outs = [acc / l for acc, l in _score_maps(maps, past, s_n, m_n, s_c, m_c)]
        o_ref[...] = jnp.concatenate(outs, axis=0).T.astype(BF16)

    @pl.when(parity == 0)
    def _():
        body(s_a, m_a, s_b, m_b)

    @pl.when(parity == 1)
    def _():
        body(s_b, m_b, s_a, m_a)


def _mla_attention(q, kv_new, kv_cache, batch, n_q):
    return _pipelined_attention(_mla_kernel, q, kv_new, kv_cache, (), lambda cur_grp, gps: [], batch, n_q,
                                H_A // 2, (2 * LANE, 2 * LANE, LANE, LANE), 2, "mla_attention")


def _diff_kernel(lam_init, gps, nq, past, *refs):
    ((q_ref, k_ref, v_ref, kc_ref, vc_ref), (lam_ref, g_ref),
     (o_ref, s_a, m_a, s_b, m_b, vt_scr)) = _attention_refs(past, refs, 2)
    parity = _pipeline_prologue(nq, past, v_ref, vc_ref, s_b, m_b, vt_scr)

    def body(s_n, m_n, s_c, m_c):
        lane = lax.broadcasted_iota(jnp.int32, (1, LANE), 1)
        lv = lam_ref[...]
        lam = (jnp.exp(jnp.sum(lv[0:1] * lv[1:2], axis=-1, keepdims=True))
               - jnp.exp(jnp.sum(lv[2:3] * lv[3:4], axis=-1, keepdims=True)) + lam_init)
        maps = []
        for h in range(gps):
            lanes = slice(h * LANE, (h + 1) * LANE)
            q = q_ref[:, lanes]
            zero = jnp.zeros_like(q)
            k_of = _keys_of(k_ref, kc_ref, past, lanes)
            vt_of = lambda c0, c1, h=h: vt_scr[h * LANE:(h + 1) * LANE, c0:c1]
            maps += [(k_of, jnp.where(lane < DK_D, q, zero), vt_of), (k_of, jnp.where(lane >= DK_D, q, zero), vt_of)]
        res = _score_maps(maps, past, s_n, m_n, s_c, m_c)
        outs = []
        for h in range(gps):
            (acc1, l1), (acc2, l2) = res[2 * h], res[2 * h + 1]
            o = (acc1 / l1 - acc2 * (lam / l2)).T
            outs.append(_rms_rows(o, g_ref[:, h * LANE:(h + 1) * LANE]) * (1.0 - lam_init))
        o_ref[...] = (outs[0] if gps == 1 else jnp.concatenate(outs, axis=1)).astype(BF16)

    @pl.when(parity == 0)
    def _():
        body(s_a, m_a, s_b, m_b)

    @pl.when(parity == 1)
    def _():
        body(s_b, m_b, s_a, m_a)


def _diff_attention(q, kv_new, kv_cache, l, wts, lam_init, batch, n_q):
    extra = lambda cur_grp, gps: [pl.BlockSpec((None, 4, DK_D), lambda s: (l, 0, 0)),
                                  pl.BlockSpec((None, 1, gps * LANE), lambda s: (l, 0, cur_grp(s)))]
    return _pipelined_attention(functools.partial(_diff_kernel, lam_init), q, kv_new, kv_cache,
                                (wts["diff_lambda"], wts["diff_norm_g"]), extra, batch, n_q, H_D,
                                (LANE, LANE, LANE, LANE), 2, "diff_attention")


def _merge_kernel(final, x_ref, mod_ref, hf_ref, hb_ref, mo_ref, oa_ref, od_ref, gate_ref,
                  gm_ref, wbm_ref, wba_ref, wbd_ref, wout_ref, g2_ref, wff1_ref, wff2_ref, gfin_ref, o_ref):
    d = D_MODEL
    mod = mod_ref[...]
    hm = hf_ref[...] + hb_ref[...]
    gm = gm_ref[...]
    o_m = jnp.concatenate(
        [_rms_rows(hm[:, h * DV_M:(h + 1) * DV_M], gm[:, h * DV_M:(h + 1) * DV_M]) for h in range(H_M)], axis=1)
    o_m = (o_m * jax.nn.sigmoid(mo_ref[...])).astype(BF16)
    gate = jax.nn.sigmoid(gate_ref[...])
    y = (gate[:, 0:d] * _dot(o_m, wbm_ref[...]) + gate[:, d:2 * d] * _dot(oa_ref[...], wba_ref[...])
         + gate[:, 2 * d:3 * d] * _dot(od_ref[...], wbd_ref[...]))
    x = x_ref[...] + mod[:, 2 * d:3 * d] * _dot(y.astype(BF16), wout_ref[...])
    h2 = (_rms_rows(x, g2_ref[...]) * (1.0 + mod[:, 4 * d:5 * d]) + mod[:, 3 * d:4 * d]).astype(BF16)
    f = jnp.maximum(_dot(h2, wff1_ref[...]), 0.0)
    x = x + mod[:, 5 * d:6 * d] * _dot((f * f).astype(BF16), wff2_ref[...])
    if final:
        x = _rms_rows(x, gfin_ref[...])
    o_ref[...] = x


def _merge(x2, l, mod, wts, batch, n_tok, latent, m_in, h_f, h_b, o_a, o_d, gate, gfin):
    t = x2.shape[0]
    tm = ROW_TILE
    tpb = n_tok // tm
    row = lambda i: (i, 0)
    in_specs = [
        pl.BlockSpec((tm, D_MODEL), row),
        _mod_spec(l, 1, tpb) if latent else _mod_spec(l, 0, batch * tpb),
        pl.BlockSpec((tm, 512), row),
        pl.BlockSpec((tm, 512), row),
        pl.BlockSpec((tm, 512), lambda i: (i, C_MO // 512)),
        pl.BlockSpec((tm, 512), row),
        pl.BlockSpec((tm, 512), row),
        pl.BlockSpec((tm, G_COLS), row),
        _layer_spec(l, (1, 512)),
        _layer_spec(l, (512, D_MODEL)),
        _layer_spec(l, (512, D_MODEL)),
        _layer_spec(l, (512, D_MODEL)),
        _layer_spec(l, (D_MODEL, D_MODEL)),
        _layer_spec(l, (1, D_MODEL)),
        _layer_spec(l, (D_MODEL, D_FF)),
        _layer_spec(l, (D_FF, D_MODEL)),
        pl.BlockSpec((1, D_MODEL), lambda i: (0, 0)),
    ]
    return pl.pallas_call(
        functools.partial(_merge_kernel, l == DEPTH - 1),
        grid=(t // tm,),
        in_specs=in_specs,
        out_specs=pl.BlockSpec((tm, D_MODEL), row),
        out_shape=jax.ShapeDtypeStruct((t, D_MODEL), F32),
        compiler_params=_params(("parallel",)),
        name="merge_mlp",
    )(x2, mod, h_f, h_b, m_in, o_a, o_d, gate, wts["mlstm_norm_g"], wts["w_br_mlstm"], wts["w_br_mla"],
      wts["w_br_diff"], wts["w_out"], wts["norm2_g"], wts["w_ff1"], wts["w_ff2"], gfin)


_BLK_GI, _BLK_GF, _BLK_KR = C_GI // LANE, C_GF // LANE, (C_A + Q_RANK + KV_RANK) // LANE
_BLK_A, _BLK_D = C_A // LANE, C_D // LANE


def _repack_kernel(w_ref, o_ref):
    c = pl.program_id(1)
    lane = lax.broadcasted_iota(jnp.int32, (1, LANE), 1)
    lo = jnp.where(c == _BLK_KR, NOPE_A, 0)
    hi = jnp.where((c == _BLK_GI) | (c == _BLK_GF), 2 * H_M, jnp.where(c == _BLK_KR, NOPE_A + ROPE_A, LANE))
    o_ref[...] = jnp.where((lane >= lo) & (lane < hi), w_ref[0].T, 0.0).astype(BF16)


def _repack_w_in(w_in):
    depth, d, cols = w_in.shape
    c_mg = 2 * H_M * DK_M + 2 * H_M * DV_M
    c_acq = c_mg + 4 * H_M
    c_akr = c_acq + Q_RANK + KV_RANK
    c_dq = c_akr + ROPE_A

    def src(c):
        return jnp.where(c < _BLK_GI, c * LANE,
               jnp.where(c == _BLK_GI, c_mg,
               jnp.where(c == _BLK_GF, c_mg + 2 * H_M,
               jnp.where(c < _BLK_KR, c_acq + (c - _BLK_A) * LANE,
               jnp.where(c == _BLK_KR, c_akr - NOPE_A, c_dq + (c - _BLK_D) * LANE)))))

    return pl.pallas_call(
        _repack_kernel,
        grid=(depth, IN_COLS_P // LANE),
        in_specs=[pl.BlockSpec((pl.Element(1), pl.Element(LANE), pl.Element(d)),
                               lambda l, c: (l, pl.multiple_of(src(c), 8), 0))],
        out_specs=pl.BlockSpec((None, d, LANE), lambda l, c: (l, 0, c)),
        out_shape=jax.ShapeDtypeStruct((depth, d, IN_COLS_P), BF16),
        compiler_params=_params(("parallel", "parallel")),
        name="repack_w_in",
    )(jnp.swapaxes(w_in, 1, 2))


def _prep_weights(w_in, mlstm_gate_b, norm1_g, mlstm_norm_g, mla_q_norm_g, mla_w_q_up, mla_kv_norm_g,
                  mla_w_kv_up, diff_lambda, diff_norm_g, w_br_mlstm, w_br_mla, w_br_diff, w_out,
                  norm2_g, w_ff1, w_ff2):
    depth = w_in.shape[0]
    w_p = _repack_w_in(w_in)
    pad8 = lambda a: jnp.pad(a, ((0, 0), (0, LANE - 8)))[:, None, :]
    wq = mla_w_q_up.astype(BF16).reshape(depth, Q_RANK, H_A, NOPE_A + ROPE_A)
    wq = jnp.pad(wq, ((0, 0), (0, 0), (0, 0), (0, LANE - NOPE_A - ROPE_A))).reshape(depth, Q_RANK, H_A * LANE)
    wkv = mla_w_kv_up.astype(BF16).reshape(depth, KV_RANK, H_A, NOPE_A + V_A)
    wk = jnp.pad(wkv[..., :NOPE_A], ((0, 0), (0, 0), (0, 0), (0, LANE - NOPE_A))).reshape(depth, KV_RANK, H_A * LANE)
    wv = wkv[..., NOPE_A:].reshape(depth, KV_RANK, H_A * V_A)
    row = lambda a: a[:, None, :]
    return {
        "w_in": w_p,
        "gate_bi": pad8(mlstm_gate_b[:, :8]), "gate_bf": pad8(mlstm_gate_b[:, 8:]),
        "norm1_g": row(norm1_g), "norm2_g": row(norm2_g), "mlstm_norm_g": row(mlstm_norm_g),
        "mla_q_norm_g": row(mla_q_norm_g), "mla_kv_norm_g": row(mla_kv_norm_g),
        "wq": wq, "wk": wk, "wv": wv,
        "diff_lambda": diff_lambda, "diff_norm_g": row(diff_norm_g),
        "w_br_mlstm": w_br_mlstm.astype(BF16), "w_br_mla": w_br_mla.astype(BF16),
        "w_br_diff": w_br_diff.astype(BF16), "w_out": w_out.astype(BF16),
        "w_ff1": w_ff1.astype(BF16), "w_ff2": w_ff2.astype(BF16),
    }


def _rope_tables(n_tokens):
    rows = n_tokens // GRID_W
    row = jnp.repeat(jnp.arange(rows, dtype=F32), GRID_W)
    col = jnp.tile(jnp.arange(GRID_W, dtype=F32), rows)

    def cs(dim):
        quarter = dim // 4
        inv = ROPE_BASE ** (-jnp.arange(quarter, dtype=F32) / quarter)
        ang = jnp.concatenate([row[:, None] * inv, col[:, None] * inv], axis=-1)
        return jnp.cos(ang), jnp.sin(ang)

    ca, sa = cs(ROPE_A)
    cd, sd = cs(DK_D)
    one = lambda n: jnp.ones((n_tokens, n), F32)
    zero = lambda n: jnp.zeros((n_tokens, n), F32)
    return jnp.concatenate([
        one(64), ca, ca, one(32),
        zero(80), sa, zero(32),
        zero(64), -sa, zero(48),
        cd, cd, cd, cd,
        zero(32), sd, zero(32), sd,
        -sd, zero(32), -sd, zero(32)], axis=1)


def _state_mask():
    r = jnp.arange(256)[:, None] // DK_M
    c = jnp.arange(S_COLS)[None, :]
    diag = (c < H_M * DV_M) & (c // DV_M == r)
    return jnp.stack([(diag | (c == H_M * DV_M + d * H_M + r)) for d in range(2)]).astype(F32)


def _layer(x2, l, mod, wts, smask, gfin, batch, n_tok, cache=None, tables=None, ctx_stacks=None):
    latent = cache is not None
    lam_init = 0.8 - 0.6 * math.exp(-0.3 * l)
    outs = _inproj(x2, l, mod, wts, batch, n_tok, tables, ctx_stacks)
    m_in, q_a, k_a, v_a, q_d, k_d, v_d, gate = outs[:8]
    kv_a = kv_d = None
    if latent:
        ck_a, cv_a, ck_d, cv_d = _cache_kv(l, wts, cache, batch)
        kv_a, kv_d = (ck_a, cv_a), (ck_d, cv_d)
    h_f, h_b, c_t, n_t, m_t = _mlstm(m_in, l, wts, smask, batch, n_tok, cache["state"] if latent else None)
    o_a = _mla_attention(q_a, (k_a, v_a), kv_a, batch, n_tok)
    o_d = _diff_attention(q_d, (k_d, v_d), kv_d, l, wts, lam_init, batch, n_tok)
    x_new = _merge(x2, l, mod, wts, batch, n_tok, latent, m_in, h_f, h_b, o_a, o_d, gate, gfin)
    return x_new, tuple(outs[8:]), (c_t, n_t, m_t)


def _cache_kv_kernel(ckv_ref, kr_ref, kd_ref, vd_ref, wk_ref, wv_ref, ka_out, va_out, kd_out, vd_out):
    ckv = ckv_ref[...].astype(BF16)
    ka_out[...] = (_dot(ckv, wk_ref[...]) + jnp.concatenate([kr_ref[...]] * H_A, axis=1)).astype(BF16)
    va_out[...] = _dot(ckv, wv_ref[...]).astype(BF16)
    kd_out[...] = kd_ref[...].astype(BF16)
    vd_out[...] = vd_ref[...].astype(BF16)


def _cache_kv(l, wts, cache, batch):
    past = cache["past"]
    tm = ROW_TILE
    ppb = past // tm
    cached = lambda i: (i // ppb, l, i % ppb, 0)
    widths = (H_A * LANE, H_A * V_A, 512, 512)
    return pl.pallas_call(
        _cache_kv_kernel,
        grid=(batch * ppb,),
        in_specs=[pl.BlockSpec((None, None, tm, KV_RANK), cached),
                  pl.BlockSpec((None, None, tm, LANE), cached),
                  pl.BlockSpec((None, None, tm, 512), cached),
                  pl.BlockSpec((None, None, tm, 512), cached),
                  _layer_spec(l, (KV_RANK, H_A * LANE)), _layer_spec(l, (KV_RANK, H_A * V_A))],
        out_specs=[pl.BlockSpec((tm, w), lambda i: (i, 0)) for w in widths],
        out_shape=[jax.ShapeDtypeStruct((batch * past, w), BF16) for w in widths],
        compiler_params=_params(("parallel",)),
        name="cache_kv",
    )(cache["ckv"], cache["krope"], cache["diff_k"], cache["diff_v"], wts["wk"], wts["wv"])


def kernel(x_prompt, x_sample, c, cache_mla_ckv, cache_mla_krope, cache_diff_k, cache_diff_v, state_mlstm_C, state_mlstm_n, state_mlstm_m, c_ctx, w_mod, b_mod, norm1_g, w_in, mlstm_gate_b, mlstm_norm_g, mla_q_norm_g, mla_w_q_up, mla_kv_norm_g, mla_w_kv_up, diff_lambda, diff_norm_g, w_br_mlstm, w_br_mla, w_br_diff, w_out, norm2_g, w_ff1, w_ff2, final_norm_g):
    bp, sp, _ = x_prompt.shape
    bs, ss, _ = x_sample.shape
    past = cache_mla_ckv.shape[2]
    assert bs + 1 <= 8 and ss % GRID_W == 0
    assert sp % ROW_TILE == 0 and ss % ROW_TILE == 0 and past % ROW_TILE == 0

    cond8 = jnp.concatenate([c_ctx[None, :], c, jnp.zeros((8 - 1 - bs, D_MODEL), F32)], axis=0)
    mod = _modulation(cond8, w_mod, b_mod).reshape(DEPTH, 8, 1, 6 * D_MODEL)
    wts = _prep_weights(w_in, mlstm_gate_b, norm1_g, mlstm_norm_g, mla_q_norm_g, mla_w_q_up, mla_kv_norm_g,
                        mla_w_kv_up, diff_lambda, diff_norm_g, w_br_mlstm, w_br_mla, w_br_diff, w_out,
                        norm2_g, w_ff1, w_ff2)
    tables = _rope_tables(ss)
    smask = _state_mask()
    gfin = final_norm_g[None, :]
    rows = H_M * DK_M
    cache = {
        "past": past,
        "ckv": cache_mla_ckv,
        "krope": jnp.pad(cache_mla_krope, ((0, 0), (0, 0), (0, 0), (NOPE_A, LANE - NOPE_A - ROPE_A))),
        "diff_k": cache_diff_k.reshape(bs, DEPTH, past, H_D * 2 * DK_D),
        "diff_v": cache_diff_v.reshape(bs, DEPTH, past, H_D * DV_D),
        "state": (state_mlstm_C.reshape(bs, DEPTH, 2, rows, DV_M),
                  state_mlstm_n.reshape(bs, DEPTH, 2, rows, 1),
                  jnp.pad(state_mlstm_m.reshape(bs, DEPTH, 1, 2 * H_M), ((0, 0), (0, 0), (0, 0), (0, LANE - 2 * H_M)))),
    }

    y_p = x_prompt.reshape(bp * sp, D_MODEL)
    y_s = x_sample.reshape(bs * ss, D_MODEL)
    stacks = None
    states = []
    for l in range(DEPTH):
        y_p, stacks, state = _layer(y_p, l, mod, wts, smask, gfin, bp, sp, ctx_stacks=stacks)
        states.append(state)
        y_s, _, _ = _layer(y_s, l, mod, wts, smask, gfin, bs, ss, cache=cache, tables=tables)

    ckv, akr, kd, vd = stacks
    c_t = jnp.stack([s[0] for s in states], axis=1).reshape(bp, DEPTH, 2, H_M, DK_M, DV_M)
    n_t = jnp.stack([s[1] for s in states], axis=1).reshape(bp, DEPTH, 2, H_M, DK_M)
    m_t = jnp.stack([s[2][:, 0, :2 * H_M] for s in states], axis=1).reshape(bp, DEPTH, 2, H_M)
    return (y_p.reshape(bp, sp, D_MODEL), y_s.reshape(bs, ss, D_MODEL),
            ckv, akr[..., NOPE_A:NOPE_A + ROPE_A],
            kd.reshape(bp, DEPTH, sp, H_D, 2 * DK_D), vd.reshape(bp, DEPTH, sp, H_D, DV_D),
            c_t, n_t, m_t)
```

```python
import functools
import math

import jax
import jax.numpy as jnp
import numpy as np
from jax import lax
from jax.experimental import pallas as pl
from jax.experimental.pallas import tpu as pltpu

F32 = jnp.float32
BF16 = jnp.bfloat16

D_MODEL = 1024
DEPTH = 2
GRID_W = 64
ROPE_BASE = 10000.0
EPS = 1e-6
H_M, DK_M, DV_M = 4, 64, 128
H_A, Q_RANK, KV_RANK, NOPE_A, ROPE_A, V_A = 8, 384, 256, 64, 32, 64
H_D, DK_D, DV_D = 4, 64, 128
D_FF = 4 * D_MODEL

LANE = 128
VMEM_LIMIT = 56 * 1024 * 1024
ROW_TILE = 256
Q_TILE = 256
CHUNK = 128
KEY_CHUNKS = 4
SCORE_VMEM_BUDGET = 24 * 1024 * 1024
HEAD_GROUP = 4
MLSTM_STREAMS = 2
NEG = -1e30
LOG2E = 1.4426950408889634

C_MQ, C_MK, C_MV, C_MO, C_GI, C_GF = 0, 256, 512, 1024, 1536, 1664
M_COLS = 1792
C_A = 1792
A_COLS = 768
C_D = 2560
D_COLS = 1536
C_G = 4096
G_COLS = 3 * D_MODEL
IN_COLS_P = 7168
S_COLS = H_M * DV_M + LANE


def _dot(a, b):
    return jnp.dot(a, b, preferred_element_type=F32)


def _dot_nt(a, b):
    return lax.dot_general(a, b, (((1,), (1,)), ((), ())), preferred_element_type=F32)


def _dot_tn(a, b):
    return lax.dot_general(a, b, (((0,), (0,)), ((), ())), preferred_element_type=F32)


def _rms_rows(x, g):
    return x * lax.rsqrt(jnp.mean(x * x, axis=-1, keepdims=True) + EPS) * g


def _rope_blocks(x, cos, sa, sb, shift):
    outs = []
    for b in range(x.shape[1] // LANE):
        xb = x[:, b * LANE:(b + 1) * LANE]
        outs.append(xb * cos + pltpu.roll(xb, shift, 1) * sa + pltpu.roll(xb, LANE - shift, 1) * sb)
    return outs[0] if len(outs) == 1 else jnp.concatenate(outs, axis=1)


def _params(sem):
    return pltpu.CompilerParams(dimension_semantics=sem, vmem_limit_bytes=VMEM_LIMIT)


def _layer_spec(l, shape):
    nd = len(shape)
    return pl.BlockSpec((None,) + shape, lambda *_: (l,) + (0,) * nd, pipeline_mode=pl.Buffered(1))


def _mod_spec(l, first_row, tiles_per_cond):
    return pl.BlockSpec((None, None, 1, 6 * D_MODEL), lambda i: (l, first_row + i // tiles_per_cond, 0, 0))


_ANY = pl.BlockSpec(memory_space=pl.ANY)


def _mod_kernel(c_ref, w_ref, b_ref, o_ref):
    c = c_ref[...]
    s = (c * jax.nn.sigmoid(c)).astype(BF16)
    o_ref[...] = _dot(s, w_ref[...].astype(BF16)) + b_ref[...]


def _modulation(cond8, w_mod, b_mod):
    tn = 1536
    n6 = 6 * D_MODEL
    return pl.pallas_call(
        _mod_kernel,
        grid=(DEPTH, n6 // tn),
        in_specs=[
            pl.BlockSpec((8, D_MODEL), lambda l, j: (0, 0)),
            pl.BlockSpec((None, D_MODEL, tn), lambda l, j: (l, 0, j)),
            pl.BlockSpec((None, 1, tn), lambda l, j: (l, 0, j)),
        ],
        out_specs=pl.BlockSpec((None, 8, tn), lambda l, j: (l, 0, j)),
        out_shape=jax.ShapeDtypeStruct((DEPTH, 8, n6), F32),
        compiler_params=_params(("parallel", "parallel")),
        name="modulation",
    )(cond8, w_mod, b_mod.reshape(DEPTH, 1, n6))


def _store_slots(ref, val):
    if len(ref.shape) == 3:
        for s in range(ref.shape[0]):
            ref[s] = val
    else:
        ref[...] = val


def _inproj_kernel(latent, n_aliased, *refs):
    (x_ref, mod_ref, g1_ref, w_ref, gq_ref, gkv_ref, wq_ref, wk_ref, wv_ref) = refs[:9]
    refs = refs[9:]
    if latent:
        tab_ref = refs[0]
        (m_ref, q_ref, k_ref, va_ref, qd_ref, kd_ref, vd_ref, gate_ref) = refs[1:]
    else:
        (m_ref, q_ref, k_ref, va_ref, qd_ref, kd_ref, vd_ref, gate_ref,
         ckv_ref, akr_ref, kdraw_ref, vdraw_ref) = refs[n_aliased:]
    d = D_MODEL
    x = x_ref[...]
    mod = mod_ref[...]
    h = (_rms_rows(x, g1_ref[...]) * (1.0 + mod[:, d:2 * d]) + mod[:, 0:d]).astype(BF16)

    m_ref[...] = _dot(h, w_ref[:, 0:M_COLS])

    za = _dot(h, w_ref[:, C_A:C_A + A_COLS])
    acq = za[:, 0:Q_RANK]
    ackv = za[:, Q_RANK:Q_RANK + KV_RANK]
    akr = za[:, Q_RANK + KV_RANK:A_COLS]
    q = _dot(_rms_rows(acq, gq_ref[...]).astype(BF16), wq_ref[...])
    ckv = _rms_rows(ackv, gkv_ref[...])
    ckv_b = ckv.astype(BF16)
    kn = _dot(ckv_b, wk_ref[...])
    if not latent:
        _store_slots(ckv_ref, ckv)
        _store_slots(akr_ref, akr)
    else:
        tab = tab_ref[...]
        cq, saq, sbq = tab[:, 0:128], tab[:, 128:256], tab[:, 256:384]
        q = _rope_blocks(q, cq, saq, sbq, ROPE_A // 2)
        akr = _rope_blocks(akr, cq, saq, sbq, ROPE_A // 2)
    q_ref[...] = (q * (LOG2E * (NOPE_A + ROPE_A) ** -0.5)).astype(BF16)
    k_ref[...] = (kn + jnp.concatenate([akr] * H_A, axis=1)).astype(BF16)
    va_ref[...] = _dot(ckv_b, wv_ref[...]).astype(BF16)

    zd = _dot(h, w_ref[:, C_D:C_D + D_COLS])
    dq, dk, dv = zd[:, 0:512], zd[:, 512:1024], zd[:, 1024:1536]
    if not latent:
        _store_slots(kdraw_ref, dk)
        _store_slots(vdraw_ref, dv)
    else:
        cd, sad, sbd = tab[:, 384:512], tab[:, 512:640], tab[:, 640:768]
        dq = _rope_blocks(dq, cd, sad, sbd, DK_D // 2)
        dk = _rope_blocks(dk, cd, sad, sbd, DK_D // 2)
    qd_ref[...] = (dq * (LOG2E * DK_D ** -0.5)).astype(BF16)
    kd_ref[...] = dk.astype(BF16)
    vd_ref[...] = dv.astype(BF16)

    gate_ref[...] = _dot(h, w_ref[:, C_G:C_G + G_COLS])


def _inproj(x2, l, mod, wts, batch, n_tok, tables=None, ctx_stacks=None):
    latent = tables is not None
    t = batch * n_tok
    tm = ROW_TILE
    tpb = n_tok // tm
    row = lambda i: (i, 0)

    in_specs = [
        pl.BlockSpec((tm, D_MODEL), row),
        _mod_spec(l, 1, tpb) if latent else _mod_spec(l, 0, batch * tpb),
        _layer_spec(l, (1, D_MODEL)),
        _layer_spec(l, (D_MODEL, IN_COLS_P)),
        _layer_spec(l, (1, Q_RANK)),
        _layer_spec(l, (1, KV_RANK)),
        _layer_spec(l, (Q_RANK, H_A * LANE)),
        _layer_spec(l, (KV_RANK, H_A * LANE)),
        _layer_spec(l, (KV_RANK, H_A * V_A)),
    ]
    args = [x2, mod, wts["norm1_g"], wts["w_in"], wts["mla_q_norm_g"], wts["mla_kv_norm_g"],
            wts["wq"], wts["wk"], wts["wv"]]
    widths = [(M_COLS, F32), (H_A * LANE, BF16), (H_A * LANE, BF16), (H_A * V_A, BF16),
              (512, BF16), (512, BF16), (512, BF16), (G_COLS, F32)]
    out_specs = [pl.BlockSpec((tm, w), row) for w, _ in widths]
    out_shape = [jax.ShapeDtypeStruct((t, w), dt) for w, dt in widths]
    aliases = {}
    if latent:
        in_specs.append(pl.BlockSpec((tm, 6 * LANE), lambda i: (i % tpb, 0)))
        args.append(tables)
    else:
        if ctx_stacks is None:
            stack_spec = lambda w: pl.BlockSpec((None, DEPTH, tm, w), lambda i: (i // tpb, 0, i % tpb, 0))
        else:
            stack_spec = lambda w: pl.BlockSpec((None, None, tm, w), lambda i: (i // tpb, l, i % tpb, 0))
            aliases = {len(args) + n: 8 + n for n in range(4)}
            in_specs += [_ANY] * 4
            args += list(ctx_stacks)
        for w in (KV_RANK, LANE, 512, 512):
            out_specs.append(stack_spec(w))
            out_shape.append(jax.ShapeDtypeStruct((batch, DEPTH, n_tok, w), F32))
    return pl.pallas_call(
        functools.partial(_inproj_kernel, latent, len(aliases)),
        grid=(t // tm,),
        in_specs=in_specs,
        out_specs=out_specs,
        out_shape=out_shape,
        input_output_aliases=aliases,
        compiler_params=_params(("parallel",)),
        name="inproj_lat" if latent else "inproj_ctx",
    )(*args)


def _split3(x):
    hi = x.astype(BF16)
    r1 = x - hi.astype(F32)
    mid = r1.astype(BF16)
    return hi, mid, (r1 - mid.astype(F32)).astype(BF16)


def _mlstm_chunks(items, bi, bfg, smask_ref):
    L = CHUNK
    r_i = lax.broadcasted_iota(jnp.int32, (L, L), 0)
    c_i = lax.broadcasted_iota(jnp.int32, (L, L), 1)
    lane256 = lax.broadcasted_iota(jnp.int32, (L, 256), 1)
    lane128 = lax.broadcasted_iota(jnp.int32, (1, LANE), 1)
    masks = {0: c_i <= r_i, 1: c_i >= r_i}
    tris = {d: jnp.where(m, 1.0, 0.0).astype(BF16) for d, m in masks.items()}
    heads = [(n, hd) for n in range(len(items)) for hd in range(H_M)]

    pre = []
    for z_of, s_of, _, m_prev, d, _ in items:
        q = (z_of(C_MQ, 256) * DK_M ** -0.5).astype(BF16)
        gi = z_of(C_GI, LANE) + bi
        xf = z_of(C_GF, LANE) + bfg
        lf = jnp.minimum(xf, 0.0) - jnp.log(1.0 + jnp.exp(-jnp.abs(xf)))
        b = sum(_dot(tris[d], part) for part in _split3(lf))
        g = gi - b
        pre.append(dict(q=q, kb=z_of(C_MK, 256).astype(BF16), g_t=g.T, b=b, g=g, last=L - 1 if d == 0 else 0))

    blk = lambda x, hd: x[:, hd * LANE:(hd + 1) * LANE]
    rep = lambda x, j: jnp.broadcast_to(x[:, j:j + 1], (L, LANE))
    u, av = {}, {}
    for g0 in range(0, len(heads), HEAD_GROUP):
        group = heads[g0:g0 + HEAD_GROUP]
        gm = {}
        for n, hd in group:
            d = items[n][4]
            j = d * H_M + hd
            gm[n, hd] = jnp.where(masks[d], pre[n]["g_t"][j:j + 1, :], NEG)
            row_max = jnp.broadcast_to(jnp.max(gm[n, hd], axis=-1, keepdims=True), (L, LANE))
            u[n, hd] = jnp.maximum(row_max, items[n][3][:, j:j + 1])
        a = {}
        for n, hd in group:
            q = pre[n]["q"]
            q_h = jnp.where((lane256 >= hd * DK_M) & (lane256 < (hd + 1) * DK_M), q, jnp.zeros_like(q))
            a[n, hd] = jnp.exp(gm[n, hd] - u[n, hd]) * _dot_nt(q_h, pre[n]["kb"])
        for n, hd in group:
            v_h = items[n][0](C_MV + hd * DV_M, DV_M).astype(BF16)
            av[n, hd] = _dot(a[n, hd].astype(BF16), jnp.concatenate([v_h, jnp.ones((L, LANE), BF16)], axis=1))

    lane_l = lax.broadcasted_iota(jnp.int32, (L, LANE), 1)
    results = []
    for n, (z_of, s_of, s_set, m_prev, d, h_set) in enumerate(items):
        p = pre[n]
        s_prev = s_of()
        p["qs"] = _dot(p["q"], s_prev.astype(BF16))
        w_rep, w_old = [], []
        m_new = m_prev
        for hd in range(H_M):
            j = d * H_M + hd
            uh = u[n, hd]
            mp = m_prev[:, j:j + 1]
            b_h = rep(p["b"], j)
            w_int = jnp.exp(mp - uh)
            num = w_int * blk(p["qs"], hd) + av[n, hd][:, :DV_M]
            den = w_int * rep(p["qs"], H_M * DV_M + j) + av[n, hd][:, DV_M:]
            h_set(hd, num / jnp.maximum(jnp.abs(den), jnp.exp(-(b_h + uh))))
            u_l = uh[p["last"]:p["last"] + 1, :]
            m_new = jnp.where(lane128 == j, b_h[p["last"]:p["last"] + 1, :] + u_l, m_new)
            w_rep.append(jnp.exp(rep(p["g"], j) - u_l))
            w_old.append(jnp.exp(mp - u_l))
        w_exp = jnp.concatenate([jnp.where(lane_l < DK_M, w_rep[0], w_rep[1]),
                                 jnp.where(lane_l < DK_M, w_rep[2], w_rep[3])], axis=1)
        k_w = (z_of(C_MK, 256) * w_exp).astype(BF16)
        v_aug = jnp.concatenate([z_of(C_MV, H_M * DV_M).astype(BF16), jnp.ones((L, LANE), BF16)], axis=1)
        upd = _dot_tn(k_w, v_aug)
        n_scale = jnp.zeros((1, LANE), F32)
        for hd in range(H_M):
            n_scale = jnp.where(lane128 == d * H_M + hd, w_old[hd], n_scale)
        scale_row = jnp.concatenate(w_old + [n_scale], axis=1)
        s_set(scale_row * s_prev + smask_ref[d] * upd)
        results.append(m_new)
    return results


def _mlstm_kernel(has_state, *refs):
    zf_ref, zb_ref, bi_ref, bf_ref, smask_ref = refs[:5]
    refs = refs[5:]
    if has_state:
        c0_ref, n0_ref, m0_ref = refs[:3]
        refs = refs[3:]
    hf_ref, hb_ref, ct_ref, nt_ref, mt_ref, s_scr, m_scr = refs
    i = pl.program_id(1)
    hv = H_M * DV_M

    streams = s_scr.shape[0]

    @pl.when(i == 0)
    def _():
        if has_state:
            for g in range(streams):
                for d in range(2):
                    tiled = jnp.concatenate(
                        [c0_ref[g, d]] * H_M + [jnp.broadcast_to(n0_ref[g, d], (256, LANE))], axis=1)
                    s_scr[g, d] = smask_ref[d] * tiled
                    m_scr[g, d] = m0_ref[g]
        else:
            s_scr[...] = jnp.zeros_like(s_scr)
            m_scr[...] = jnp.zeros_like(m_scr)

    bi = bi_ref[...]
    bfg = bf_ref[...]
    slots = [(g, d) for g in range(streams) for d in range(2)]

    def item(g, d):
        z_ref, h_ref = (zf_ref, hf_ref) if d == 0 else (zb_ref, hb_ref)

        def h_set(hd, val):
            h_ref[g, :, hd * DV_M:(hd + 1) * DV_M] = val

        def s_set(val):
            s_scr[g, d] = val

        return (lambda c, w: z_ref[g, :, c:c + w], lambda: s_scr[g, d], s_set, m_scr[g, d], d, h_set)

    for (g, d), m_new in zip(slots, _mlstm_chunks([item(g, d) for g, d in slots], bi, bfg, smask_ref)):
        m_scr[g, d] = m_new

    @pl.when(i == pl.num_programs(1) - 1)
    def _():
        lane = lax.broadcasted_iota(jnp.int32, (1, LANE), 1)
        for g in range(streams):
            for d in range(2):
                s = s_scr[g, d]
                ct_ref[g, d] = ((s[:, 0:DV_M] + s[:, DV_M:2 * DV_M])
                                + (s[:, 2 * DV_M:3 * DV_M] + s[:, 3 * DV_M:hv]))
                nt_ref[g, d] = jnp.sum(s[:, hv:], axis=-1, keepdims=True)
            mt_ref[g] = jnp.where(lane < H_M, m_scr[g, 0], m_scr[g, 1])


def _mlstm(m_in, l, wts, smask, batch, n_tok, state=None):
    nc = n_tok // CHUNK
    hv = H_M * DV_M
    g = math.gcd(batch, MLSTM_STREAMS)
    z3 = m_in.reshape(batch, n_tok, M_COLS)
    fwd = lambda b, i: (b, i, 0)
    bwd = lambda b, i: (b, nc - 1 - i, 0)
    in_specs = [
        pl.BlockSpec((g, CHUNK, M_COLS), fwd),
        pl.BlockSpec((g, CHUNK, M_COLS), bwd),
        pl.BlockSpec((None, 1, LANE), lambda b, i: (l, 0, 0)),
        pl.BlockSpec((None, 1, LANE), lambda b, i: (l, 0, 0)),
        pl.BlockSpec((2, 256, S_COLS), lambda b, i: (0, 0, 0)),
    ]
    args = [z3, z3, wts["gate_bi"], wts["gate_bf"], smask]
    if state is not None:
        in_specs += [
            pl.BlockSpec((g, None, 2, 256, DV_M), lambda b, i: (b, l, 0, 0, 0)),
            pl.BlockSpec((g, None, 2, 256, 1), lambda b, i: (b, l, 0, 0, 0)),
            pl.BlockSpec((g, None, 1, LANE), lambda b, i: (b, l, 0, 0)),
        ]
        args += list(state)
    h_f, h_b, c_t, n_t, m_t = pl.pallas_call(
        functools.partial(_mlstm_kernel, state is not None),
        grid=(batch // g, nc),
        in_specs=in_specs,
        out_specs=[
            pl.BlockSpec((g, CHUNK, hv), fwd),
            pl.BlockSpec((g, CHUNK, hv), bwd),
            pl.BlockSpec((g, 2, 256, DV_M), lambda b, i: (b, 0, 0, 0)),
            pl.BlockSpec((g, 2, 256, 1), lambda b, i: (b, 0, 0, 0)),
            pl.BlockSpec((g, 1, LANE), lambda b, i: (b, 0, 0)),
        ],
        out_shape=[
            jax.ShapeDtypeStruct((batch, n_tok, hv), F32),
            jax.ShapeDtypeStruct((batch, n_tok, hv), F32),
            jax.ShapeDtypeStruct((batch, 2, 256, DV_M), F32),
            jax.ShapeDtypeStruct((batch, 2, 256, 1), F32),
            jax.ShapeDtypeStruct((batch, 1, LANE), F32),
        ],
        scratch_shapes=[pltpu.VMEM((g, 2, 256, S_COLS), F32), pltpu.VMEM((g, 2, 1, LANE), F32)],
        compiler_params=_params(("parallel", "arbitrary")),
        name="mlstm",
    )(*args)
    return h_f.reshape(batch * n_tok, hv), h_b.reshape(batch * n_tok, hv), c_t, n_t, m_t


def _groups_per_step(n_groups, n_maps, n_kv, tq):
    return n_groups if 2 * n_groups * n_maps * n_kv * tq * 4 <= SCORE_VMEM_BUDGET else 1


def _pipelined_attention(kernel_fn, q, kv_new, kv_cache, extra_inputs, extra_specs, batch, n_q, n_groups,
                         widths, n_maps, name):
    past = 0 if kv_cache is None else kv_cache[0].shape[0] // batch
    n_kv = past + n_q
    tq = min(Q_TILE, n_q)
    nq = n_q // tq
    gps = _groups_per_step(n_groups, n_maps, n_kv, tq)
    n_groups //= gps
    n_maps *= gps
    q_w, k_w, v_w, out_w = (w * gps for w in widths)
    n_units = batch * n_groups * nq
    nxt = lambda s: jnp.minimum(s, n_units - 1)
    cur = lambda s: jnp.maximum(s - 1, 0)
    row_blk = lambda u: (u // (n_groups * nq)) * nq + u % nq
    grp = lambda u: (u // nq) % n_groups
    bat = lambda u: u // (n_groups * nq)
    k_map = lambda s: (bat(nxt(s)), grp(nxt(s)))
    v_map = lambda s: (bat(cur(s)), grp(cur(s)))
    in_specs = [pl.BlockSpec((tq, q_w), lambda s: (row_blk(nxt(s)), grp(nxt(s)))),
                pl.BlockSpec((n_q, k_w), k_map), pl.BlockSpec((n_q, v_w), v_map)]
    inputs = [q, *kv_new]
    if past:
        in_specs += [pl.BlockSpec((past, k_w), k_map), pl.BlockSpec((past, v_w), v_map)]
        inputs += list(kv_cache)
    score = pltpu.VMEM((n_maps, n_kv, tq), F32)
    cmax = pltpu.VMEM((n_maps, 1, tq), F32)
    return pl.pallas_call(
        functools.partial(kernel_fn, gps, nq, past),
        grid=(n_units + 1,),
        in_specs=in_specs + extra_specs(lambda s: grp(cur(s)), gps),
        out_specs=pl.BlockSpec((tq, out_w), lambda s: (row_blk(cur(s)), grp(cur(s)))),
        out_shape=jax.ShapeDtypeStruct((batch * n_q, n_groups * out_w), BF16),
        scratch_shapes=[score, cmax, score, cmax, pltpu.VMEM((v_w, n_kv), BF16)],
        compiler_params=_params(("arbitrary",)),
        name=name,
    )(*inputs, *extra_inputs)


def _attention_refs(past, refs, n_extra):
    q_ref, k_ref, v_ref = refs[:3]
    refs = refs[3:]
    kc_ref = vc_ref = None
    if past:
        kc_ref, vc_ref = refs[:2]
        refs = refs[2:]
    return (q_ref, k_ref, v_ref, kc_ref, vc_ref), refs[:n_extra], refs[n_extra:]


def _pipeline_prologue(nq, past, v_ref, vc_ref, s_b, m_b, vt_scr):
    s = pl.program_id(0)

    @pl.when(s == 0)
    def _():
        s_b[...] = jnp.zeros_like(s_b)
        m_b[...] = jnp.zeros_like(m_b)

    @pl.when(lax.rem(jnp.maximum(s - 1, 0), nq) == 0)
    def _():
        if past:
            vt_scr[:, 0:past] = vc_ref[...].T
        vt_scr[:, past:] = v_ref[...].T

    return lax.rem(s, 2)


def _key_chunks(n_kv, past):
    blocks = (n_kv - past) // LANE
    n = min(KEY_CHUNKS, blocks)
    edges = ([0] if past else []) + [past + (i * blocks // n) * LANE for i in range(n + 1)]
    return list(zip(edges[:-1], edges[1:]))


def _keys_of(k_ref, kc_ref, past, lanes):
    def k_of(c0, c1):
        return kc_ref[c0:c1, lanes] if c1 <= past else k_ref[c0 - past:c1 - past, lanes]
    return k_of


def _score_maps(maps, past, s_n, m_n, s_c, m_c):
    out = []
    for a, (k_of, q, vt_of) in enumerate(maps):
        mc = m_c[a]
        mx = l = acc = None
        for c0, c1 in _key_chunks(s_n.shape[1], past):
            st = _dot_nt(k_of(c0, c1), q)
            s_n[a, c0:c1, :] = st
            cm = jnp.max(st, axis=0, keepdims=True)
            mx = cm if mx is None else jnp.maximum(mx, cm)
            p = jnp.exp2(s_c[a, c0:c1, :] - mc)
            ps = jnp.sum(p, axis=0, keepdims=True)
            l = ps if l is None else l + ps
            pv = _dot(vt_of(c0, c1), p.astype(BF16))
            acc = pv if acc is None else acc + pv
        m_n[a] = mx
        out.append((acc, l))
    return out


def _mla_kernel(gps, nq, past, *refs):
    (q_ref, k_ref, v_ref, kc_ref, vc_ref), _, (o_ref, s_a, m_a, s_b, m_b, vt_scr) = _attention_refs(past, refs, 0)
    parity = _pipeline_prologue(nq, past, v_ref, vc_ref, s_b, m_b, vt_scr)

    def body(s_n, m_n, s_c, m_c):
        maps = []
        for e in range(2 * gps):
            lanes = slice(e * LANE, (e + 1) * LANE)
            maps.append((_keys_of(k_ref, kc_ref, past, lanes), q_ref[:, lanes],
                         lambda c0, c1, e=e: vt_scr[e * V_A:(e + 1) * V_A, c0:c1]))
        outs = [acc / l for acc, l in _score_maps(maps, past, s_n, m_n, s_c, m_c)]
        o_ref[...] = jnp.concatenate(outs, axis=0).T.astype(BF16)

    @pl.when(parity == 0)
    def _():
        body(s_a, m_a, s_b, m_b)

    @pl.when(parity == 1)
    def _():
        body(s_b, m_b, s_a, m_a)


def _mla_attention(q, kv_new, kv_cache, batch, n_q):
    return _pipelined_attention(_mla_kernel, q, kv_new, kv_cache, (), lambda cur_grp, gps: [], batch, n_q,
                                H_A // 2, (2 * LANE, 2 * LANE, LANE, LANE), 2, "mla_attention")


def _diff_kernel(lam_init, gps, nq, past, *refs):
    ((q_ref, k_ref, v_ref, kc_ref, vc_ref), (lam_ref, g_ref),
     (o_ref, s_a, m_a, s_b, m_b, vt_scr)) = _attention_refs(past, refs, 2)
    parity = _pipeline_prologue(nq, past, v_ref, vc_ref, s_b, m_b, vt_scr)

    def body(s_n, m_n, s_c, m_c):
        lane = lax.broadcasted_iota(jnp.int32, (1, LANE), 1)
        lv = lam_ref[...]
        lam = (jnp.exp(jnp.sum(lv[0:1] * lv[1:2], axis=-1, keepdims=True))
               - jnp.exp(jnp.sum(lv[2:3] * lv[3:4], axis=-1, keepdims=True)) + lam_init)
        maps = []
        for h in range(gps):
            lanes = slice(h * LANE, (h + 1) * LANE)
            q = q_ref[:, lanes]
            zero = jnp.zeros_like(q)
            k_of = _keys_of(k_ref, kc_ref, past, lanes)
            vt_of = lambda c0, c1, h=h: vt_scr[h * LANE:(h + 1) * LANE, c0:c1]
            maps += [(k_of, jnp.where(lane < DK_D, q, zero), vt_of), (k_of, jnp.where(lane >= DK_D, q, zero), vt_of)]
        res = _score_maps(maps, past, s_n, m_n, s_c, m_c)
        outs = []
        for h in range(gps):
            (acc1, l1), (acc2, l2) = res[2 * h], res[2 * h + 1]
            o = (acc1 / l1 - acc2 * (lam / l2)).T
            outs.append(_rms_rows(o, g_ref[:, h * LANE:(h + 1) * LANE]) * (1.0 - lam_init))
        o_ref[...] = (outs[0] if gps == 1 else jnp.concatenate(outs, axis=1)).astype(BF16)

    @pl.when(parity == 0)
    def _():
        body(s_a, m_a, s_b, m_b)

    @pl.when(parity == 1)
    def _():
        body(s_b, m_b, s_a, m_a)


def _diff_attention(q, kv_new, kv_cache, l, wts, lam_init, batch, n_q):
    extra = lambda cur_grp, gps: [pl.BlockSpec((None, 4, DK_D), lambda s: (l, 0, 0)),
                                  pl.BlockSpec((None, 1, gps * LANE), lambda s: (l, 0, cur_grp(s)))]
    return _pipelined_attention(functools.partial(_diff_kernel, lam_init), q, kv_new, kv_cache,
                                (wts["diff_lambda"], wts["diff_norm_g"]), extra, batch, n_q, H_D,
                                (LANE, LANE, LANE, LANE), 2, "diff_attention")


def _merge_kernel(final, x_ref, mod_ref, hf_ref, hb_ref, mo_ref, oa_ref, od_ref, gate_ref,
                  gm_ref, wbm_ref, wba_ref, wbd_ref, wout_ref, g2_ref, wff1_ref, wff2_ref, gfin_ref, o_ref):
    d = D_MODEL
    mod = mod_ref[...]
    hm = hf_ref[...] + hb_ref[...]
    gm = gm_ref[...]
    o_m = jnp.concatenate(
        [_rms_rows(hm[:, h * DV_M:(h + 1) * DV_M], gm[:, h * DV_M:(h + 1) * DV_M]) for h in range(H_M)], axis=1)
    o_m = (o_m * jax.nn.sigmoid(mo_ref[...])).astype(BF16)
    gate = jax.nn.sigmoid(gate_ref[...])
    y = (gate[:, 0:d] * _dot(o_m, wbm_ref[...]) + gate[:, d:2 * d] * _dot(oa_ref[...], wba_ref[...])
         + gate[:, 2 * d:3 * d] * _dot(od_ref[...], wbd_ref[...]))
    x = x_ref[...] + mod[:, 2 * d:3 * d] * _dot(y.astype(BF16), wout_ref[...])
    h2 = (_rms_rows(x, g2_ref[...]) * (1.0 + mod[:, 4 * d:5 * d]) + mod[:, 3 * d:4 * d]).astype(BF16)
    f = jnp.maximum(_dot(h2, wff1_ref[...]), 0.0)
    x = x + mod[:, 5 * d:6 * d] * _dot((f * f).astype(BF16), wff2_ref[...])
    if final:
        x = _rms_rows(x, gfin_ref[...])
    o_ref[...] = x


def _merge(x2, l, mod, wts, batch, n_tok, latent, m_in, h_f, h_b, o_a, o_d, gate, gfin):
    t = x2.shape[0]
    tm = ROW_TILE
    tpb = n_tok // tm
    row = lambda i: (i, 0)
    in_specs = [
        pl.BlockSpec((tm, D_MODEL), row),
        _mod_spec(l, 1, tpb) if latent else _mod_spec(l, 0, batch * tpb),
        pl.BlockSpec((tm, 512), row),
        pl.BlockSpec((tm, 512), row),
        pl.BlockSpec((tm, 512), lambda i: (i, C_MO // 512)),
        pl.BlockSpec((tm, 512), row),
        pl.BlockSpec((tm, 512), row),
        pl.BlockSpec((tm, G_COLS), row),
        _layer_spec(l, (1, 512)),
        _layer_spec(l, (512, D_MODEL)),
        _layer_spec(l, (512, D_MODEL)),
        _layer_spec(l, (512, D_MODEL)),
        _layer_spec(l, (D_MODEL, D_MODEL)),
        _layer_spec(l, (1, D_MODEL)),
        _layer_spec(l, (D_MODEL, D_FF)),
        _layer_spec(l, (D_FF, D_MODEL)),
        pl.BlockSpec((1, D_MODEL), lambda i: (0, 0)),
    ]
    return pl.pallas_call(
        functools.partial(_merge_kernel, l == DEPTH - 1),
        grid=(t // tm,),
        in_specs=in_specs,
        out_specs=pl.BlockSpec((tm, D_MODEL), row),
        out_shape=jax.ShapeDtypeStruct((t, D_MODEL), F32),
        compiler_params=_params(("parallel",)),
        name="merge_mlp",
    )(x2, mod, h_f, h_b, m_in, o_a, o_d, gate, wts["mlstm_norm_g"], wts["w_br_mlstm"], wts["w_br_mla"],
      wts["w_br_diff"], wts["w_out"], wts["norm2_g"], wts["w_ff1"], wts["w_ff2"], gfin)


_BLK_GI, _BLK_GF, _BLK_KR = C_GI // LANE, C_GF // LANE, (C_A + Q_RANK + KV_RANK) // LANE
_BLK_A, _BLK_D = C_A // LANE, C_D // LANE


REPACK_BLOCKS = 8


def _repack_kernel(*refs):
    o_ref = refs[-1]
    lane = lax.broadcasted_iota(jnp.int32, (1, LANE), 1)
    for k, w_ref in enumerate(refs[:-1]):
        c = pl.program_id(1) * REPACK_BLOCKS + k
        lo = jnp.where(c == _BLK_KR, NOPE_A, 0)
        hi = jnp.where((c == _BLK_GI) | (c == _BLK_GF), 2 * H_M, jnp.where(c == _BLK_KR, NOPE_A + ROPE_A, LANE))
        o_ref[:, k * LANE:(k + 1) * LANE] = jnp.where((lane >= lo) & (lane < hi), w_ref[0].T, 0.0).astype(BF16)


def _repack_w_in(w_in):
    depth, d, cols = w_in.shape
    c_mg = 2 * H_M * DK_M + 2 * H_M * DV_M
    c_acq = c_mg + 4 * H_M
    c_akr = c_acq + Q_RANK + KV_RANK
    c_dq = c_akr + ROPE_A

    def src(c):
        return jnp.where(c < _BLK_GI, c * LANE,
               jnp.where(c == _BLK_GI, c_mg,
               jnp.where(c == _BLK_GF, c_mg + 2 * H_M,
               jnp.where(c < _BLK_KR, c_acq + (c - _BLK_A) * LANE,
               jnp.where(c == _BLK_KR, c_akr - NOPE_A, c_dq + (c - _BLK_D) * LANE)))))

    nb = REPACK_BLOCKS
    window = lambda k: pl.BlockSpec((pl.Element(1), pl.Element(LANE), pl.Element(d)),
                                    lambda l, s: (l, pl.multiple_of(src(s * nb + k), 8), 0))
    w_t = jnp.swapaxes(w_in, 1, 2)
    return pl.pallas_call(
        _repack_kernel,
        grid=(depth, IN_COLS_P // (nb * LANE)),
        in_specs=[window(k) for k in range(nb)],
        out_specs=pl.BlockSpec((None, d, nb * LANE), lambda l, s: (l, 0, s)),
        out_shape=jax.ShapeDtypeStruct((depth, d, IN_COLS_P), BF16),
        compiler_params=_params(("parallel", "parallel")),
        name="repack_w_in",
    )(*([w_t] * nb))


def _prep_weights(w_in, mlstm_gate_b, norm1_g, mlstm_norm_g, mla_q_norm_g, mla_w_q_up, mla_kv_norm_g,
                  mla_w_kv_up, diff_lambda, diff_norm_g, w_br_mlstm, w_br_mla, w_br_diff, w_out,
                  norm2_g, w_ff1, w_ff2):
    depth = w_in.shape[0]
    w_p = _repack_w_in(w_in)
    pad8 = lambda a: jnp.pad(a, ((0, 0), (0, LANE - 8)))[:, None, :]
    wq = mla_w_q_up.astype(BF16).reshape(depth, Q_RANK, H_A, NOPE_A + ROPE_A)
    wq = jnp.pad(wq, ((0, 0), (0, 0), (0, 0), (0, LANE - NOPE_A - ROPE_A))).reshape(depth, Q_RANK, H_A * LANE)
    wkv = mla_w_kv_up.astype(BF16).reshape(depth, KV_RANK, H_A, NOPE_A + V_A)
    wk = jnp.pad(wkv[..., :NOPE_A], ((0, 0), (0, 0), (0, 0), (0, LANE - NOPE_A))).reshape(depth, KV_RANK, H_A * LANE)
    wv = wkv[..., NOPE_A:].reshape(depth, KV_RANK, H_A * V_A)
    row = lambda a: a[:, None, :]
    return {
        "w_in": w_p,
        "gate_bi": pad8(mlstm_gate_b[:, :8]), "gate_bf": pad8(mlstm_gate_b[:, 8:]),
        "norm1_g": row(norm1_g), "norm2_g": row(norm2_g), "mlstm_norm_g": row(mlstm_norm_g),
        "mla_q_norm_g": row(mla_q_norm_g), "mla_kv_norm_g": row(mla_kv_norm_g),
        "wq": wq, "wk": wk, "wv": wv,
        "diff_lambda": diff_lambda, "diff_norm_g": row(diff_norm_g),
        "w_br_mlstm": w_br_mlstm.astype(BF16), "w_br_mla": w_br_mla.astype(BF16),
        "w_br_diff": w_br_diff.astype(BF16), "w_out": w_out.astype(BF16),
        "w_ff1": w_ff1.astype(BF16), "w_ff2": w_ff2.astype(BF16),
    }


def _rope_tables(n_tokens):
    rows = n_tokens // GRID_W
    row = jnp.repeat(jnp.arange(rows, dtype=F32), GRID_W)
    col = jnp.tile(jnp.arange(GRID_W, dtype=F32), rows)

    def cs(dim):
        quarter = dim // 4
        inv = ROPE_BASE ** (-jnp.arange(quarter, dtype=F32) / quarter)
        ang = jnp.concatenate([row[:, None] * inv, col[:, None] * inv], axis=-1)
        return jnp.cos(ang), jnp.sin(ang)

    ca, sa = cs(ROPE_A)
    cd, sd = cs(DK_D)
    one = lambda n: jnp.ones((n_tokens, n), F32)
    zero = lambda n: jnp.zeros((n_tokens, n), F32)
    return jnp.concatenate([
        one(64), ca, ca, one(32),
        zero(80), sa, zero(32),
        zero(64), -sa, zero(48),
        cd, cd, cd, cd,
        zero(32), sd, zero(32), sd,
        -sd, zero(32), -sd, zero(32)], axis=1)


def _state_mask():
    r = jnp.arange(256)[:, None] // DK_M
    c = jnp.arange(S_COLS)[None, :]
    diag = (c < H_M * DV_M) & (c // DV_M == r)
    return jnp.stack([(diag | (c == H_M * DV_M + d * H_M + r)) for d in range(2)]).astype(F32)


def _layer(x2, l, mod, wts, smask, gfin, batch, n_tok, cache=None, tables=None, ctx_stacks=None):
    latent = cache is not None
    lam_init = 0.8 - 0.6 * math.exp(-0.3 * l)
    outs = _inproj(x2, l, mod, wts, batch, n_tok, tables, ctx_stacks)
    m_in, q_a, k_a, v_a, q_d, k_d, v_d, gate = outs[:8]
    kv_a = kv_d = None
    if latent:
        ck_a, cv_a, ck_d, cv_d = _cache_kv(l, wts, cache, batch)
        kv_a, kv_d = (ck_a, cv_a), (ck_d, cv_d)
    h_f, h_b, c_t, n_t, m_t = _mlstm(m_in, l, wts, smask, batch, n_tok, cache["state"] if latent else None)
    o_a = _mla_attention(q_a, (k_a, v_a), kv_a, batch, n_tok)
    o_d = _diff_attention(q_d, (k_d, v_d), kv_d, l, wts, lam_init, batch, n_tok)
    x_new = _merge(x2, l, mod, wts, batch, n_tok, latent, m_in, h_f, h_b, o_a, o_d, gate, gfin)
    return x_new, tuple(outs[8:]), (c_t, n_t, m_t)


def _cache_kv_kernel(ckv_ref, kr_ref, kd_ref, vd_ref, wk_ref, wv_ref, ka_out, va_out, kd_out, vd_out):
    ckv = ckv_ref[...].astype(BF16)
    ka_out[...] = (_dot(ckv, wk_ref[...]) + jnp.concatenate([kr_ref[...]] * H_A, axis=1)).astype(BF16)
    va_out[...] = _dot(ckv, wv_ref[...]).astype(BF16)
    kd_out[...] = kd_ref[...].astype(BF16)
    vd_out[...] = vd_ref[...].astype(BF16)


def _cache_kv(l, wts, cache, batch):
    past = cache["past"]
    tm = ROW_TILE
    ppb = past // tm
    cached = lambda i: (i // ppb, l, i % ppb, 0)
    widths = (H_A * LANE, H_A * V_A, 512, 512)
    return pl.pallas_call(
        _cache_kv_kernel,
        grid=(batch * ppb,),
        in_specs=[pl.BlockSpec((None, None, tm, KV_RANK), cached),
                  pl.BlockSpec((None, None, tm, LANE), cached),
                  pl.BlockSpec((None, None, tm, 512), cached),
                  pl.BlockSpec((None, None, tm, 512), cached),
                  _layer_spec(l, (KV_RANK, H_A * LANE)), _layer_spec(l, (KV_RANK, H_A * V_A))],
        out_specs=[pl.BlockSpec((tm, w), lambda i: (i, 0)) for w in widths],
        out_shape=[jax.ShapeDtypeStruct((batch * past, w), BF16) for w in widths],
        compiler_params=_params(("parallel",)),
        name="cache_kv",
    )(cache["ckv"], cache["krope"], cache["diff_k"], cache["diff_v"], wts["wk"], wts["wv"])


def kernel(x_prompt, x_sample, c, cache_mla_ckv, cache_mla_krope, cache_diff_k, cache_diff_v, state_mlstm_C, state_mlstm_n, state_mlstm_m, c_ctx, w_mod, b_mod, norm1_g, w_in, mlstm_gate_b, mlstm_norm_g, mla_q_norm_g, mla_w_q_up, mla_kv_norm_g, mla_w_kv_up, diff_lambda, diff_norm_g, w_br_mlstm, w_br_mla, w_br_diff, w_out, norm2_g, w_ff1, w_ff2, final_norm_g):
    bp, sp, _ = x_prompt.shape
    bs, ss, _ = x_sample.shape
    past = cache_mla_ckv.shape[2]
    assert bs + 1 <= 8 and ss % GRID_W == 0
    assert sp % ROW_TILE == 0 and ss % ROW_TILE == 0 and past % ROW_TILE == 0

    cond8 = jnp.concatenate([c_ctx[None, :], c, jnp.zeros((8 - 1 - bs, D_MODEL), F32)], axis=0)
    mod = _modulation(cond8, w_mod, b_mod).reshape(DEPTH, 8, 1, 6 * D_MODEL)
    wts = _prep_weights(w_in, mlstm_gate_b, norm1_g, mlstm_norm_g, mla_q_norm_g, mla_w_q_up, mla_kv_norm_g,
                        mla_w_kv_up, diff_lambda, diff_norm_g, w_br_mlstm, w_br_mla, w_br_diff, w_out,
                        norm2_g, w_ff1, w_ff2)
    tables = _rope_tables(ss)
    smask = _state_mask()
    gfin = final_norm_g[None, :]
    rows = H_M * DK_M
    cache = {
        "past": past,
        "ckv": cache_mla_ckv,
        "krope": jnp.pad(cache_mla_krope, ((0, 0), (0, 0), (0, 0), (NOPE_A, LANE - NOPE_A - ROPE_A))),
        "diff_k": cache_diff_k.reshape(bs, DEPTH, past, H_D * 2 * DK_D),
        "diff_v": cache_diff_v.reshape(bs, DEPTH, past, H_D * DV_D),
        "state": (state_mlstm_C.reshape(bs, DEPTH, 2, rows, DV_M),
                  state_mlstm_n.reshape(bs, DEPTH, 2, rows, 1),
                  jnp.pad(state_mlstm_m.reshape(bs, DEPTH, 1, 2 * H_M), ((0, 0), (0, 0), (0, 0), (0, LANE - 2 * H_M)))),
    }

    y_p = x_prompt.reshape(bp * sp, D_MODEL)
    y_s = x_sample.reshape(bs * ss, D_MODEL)
    stacks = None
    states = []
    for l in range(DEPTH):
        y_p, stacks, state = _layer(y_p, l, mod, wts, smask, gfin, bp, sp, ctx_stacks=stacks)
        states.append(state)
        y_s, _, _ = _layer(y_s, l, mod, wts, smask, gfin, bs, ss, cache=cache, tables=tables)

    ckv, akr, kd, vd = stacks
    c_t = jnp.stack([s[0] for s in states], axis=1).reshape(bp, DEPTH, 2, H_M, DK_M, DV_M)
    n_t = jnp.stack([s[1] for s in states], axis=1).reshape(bp, DEPTH, 2, H_M, DK_M)
    m_t = jnp.stack([s[2][:, 0, :2 * H_M] for s in states], axis=1).reshape(bp, DEPTH, 2, H_M)
    return (y_p.reshape(bp, sp, D_MODEL), y_s.reshape(bs, ss, D_MODEL),
            ckv, akr[..., NOPE_A:NOPE_A + ROPE_A],
            kd.reshape(bp, DEPTH, sp, H_D, 2 * DK_D), vd.reshape(bp, DEPTH, sp, H_D, DV_D),
            c_t, n_t, m_t)
```

```python
import functools
import math

import jax
import jax.numpy as jnp
from jax import lax
from jax.experimental import pallas as pl
from jax.experimental.pallas import tpu as pltpu

F32 = jnp.float32
BF16 = jnp.bfloat16

D_MODEL = 1024
DEPTH = 2
GRID_W = 64
ROPE_BASE = 10000.0
EPS = 1e-6
H_M, DK_M, DV_M = 4, 64, 128
H_A, Q_RANK, KV_RANK, NOPE_A, ROPE_A, V_A = 8, 384, 256, 64, 32, 64
H_D, DK_D, DV_D = 4, 64, 128
D_FF = 4 * D_MODEL

LANE = 128
VMEM_LIMIT = 56 * 1024 * 1024
ROW_TILE = 256
Q_TILE = 256
CHUNK = 128
KEY_CHUNKS = 16
SCORE_VMEM_BUDGET = 24 * 1024 * 1024
HEAD_GROUP = 4
MLSTM_STREAMS = 2
NEG = -1e30
LOG2E = 1.4426950408889634

C_MQ, C_MK, C_MV, C_MO, C_GI, C_GF = 0, 256, 512, 1024, 1536, 1664
M_COLS = 1792
C_A = 1792
A_COLS = 768
C_D = 2560
D_COLS = 1536
C_G = 4096
G_COLS = 3 * D_MODEL
IN_COLS_P = 7168
S_COLS = H_M * DV_M + LANE


def _dot(a, b):
    return jnp.dot(a, b, preferred_element_type=F32)


def _dot_nt(a, b):
    return lax.dot_general(a, b, (((1,), (1,)), ((), ())), preferred_element_type=F32)


def _dot_tn(a, b):
    return lax.dot_general(a, b, (((0,), (0,)), ((), ())), preferred_element_type=F32)


def _rms_rows(x, g):
    return x * lax.rsqrt(jnp.mean(x * x, axis=-1, keepdims=True) + EPS) * g


def _rope_blocks(x, cos, sa, sb, shift):
    outs = []
    for b in range(x.shape[1] // LANE):
        xb = x[:, b * LANE:(b + 1) * LANE]
        outs.append(xb * cos + pltpu.roll(xb, shift, 1) * sa + pltpu.roll(xb, LANE - shift, 1) * sb)
    return outs[0] if len(outs) == 1 else jnp.concatenate(outs, axis=1)


def _params(sem):
    return pltpu.CompilerParams(dimension_semantics=sem, vmem_limit_bytes=VMEM_LIMIT)


def _layer_spec(l, shape):
    nd = len(shape)
    return pl.BlockSpec((None,) + shape, lambda *_: (l,) + (0,) * nd, pipeline_mode=pl.Buffered(1))


def _mod_spec(l, first_row, tiles_per_cond):
    return pl.BlockSpec((None, None, 1, 6 * D_MODEL), lambda i: (l, first_row + i // tiles_per_cond, 0, 0))


_ANY = pl.BlockSpec(memory_space=pl.ANY)


def _mod_kernel(c_ref, w_ref, b_ref, o_ref):
    c = c_ref[...]
    s = (c * jax.nn.sigmoid(c)).astype(BF16)
    o_ref[...] = _dot(s, w_ref[...].astype(BF16)) + b_ref[...]


def _modulation(cond8, w_mod, b_mod):
    tn = 1536
    n6 = 6 * D_MODEL
    return pl.pallas_call(
        _mod_kernel,
        grid=(DEPTH, n6 // tn),
        in_specs=[
            pl.BlockSpec((8, D_MODEL), lambda l, j: (0, 0)),
            pl.BlockSpec((None, D_MODEL, tn), lambda l, j: (l, 0, j)),
            pl.BlockSpec((None, 1, tn), lambda l, j: (l, 0, j)),
        ],
        out_specs=pl.BlockSpec((None, 8, tn), lambda l, j: (l, 0, j)),
        out_shape=jax.ShapeDtypeStruct((DEPTH, 8, n6), F32),
        compiler_params=_params(("parallel", "parallel")),
        name="modulation",
    )(cond8, w_mod, b_mod.reshape(DEPTH, 1, n6))


def _store_slots(ref, val):
    if len(ref.shape) == 3:
        for s in range(ref.shape[0]):
            ref[s] = val
    else:
        ref[...] = val


def _inproj_kernel(latent, n_aliased, *refs):
    (x_ref, mod_ref, g1_ref, w_ref, gq_ref, gkv_ref, wq_ref, wk_ref, wv_ref) = refs[:9]
    refs = refs[9:]
    if latent:
        tab_ref = refs[0]
        (m_ref, q_ref, k_ref, va_ref, qd_ref, kd_ref, vd_ref, gate_ref) = refs[1:]
    else:
        (m_ref, q_ref, k_ref, va_ref, qd_ref, kd_ref, vd_ref, gate_ref,
         ckv_ref, akr_ref, kdraw_ref, vdraw_ref) = refs[n_aliased:]
    d = D_MODEL
    x = x_ref[...]
    mod = mod_ref[...]
    h = (_rms_rows(x, g1_ref[...]) * (1.0 + mod[:, d:2 * d]) + mod[:, 0:d]).astype(BF16)

    m_ref[...] = _dot(h, w_ref[:, 0:M_COLS])

    za = _dot(h, w_ref[:, C_A:C_A + A_COLS])
    acq = za[:, 0:Q_RANK]
    ackv = za[:, Q_RANK:Q_RANK + KV_RANK]
    akr = za[:, Q_RANK + KV_RANK:A_COLS]
    q = _dot(_rms_rows(acq, gq_ref[...]).astype(BF16), wq_ref[...])
    ckv = _rms_rows(ackv, gkv_ref[...])
    ckv_b = ckv.astype(BF16)
    kn = _dot(ckv_b, wk_ref[...])
    if not latent:
        _store_slots(ckv_ref, ckv)
        _store_slots(akr_ref, akr)
    else:
        tab = tab_ref[...]
        cq, saq, sbq = tab[:, 0:128], tab[:, 128:256], tab[:, 256:384]
        q = _rope_blocks(q, cq, saq, sbq, ROPE_A // 2)
        akr = _rope_blocks(akr, cq, saq, sbq, ROPE_A // 2)
    q_ref[...] = (q * (LOG2E * (NOPE_A + ROPE_A) ** -0.5)).astype(BF16)
    k_ref[...] = (kn + jnp.concatenate([akr] * H_A, axis=1)).astype(BF16)
    va_ref[...] = _dot(ckv_b, wv_ref[...]).astype(BF16)

    zd = _dot(h, w_ref[:, C_D:C_D + D_COLS])
    dq, dk, dv = zd[:, 0:512], zd[:, 512:1024], zd[:, 1024:1536]
    if not latent:
        _store_slots(kdraw_ref, dk)
        _store_slots(vdraw_ref, dv)
    else:
        cd, sad, sbd = tab[:, 384:512], tab[:, 512:640], tab[:, 640:768]
        dq = _rope_blocks(dq, cd, sad, sbd, DK_D // 2)
        dk = _rope_blocks(dk, cd, sad, sbd, DK_D // 2)
    qd_ref[...] = (dq * (LOG2E * DK_D ** -0.5)).astype(BF16)
    kd_ref[...] = dk.astype(BF16)
    vd_ref[...] = dv.astype(BF16)

    gate_ref[...] = _dot(h, w_ref[:, C_G:C_G + G_COLS])


def _inproj(x2, l, mod, wts, batch, n_tok, tables=None, ctx_stacks=None):
    latent = tables is not None
    t = batch * n_tok
    tm = ROW_TILE
    tpb = n_tok // tm
    row = lambda i: (i, 0)

    in_specs = [
        pl.BlockSpec((tm, D_MODEL), row),
        _mod_spec(l, 1, tpb) if latent else _mod_spec(l, 0, batch * tpb),
        _layer_spec(l, (1, D_MODEL)),
        _layer_spec(l, (D_MODEL, IN_COLS_P)),
        _layer_spec(l, (1, Q_RANK)),
        _layer_spec(l, (1, KV_RANK)),
        _layer_spec(l, (Q_RANK, H_A * LANE)),
        _layer_spec(l, (KV_RANK, H_A * LANE)),
        _layer_spec(l, (KV_RANK, H_A * V_A)),
    ]
    args = [x2, mod, wts["norm1_g"], wts["w_in"], wts["mla_q_norm_g"], wts["mla_kv_norm_g"],
            wts["wq"], wts["wk"], wts["wv"]]
    widths = [(M_COLS, F32), (H_A * LANE, BF16), (H_A * LANE, BF16), (H_A * V_A, BF16),
              (512, BF16), (512, BF16), (512, BF16), (G_COLS, F32)]
    out_specs = [pl.BlockSpec((tm, w), row) for w, _ in widths]
    out_shape = [jax.ShapeDtypeStruct((t, w), dt) for w, dt in widths]
    aliases = {}
    if latent:
        in_specs.append(pl.BlockSpec((tm, 6 * LANE), lambda i: (i % tpb, 0)))
        args.append(tables)
    else:
        if ctx_stacks is None:
            stack_spec = lambda w: pl.BlockSpec((None, DEPTH, tm, w), lambda i: (i // tpb, 0, i % tpb, 0))
        else:
            stack_spec = lambda w: pl.BlockSpec((None, None, tm, w), lambda i: (i // tpb, l, i % tpb, 0))
            aliases = {len(args) + n: 8 + n for n in range(4)}
            in_specs += [_ANY] * 4
            args += list(ctx_stacks)
        for w in (KV_RANK, LANE, 512, 512):
            out_specs.append(stack_spec(w))
            out_shape.append(jax.ShapeDtypeStruct((batch, DEPTH, n_tok, w), F32))
    return pl.pallas_call(
        functools.partial(_inproj_kernel, latent, len(aliases)),
        grid=(t // tm,),
        in_specs=in_specs,
        out_specs=out_specs,
        out_shape=out_shape,
        input_output_aliases=aliases,
        compiler_params=_params(("parallel",)),
        name="inproj_lat" if latent else "inproj_ctx",
    )(*args)


def _split3(x):
    hi = x.astype(BF16)
    r1 = x - hi.astype(F32)
    mid = r1.astype(BF16)
    return hi, mid, (r1 - mid.astype(F32)).astype(BF16)


def _mlstm_chunks(items, bi, bfg, smask_ref):
    L = CHUNK
    r_i = lax.broadcasted_iota(jnp.int32, (L, L), 0)
    c_i = lax.broadcasted_iota(jnp.int32, (L, L), 1)
    lane256 = lax.broadcasted_iota(jnp.int32, (L, 256), 1)
    lane128 = lax.broadcasted_iota(jnp.int32, (1, LANE), 1)
    masks = {0: c_i <= r_i, 1: c_i >= r_i}
    tris = {d: jnp.where(m, 1.0, 0.0).astype(BF16) for d, m in masks.items()}
    heads = [(n, hd) for n in range(len(items)) for hd in range(H_M)]

    pre = []
    for z_of, s_of, _, m_prev, d, _ in items:
        q = (z_of(C_MQ, 256) * DK_M ** -0.5).astype(BF16)
        gi = z_of(C_GI, LANE) + bi
        xf = z_of(C_GF, LANE) + bfg
        lf = jnp.minimum(xf, 0.0) - jnp.log(1.0 + jnp.exp(-jnp.abs(xf)))
        b = sum(_dot(tris[d], part) for part in _split3(lf))
        g = gi - b
        pre.append(dict(q=q, kb=z_of(C_MK, 256).astype(BF16), g_t=g.T, b=b, g=g, last=L - 1 if d == 0 else 0))

    blk = lambda x, hd: x[:, hd * LANE:(hd + 1) * LANE]
    rep = lambda x, j: jnp.broadcast_to(x[:, j:j + 1], (L, LANE))
    u, av = {}, {}
    for g0 in range(0, len(heads), HEAD_GROUP):
        group = heads[g0:g0 + HEAD_GROUP]
        gm = {}
        for n, hd in group:
            d = items[n][4]
            j = d * H_M + hd
            gm[n, hd] = jnp.where(masks[d], pre[n]["g_t"][j:j + 1, :], NEG)
            row_max = jnp.broadcast_to(jnp.max(gm[n, hd], axis=-1, keepdims=True), (L, LANE))
            u[n, hd] = jnp.maximum(row_max, items[n][3][:, j:j + 1])
        a = {}
        for n, hd in group:
            q = pre[n]["q"]
            q_h = jnp.where((lane256 >= hd * DK_M) & (lane256 < (hd + 1) * DK_M), q, jnp.zeros_like(q))
            a[n, hd] = jnp.exp(gm[n, hd] - u[n, hd]) * _dot_nt(q_h, pre[n]["kb"])
        for n, hd in group:
            v_h = items[n][0](C_MV + hd * DV_M, DV_M).astype(BF16)
            av[n, hd] = _dot(a[n, hd].astype(BF16), jnp.concatenate([v_h, jnp.ones((L, LANE), BF16)], axis=1))

    lane_l = lax.broadcasted_iota(jnp.int32, (L, LANE), 1)
    results = []
    for n, (z_of, s_of, s_set, m_prev, d, h_set) in enumerate(items):
        p = pre[n]
        s_prev = s_of()
        p["qs"] = _dot(p["q"], s_prev.astype(BF16))
        w_rep, w_old = [], []
        m_new = m_prev
        for hd in range(H_M):
            j = d * H_M + hd
            uh = u[n, hd]
            mp = m_prev[:, j:j + 1]
            b_h = rep(p["b"], j)
            w_int = jnp.exp(mp - uh)
            num = w_int * blk(p["qs"], hd) + av[n, hd][:, :DV_M]
            den = w_int * rep(p["qs"], H_M * DV_M + j) + av[n, hd][:, DV_M:]
            h_set(hd, num / jnp.maximum(jnp.abs(den), jnp.exp(-(b_h + uh))))
            u_l = uh[p["last"]:p["last"] + 1, :]
            m_new = jnp.where(lane128 == j, b_h[p["last"]:p["last"] + 1, :] + u_l, m_new)
            w_rep.append(jnp.exp(rep(p["g"], j) - u_l))
            w_old.append(jnp.exp(mp - u_l))
        w_exp = jnp.concatenate([jnp.where(lane_l < DK_M, w_rep[0], w_rep[1]),
                                 jnp.where(lane_l < DK_M, w_rep[2], w_rep[3])], axis=1)
        k_w = (z_of(C_MK, 256) * w_exp).astype(BF16)
        v_aug = jnp.concatenate([z_of(C_MV, H_M * DV_M).astype(BF16), jnp.ones((L, LANE), BF16)], axis=1)
        upd = _dot_tn(k_w, v_aug)
        n_scale = jnp.zeros((1, LANE), F32)
        for hd in range(H_M):
            n_scale = jnp.where(lane128 == d * H_M + hd, w_old[hd], n_scale)
        scale_row = jnp.concatenate(w_old + [n_scale], axis=1)
        s_set(scale_row * s_prev + smask_ref[d] * upd)
        results.append(m_new)
    return results


def _mlstm_kernel(has_state, *refs):
    zf_ref, zb_ref, bi_ref, bf_ref, smask_ref = refs[:5]
    refs = refs[5:]
    if has_state:
        c0_ref, n0_ref, m0_ref = refs[:3]
        refs = refs[3:]
    hf_ref, hb_ref, ct_ref, nt_ref, mt_ref, s_scr, m_scr = refs
    i = pl.program_id(1)
    hv = H_M * DV_M

    streams = s_scr.shape[0]

    @pl.when(i == 0)
    def _():
        if has_state:
            for g in range(streams):
                for d in range(2):
                    tiled = jnp.concatenate(
                        [c0_ref[g, d]] * H_M + [jnp.broadcast_to(n0_ref[g, d], (256, LANE))], axis=1)
                    s_scr[g, d] = smask_ref[d] * tiled
                    m_scr[g, d] = m0_ref[g]
        else:
            s_scr[...] = jnp.zeros_like(s_scr)
            m_scr[...] = jnp.zeros_like(m_scr)

    bi = bi_ref[...]
    bfg = bf_ref[...]
    slots = [(g, d) for g in range(streams) for d in range(2)]

    def item(g, d):
        z_ref, h_ref = (zf_ref, hf_ref) if d == 0 else (zb_ref, hb_ref)

        def h_set(hd, val):
            h_ref[g, :, hd * DV_M:(hd + 1) * DV_M] = val

        def s_set(val):
            s_scr[g, d] = val

        return (lambda c, w: z_ref[g, :, c:c + w], lambda: s_scr[g, d], s_set, m_scr[g, d], d, h_set)

    for (g, d), m_new in zip(slots, _mlstm_chunks([item(g, d) for g, d in slots], bi, bfg, smask_ref)):
        m_scr[g, d] = m_new

    @pl.when(i == pl.num_programs(1) - 1)
    def _():
        lane = lax.broadcasted_iota(jnp.int32, (1, LANE), 1)
        for g in range(streams):
            for d in range(2):
                s = s_scr[g, d]
                ct_ref[g, d] = ((s[:, 0:DV_M] + s[:, DV_M:2 * DV_M])
                                + (s[:, 2 * DV_M:3 * DV_M] + s[:, 3 * DV_M:hv]))
                nt_ref[g, d] = jnp.sum(s[:, hv:], axis=-1, keepdims=True)
            mt_ref[g] = jnp.where(lane < H_M, m_scr[g, 0], m_scr[g, 1])


def _mlstm(m_in, l, wts, smask, batch, n_tok, state=None):
    nc = n_tok // CHUNK
    hv = H_M * DV_M
    g = math.gcd(batch, MLSTM_STREAMS)
    z3 = m_in.reshape(batch, n_tok, M_COLS)
    fwd = lambda b, i: (b, i, 0)
    bwd = lambda b, i: (b, nc - 1 - i, 0)
    in_specs = [
        pl.BlockSpec((g, CHUNK, M_COLS), fwd),
        pl.BlockSpec((g, CHUNK, M_COLS), bwd),
        pl.BlockSpec((None, 1, LANE), lambda b, i: (l, 0, 0)),
        pl.BlockSpec((None, 1, LANE), lambda b, i: (l, 0, 0)),
        pl.BlockSpec((2, 256, S_COLS), lambda b, i: (0, 0, 0)),
    ]
    args = [z3, z3, wts["gate_bi"], wts["gate_bf"], smask]
    if state is not None:
        in_specs += [
            pl.BlockSpec((g, None, 2, 256, DV_M), lambda b, i: (b, l, 0, 0, 0)),
            pl.BlockSpec((g, None, 2, 256, 1), lambda b, i: (b, l, 0, 0, 0)),
            pl.BlockSpec((g, None, 1, LANE), lambda b, i: (b, l, 0, 0)),
        ]
        args += list(state)
    h_f, h_b, c_t, n_t, m_t = pl.pallas_call(
        functools.partial(_mlstm_kernel, state is not None),
        grid=(batch // g, nc),
        in_specs=in_specs,
        out_specs=[
            pl.BlockSpec((g, CHUNK, hv), fwd),
            pl.BlockSpec((g, CHUNK, hv), bwd),
            pl.BlockSpec((g, 2, 256, DV_M), lambda b, i: (b, 0, 0, 0)),
            pl.BlockSpec((g, 2, 256, 1), lambda b, i: (b, 0, 0, 0)),
            pl.BlockSpec((g, 1, LANE), lambda b, i: (b, 0, 0)),
        ],
        out_shape=[
            jax.ShapeDtypeStruct((batch, n_tok, hv), F32),
            jax.ShapeDtypeStruct((batch, n_tok, hv), F32),
            jax.ShapeDtypeStruct((batch, 2, 256, DV_M), F32),
            jax.ShapeDtypeStruct((batch, 2, 256, 1), F32),
            jax.ShapeDtypeStruct((batch, 1, LANE), F32),
        ],
        scratch_shapes=[pltpu.VMEM((g, 2, 256, S_COLS), F32), pltpu.VMEM((g, 2, 1, LANE), F32)],
        compiler_params=_params(("parallel", "arbitrary")),
        name="mlstm",
    )(*args)
    return h_f.reshape(batch * n_tok, hv), h_b.reshape(batch * n_tok, hv), c_t, n_t, m_t


def _groups_per_step(n_groups, n_maps, n_kv, tq):
    return n_groups if 2 * n_groups * n_maps * n_kv * tq * 4 <= SCORE_VMEM_BUDGET else 1


def _pipelined_attention(kernel_fn, q, kv_new, kv_cache, extra_inputs, extra_specs, batch, n_q, n_groups,
                         widths, n_maps, name):
    past = 0 if kv_cache is None else kv_cache[0].shape[0] // batch
    n_kv = past + n_q
    tq = min(Q_TILE, n_q)
    nq = n_q // tq
    gps = _groups_per_step(n_groups, n_maps, n_kv, tq)
    n_groups //= gps
    n_maps *= gps
    q_w, k_w, v_w, out_w = (w * gps for w in widths)
    n_units = batch * n_groups * nq
    nxt = lambda s: jnp.minimum(s, n_units - 1)
    cur = lambda s: jnp.maximum(s - 1, 0)
    row_blk = lambda u: (u // (n_groups * nq)) * nq + u % nq
    grp = lambda u: (u // nq) % n_groups
    bat = lambda u: u // (n_groups * nq)
    k_map = lambda s: (bat(nxt(s)), grp(nxt(s)))
    v_map = lambda s: (bat(cur(s)), grp(cur(s)))
    in_specs = [pl.BlockSpec((tq, q_w), lambda s: (row_blk(nxt(s)), grp(nxt(s)))),
                pl.BlockSpec((n_q, k_w), k_map), pl.BlockSpec((n_q, v_w), v_map)]
    inputs = [q, *kv_new]
    if past:
        in_specs += [pl.BlockSpec((past, k_w), k_map), pl.BlockSpec((past, v_w), v_map)]
        inputs += list(kv_cache)
    score = pltpu.VMEM((n_maps, n_kv, tq), F32)
    cmax = pltpu.VMEM((n_maps, 1, tq), F32)
    return pl.pallas_call(
        functools.partial(kernel_fn, gps, nq, past),
        grid=(n_units + 1,),
        in_specs=in_specs + extra_specs(lambda s: grp(cur(s)), gps),
        out_specs=pl.BlockSpec((tq, out_w), lambda s: (row_blk(cur(s)), grp(cur(s)))),
        out_shape=jax.ShapeDtypeStruct((batch * n_q, n_groups * out_w), BF16),
        scratch_shapes=[score, cmax, score, cmax, pltpu.VMEM((v_w, n_kv), BF16)],
        compiler_params=_params(("arbitrary",)),
        name=name,
    )(*inputs, *extra_inputs)


def _attention_refs(past, refs, n_extra):
    q_ref, k_ref, v_ref = refs[:3]
    refs = refs[3:]
    kc_ref = vc_ref = None
    if past:
        kc_ref, vc_ref = refs[:2]
        refs = refs[2:]
    return (q_ref, k_ref, v_ref, kc_ref, vc_ref), refs[:n_extra], refs[n_extra:]


def _pipeline_prologue(nq, past, v_ref, vc_ref, s_b, m_b, vt_scr):
    s = pl.program_id(0)

    @pl.when(s == 0)
    def _():
        s_b[...] = jnp.zeros_like(s_b)
        m_b[...] = jnp.zeros_like(m_b)

    @pl.when(lax.rem(jnp.maximum(s - 1, 0), nq) == 0)
    def _():
        if past:
            vt_scr[:, 0:past] = vc_ref[...].T
        vt_scr[:, past:] = v_ref[...].T

    return lax.rem(s, 2)


def _key_chunks(n_kv, past):
    blocks = (n_kv - past) // LANE
    n = min(KEY_CHUNKS, blocks)
    edges = ([0] if past else []) + [past + (i * blocks // n) * LANE for i in range(n + 1)]
    return list(zip(edges[:-1], edges[1:]))


def _keys_of(k_ref, kc_ref, past, lanes):
    def k_of(c0, c1):
        return kc_ref[c0:c1, lanes] if c1 <= past else k_ref[c0 - past:c1 - past, lanes]
    return k_of


def _score_maps(maps, past, s_n, m_n, s_c, m_c):
    out = []
    for a, (k_of, q, vt_of) in enumerate(maps):
        mc = m_c[a]
        mx = l = acc = None
        for c0, c1 in _key_chunks(s_n.shape[1], past):
            st = _dot_nt(k_of(c0, c1), q)
            s_n[a, c0:c1, :] = st
            cm = jnp.max(st, axis=0, keepdims=True)
            mx = cm if mx is None else jnp.maximum(mx, cm)
            p = jnp.exp2(s_c[a, c0:c1, :] - mc)
            ps = jnp.sum(p, axis=0, keepdims=True)
            l = ps if l is None else l + ps
            pv = _dot(vt_of(c0, c1), p.astype(BF16))
            acc = pv if acc is None else acc + pv
        m_n[a] = mx
        out.append((acc, l))
    return out


def _mla_kernel(gps, nq, past, *refs):
    (q_ref, k_ref, v_ref, kc_ref, vc_ref), _, (o_ref, s_a, m_a, s_b, m_b, vt_scr) = _attention_refs(past, refs, 0)
    parity = _pipeline_prologue(nq, past, v_ref, vc_ref, s_b, m_b, vt_scr)

    def body(s_n, m_n, s_c, m_c):
        maps = []
        for e in range(2 * gps):
            lanes = slice(e * LANE, (e + 1) * LANE)
            maps.append((_keys_of(k_ref, kc_ref, past, lanes), q_ref[:, lanes],
                         lambda c0, c1, e=e: vt_scr[e * V_A:(e + 1) * V_A, c0:c1]))
        outs = [acc / l for acc, l in _score_maps(maps, past, s_n, m_n, s_c, m_c)]
        o_ref[...] = jnp.concatenate(outs, axis=0).T.astype(BF16)

    @pl.when(parity == 0)
    def _():
        body(s_a, m_a, s_b, m_b)

    @pl.when(parity == 1)
    def _():
        body(s_b, m_b, s_a, m_a)


def _mla_attention(q, kv_new, kv_cache, batch, n_q):
    return _pipelined_attention(_mla_kernel, q, kv_new, kv_cache, (), lambda cur_grp, gps: [], batch, n_q,
                                H_A // 2, (2 * LANE, 2 * LANE, LANE, LANE), 2, "mla_attention")


def _diff_kernel(lam_init, gps, nq, past, *refs):
    ((q_ref, k_ref, v_ref, kc_ref, vc_ref), (lam_ref, g_ref),
     (o_ref, s_a, m_a, s_b, m_b, vt_scr)) = _attention_refs(past, refs, 2)
    parity = _pipeline_prologue(nq, past, v_ref, vc_ref, s_b, m_b, vt_scr)

    def body(s_n, m_n, s_c, m_c):
        lane = lax.broadcasted_iota(jnp.int32, (1, LANE), 1)
        lv = lam_ref[...]
        lam = (jnp.exp(jnp.sum(lv[0:1] * lv[1:2], axis=-1, keepdims=True))
               - jnp.exp(jnp.sum(lv[2:3] * lv[3:4], axis=-1, keepdims=True)) + lam_init)
        maps = []
        for h in range(gps):
            lanes = slice(h * LANE, (h + 1) * LANE)
            q = q_ref[:, lanes]
            zero = jnp.zeros_like(q)
            k_of = _keys_of(k_ref, kc_ref, past, lanes)
            vt_of = lambda c0, c1, h=h: vt_scr[h * LANE:(h + 1) * LANE, c0:c1]
            maps += [(k_of, jnp.where(lane < DK_D, q, zero), vt_of), (k_of, jnp.where(lane >= DK_D, q, zero), vt_of)]
        res = _score_maps(maps, past, s_n, m_n, s_c, m_c)
        outs = []
        for h in range(gps):
            (acc1, l1), (acc2, l2) = res[2 * h], res[2 * h + 1]
            o = (acc1 / l1 - acc2 * (lam / l2)).T
            outs.append(_rms_rows(o, g_ref[:, h * LANE:(h + 1) * LANE]) * (1.0 - lam_init))
        o_ref[...] = (outs[0] if gps == 1 else jnp.concatenate(outs, axis=1)).astype(BF16)

    @pl.when(parity == 0)
    def _():
        body(s_a, m_a, s_b, m_b)

    @pl.when(parity == 1)
    def _():
        body(s_b, m_b, s_a, m_a)


def _diff_attention(q, kv_new, kv_cache, l, wts, lam_init, batch, n_q):
    extra = lambda cur_grp, gps: [pl.BlockSpec((None, 4, DK_D), lambda s: (l, 0, 0)),
                                  pl.BlockSpec((None, 1, gps * LANE), lambda s: (l, 0, cur_grp(s)))]
    return _pipelined_attention(functools.partial(_diff_kernel, lam_init), q, kv_new, kv_cache,
                                (wts["diff_lambda"], wts["diff_norm_g"]), extra, batch, n_q, H_D,
                                (LANE, LANE, LANE, LANE), 2, "diff_attention")


def _merge_kernel(final, x_ref, mod_ref, hf_ref, hb_ref, mo_ref, oa_ref, od_ref, gate_ref,
                  gm_ref, wbm_ref, wba_ref, wbd_ref, wout_ref, g2_ref, wff1_ref, wff2_ref, gfin_ref, o_ref):
    d = D_MODEL
    mod = mod_ref[...]
    hm = hf_ref[...] + hb_ref[...]
    gm = gm_ref[...]
    o_m = jnp.concatenate(
        [_rms_rows(hm[:, h * DV_M:(h + 1) * DV_M], gm[:, h * DV_M:(h + 1) * DV_M]) for h in range(H_M)], axis=1)
    o_m = (o_m * jax.nn.sigmoid(mo_ref[...])).astype(BF16)
    gate = jax.nn.sigmoid(gate_ref[...])
    y = (gate[:, 0:d] * _dot(o_m, wbm_ref[...]) + gate[:, d:2 * d] * _dot(oa_ref[...], wba_ref[...])
         + gate[:, 2 * d:3 * d] * _dot(od_ref[...], wbd_ref[...]))
    x = x_ref[...] + mod[:, 2 * d:3 * d] * _dot(y.astype(BF16), wout_ref[...])
    h2 = (_rms_rows(x, g2_ref[...]) * (1.0 + mod[:, 4 * d:5 * d]) + mod[:, 3 * d:4 * d]).astype(BF16)
    f = jnp.maximum(_dot(h2, wff1_ref[...]), 0.0)
    x = x + mod[:, 5 * d:6 * d] * _dot((f * f).astype(BF16), wff2_ref[...])
    if final:
        x = _rms_rows(x, gfin_ref[...])
    o_ref[...] = x


def _merge(x2, l, mod, wts, batch, n_tok, latent, m_in, h_f, h_b, o_a, o_d, gate, gfin):
    t = x2.shape[0]
    tm = ROW_TILE
    tpb = n_tok // tm
    row = lambda i: (i, 0)
    in_specs = [
        pl.BlockSpec((tm, D_MODEL), row),
        _mod_spec(l, 1, tpb) if latent else _mod_spec(l, 0, batch * tpb),
        pl.BlockSpec((tm, 512), row),
        pl.BlockSpec((tm, 512), row),
        pl.BlockSpec((tm, 512), lambda i: (i, C_MO // 512)),
        pl.BlockSpec((tm, 512), row),
        pl.BlockSpec((tm, 512), row),
        pl.BlockSpec((tm, G_COLS), row),
        _layer_spec(l, (1, 512)),
        _layer_spec(l, (512, D_MODEL)),
        _layer_spec(l, (512, D_MODEL)),
        _layer_spec(l, (512, D_MODEL)),
        _layer_spec(l, (D_MODEL, D_MODEL)),
        _layer_spec(l, (1, D_MODEL)),
        _layer_spec(l, (D_MODEL, D_FF)),
        _layer_spec(l, (D_FF, D_MODEL)),
        pl.BlockSpec((1, D_MODEL), lambda i: (0, 0)),
    ]
    return pl.pallas_call(
        functools.partial(_merge_kernel, l == DEPTH - 1),
        grid=(t // tm,),
        in_specs=in_specs,
        out_specs=pl.BlockSpec((tm, D_MODEL), row),
        out_shape=jax.ShapeDtypeStruct((t, D_MODEL), F32),
        compiler_params=_params(("parallel",)),
        name="merge_mlp",
    )(x2, mod, h_f, h_b, m_in, o_a, o_d, gate, wts["mlstm_norm_g"], wts["w_br_mlstm"], wts["w_br_mla"],
      wts["w_br_diff"], wts["w_out"], wts["norm2_g"], wts["w_ff1"], wts["w_ff2"], gfin)


_BLK_GI, _BLK_GF, _BLK_KR = C_GI // LANE, C_GF // LANE, (C_A + Q_RANK + KV_RANK) // LANE
_BLK_A, _BLK_D = C_A // LANE, C_D // LANE


REPACK_BLOCKS = 8


def _repack_kernel(*refs):
    o_ref = refs[-1]
    lane = lax.broadcasted_iota(jnp.int32, (1, LANE), 1)
    for k, w_ref in enumerate(refs[:-1]):
        c = pl.program_id(1) * REPACK_BLOCKS + k
        lo = jnp.where(c == _BLK_KR, NOPE_A, 0)
        hi = jnp.where((c == _BLK_GI) | (c == _BLK_GF), 2 * H_M, jnp.where(c == _BLK_KR, NOPE_A + ROPE_A, LANE))
        o_ref[:, k * LANE:(k + 1) * LANE] = jnp.where((lane >= lo) & (lane < hi), w_ref[0].T, 0.0).astype(BF16)


def _repack_w_in(w_in):
    depth, d, cols = w_in.shape
    c_mg = 2 * H_M * DK_M + 2 * H_M * DV_M
    c_acq = c_mg + 4 * H_M
    c_akr = c_acq + Q_RANK + KV_RANK
    c_dq = c_akr + ROPE_A

    def src(c):
        return jnp.where(c < _BLK_GI, c * LANE,
               jnp.where(c == _BLK_GI, c_mg,
               jnp.where(c == _BLK_GF, c_mg + 2 * H_M,
               jnp.where(c < _BLK_KR, c_acq + (c - _BLK_A) * LANE,
               jnp.where(c == _BLK_KR, c_akr - NOPE_A, c_dq + (c - _BLK_D) * LANE)))))

    nb = REPACK_BLOCKS
    window = lambda k: pl.BlockSpec((pl.Element(1), pl.Element(LANE), pl.Element(d)),
                                    lambda l, s: (l, pl.multiple_of(src(s * nb + k), 8), 0))
    w_t = jnp.swapaxes(w_in, 1, 2)
    return pl.pallas_call(
        _repack_kernel,
        grid=(depth, IN_COLS_P // (nb * LANE)),
        in_specs=[window(k) for k in range(nb)],
        out_specs=pl.BlockSpec((None, d, nb * LANE), lambda l, s: (l, 0, s)),
        out_shape=jax.ShapeDtypeStruct((depth, d, IN_COLS_P), BF16),
        compiler_params=_params(("parallel", "parallel")),
        name="repack_w_in",
    )(*([w_t] * nb))


def _prep_weights(w_in, mlstm_gate_b, norm1_g, mlstm_norm_g, mla_q_norm_g, mla_w_q_up, mla_kv_norm_g,
                  mla_w_kv_up, diff_lambda, diff_norm_g, w_br_mlstm, w_br_mla, w_br_diff, w_out,
                  norm2_g, w_ff1, w_ff2):
    depth = w_in.shape[0]
    w_p = _repack_w_in(w_in)
    pad8 = lambda a: jnp.pad(a, ((0, 0), (0, LANE - 8)))[:, None, :]
    wq = mla_w_q_up.astype(BF16).reshape(depth, Q_RANK, H_A, NOPE_A + ROPE_A)
    wq = jnp.pad(wq, ((0, 0), (0, 0), (0, 0), (0, LANE - NOPE_A - ROPE_A))).reshape(depth, Q_RANK, H_A * LANE)
    wkv = mla_w_kv_up.astype(BF16).reshape(depth, KV_RANK, H_A, NOPE_A + V_A)
    wk = jnp.pad(wkv[..., :NOPE_A], ((0, 0), (0, 0), (0, 0), (0, LANE - NOPE_A))).reshape(depth, KV_RANK, H_A * LANE)
    wv = wkv[..., NOPE_A:].reshape(depth, KV_RANK, H_A * V_A)
    row = lambda a: a[:, None, :]
    return {
        "w_in": w_p,
        "gate_bi": pad8(mlstm_gate_b[:, :8]), "gate_bf": pad8(mlstm_gate_b[:, 8:]),
        "norm1_g": row(norm1_g), "norm2_g": row(norm2_g), "mlstm_norm_g": row(mlstm_norm_g),
        "mla_q_norm_g": row(mla_q_norm_g), "mla_kv_norm_g": row(mla_kv_norm_g),
        "wq": wq, "wk": wk, "wv": wv,
        "diff_lambda": diff_lambda, "diff_norm_g": row(diff_norm_g),
        "w_br_mlstm": w_br_mlstm.astype(BF16), "w_br_mla": w_br_mla.astype(BF16),
        "w_br_diff": w_br_diff.astype(BF16), "w_out": w_out.astype(BF16),
        "w_ff1": w_ff1.astype(BF16), "w_ff2": w_ff2.astype(BF16),
    }


def _rope_tables(n_tokens):
    rows = n_tokens // GRID_W
    row = jnp.repeat(jnp.arange(rows, dtype=F32), GRID_W)
    col = jnp.tile(jnp.arange(GRID_W, dtype=F32), rows)

    def cs(dim):
        quarter = dim // 4
        inv = ROPE_BASE ** (-jnp.arange(quarter, dtype=F32) / quarter)
        ang = jnp.concatenate([row[:, None] * inv, col[:, None] * inv], axis=-1)
        return jnp.cos(ang), jnp.sin(ang)

    ca, sa = cs(ROPE_A)
    cd, sd = cs(DK_D)
    one = lambda n: jnp.ones((n_tokens, n), F32)
    zero = lambda n: jnp.zeros((n_tokens, n), F32)
    return jnp.concatenate([
        one(64), ca, ca, one(32),
        zero(80), sa, zero(32),
        zero(64), -sa, zero(48),
        cd, cd, cd, cd,
        zero(32), sd, zero(32), sd,
        -sd, zero(32), -sd, zero(32)], axis=1)


def _state_mask():
    r = jnp.arange(256)[:, None] // DK_M
    c = jnp.arange(S_COLS)[None, :]
    diag = (c < H_M * DV_M) & (c // DV_M == r)
    return jnp.stack([(diag | (c == H_M * DV_M + d * H_M + r)) for d in range(2)]).astype(F32)


def _layer(x2, l, mod, wts, smask, gfin, batch, n_tok, cache=None, tables=None, ctx_stacks=None):
    latent = cache is not None
    lam_init = 0.8 - 0.6 * math.exp(-0.3 * l)
    outs = _inproj(x2, l, mod, wts, batch, n_tok, tables, ctx_stacks)
    m_in, q_a, k_a, v_a, q_d, k_d, v_d, gate = outs[:8]
    kv_a = kv_d = None
    if latent:
        ck_a, cv_a, ck_d, cv_d = _cache_kv(l, wts, cache, batch)
        kv_a, kv_d = (ck_a, cv_a), (ck_d, cv_d)
    h_f, h_b, c_t, n_t, m_t = _mlstm(m_in, l, wts, smask, batch, n_tok, cache["state"] if latent else None)
    o_a = _mla_attention(q_a, (k_a, v_a), kv_a, batch, n_tok)
    o_d = _diff_attention(q_d, (k_d, v_d), kv_d, l, wts, lam_init, batch, n_tok)
    x_new = _merge(x2, l, mod, wts, batch, n_tok, latent, m_in, h_f, h_b, o_a, o_d, gate, gfin)
    return x_new, tuple(outs[8:]), (c_t, n_t, m_t)


def _cache_kv_kernel(ckv_ref, kr_ref, kd_ref, vd_ref, wk_ref, wv_ref, ka_out, va_out, kd_out, vd_out):
    ckv = ckv_ref[...].astype(BF16)
    ka_out[...] = (_dot(ckv, wk_ref[...]) + jnp.concatenate([kr_ref[...]] * H_A, axis=1)).astype(BF16)
    va_out[...] = _dot(ckv, wv_ref[...]).astype(BF16)
    kd_out[...] = kd_ref[...].astype(BF16)
    vd_out[...] = vd_ref[...].astype(BF16)


def _cache_kv(l, wts, cache, batch):
    past = cache["past"]
    tm = ROW_TILE
    ppb = past // tm
    cached = lambda i: (i // ppb, l, i % ppb, 0)
    widths = (H_A * LANE, H_A * V_A, 512, 512)
    return pl.pallas_call(
        _cache_kv_kernel,
        grid=(batch * ppb,),
        in_specs=[pl.BlockSpec((None, None, tm, KV_RANK), cached),
                  pl.BlockSpec((None, None, tm, LANE), cached),
                  pl.BlockSpec((None, None, tm, 512), cached),
                  pl.BlockSpec((None, None, tm, 512), cached),
                  _layer_spec(l, (KV_RANK, H_A * LANE)), _layer_spec(l, (KV_RANK, H_A * V_A))],
        out_specs=[pl.BlockSpec((tm, w), lambda i: (i, 0)) for w in widths],
        out_shape=[jax.ShapeDtypeStruct((batch * past, w), BF16) for w in widths],
        compiler_params=_params(("parallel",)),
        name="cache_kv",
    )(cache["ckv"], cache["krope"], cache["diff_k"], cache["diff_v"], wts["wk"], wts["wv"])


def kernel(x_prompt, x_sample, c, cache_mla_ckv, cache_mla_krope, cache_diff_k, cache_diff_v, state_mlstm_C, state_mlstm_n, state_mlstm_m, c_ctx, w_mod, b_mod, norm1_g, w_in, mlstm_gate_b, mlstm_norm_g, mla_q_norm_g, mla_w_q_up, mla_kv_norm_g, mla_w_kv_up, diff_lambda, diff_norm_g, w_br_mlstm, w_br_mla, w_br_diff, w_out, norm2_g, w_ff1, w_ff2, final_norm_g):
    bp, sp, _ = x_prompt.shape
    bs, ss, _ = x_sample.shape
    past = cache_mla_ckv.shape[2]
    assert bs + 1 <= 8 and ss % GRID_W == 0
    assert sp % ROW_TILE == 0 and ss % ROW_TILE == 0 and past % ROW_TILE == 0

    cond8 = jnp.concatenate([c_ctx[None, :], c, jnp.zeros((8 - 1 - bs, D_MODEL), F32)], axis=0)
    mod = _modulation(cond8, w_mod, b_mod).reshape(DEPTH, 8, 1, 6 * D_MODEL)
    wts = _prep_weights(w_in, mlstm_gate_b, norm1_g, mlstm_norm_g, mla_q_norm_g, mla_w_q_up, mla_kv_norm_g,
                        mla_w_kv_up, diff_lambda, diff_norm_g, w_br_mlstm, w_br_mla, w_br_diff, w_out,
                        norm2_g, w_ff1, w_ff2)
    tables = _rope_tables(ss)
    smask = _state_mask()
    gfin = final_norm_g[None, :]
    rows = H_M * DK_M
    cache = {
        "past": past,
        "ckv": cache_mla_ckv,
        "krope": jnp.pad(cache_mla_krope, ((0, 0), (0, 0), (0, 0), (NOPE_A, LANE - NOPE_A - ROPE_A))),
        "diff_k": cache_diff_k.reshape(bs, DEPTH, past, H_D * 2 * DK_D),
        "diff_v": cache_diff_v.reshape(bs, DEPTH, past, H_D * DV_D),
        "state": (state_mlstm_C.reshape(bs, DEPTH, 2, rows, DV_M),
                  state_mlstm_n.reshape(bs, DEPTH, 2, rows, 1),
                  jnp.pad(state_mlstm_m.reshape(bs, DEPTH, 1, 2 * H_M), ((0, 0), (0, 0), (0, 0), (0, LANE - 2 * H_M)))),
    }

    y_p = x_prompt.reshape(bp * sp, D_MODEL)
    y_s = x_sample.reshape(bs * ss, D_MODEL)
    stacks = None
    states = []
    for l in range(DEPTH):
        y_p, stacks, state = _layer(y_p, l, mod, wts, smask, gfin, bp, sp, ctx_stacks=stacks)
        states.append(state)
        y_s, _, _ = _layer(y_s, l, mod, wts, smask, gfin, bs, ss, cache=cache, tables=tables)

    ckv, akr, kd, vd = stacks
    c_t = jnp.stack([s[0] for s in states], axis=1).reshape(bp, DEPTH, 2, H_M, DK_M, DV_M)
    n_t = jnp.stack([s[1] for s in states], axis=1).reshape(bp, DEPTH, 2, H_M, DK_M)
    m_t = jnp.stack([s[2][:, 0, :2 * H_M] for s in states], axis=1).reshape(bp, DEPTH, 2, H_M)
    return (y_p.reshape(bp, sp, D_MODEL), y_s.reshape(bs, ss, D_MODEL),
            ckv, akr[..., NOPE_A:NOPE_A + ROPE_A],
            kd.reshape(bp, DEPTH, sp, H_D, 2 * DK_D), vd.reshape(bp, DEPTH, sp, H_D, DV_D),
            c_t, n_t, m_t)
```

```python
import functools
import math

import jax
import jax.numpy as jnp
from jax import lax
from jax.experimental import pallas as pl
from jax.experimental.pallas import tpu as pltpu

F32 = jnp.float32
BF16 = jnp.bfloat16

D_MODEL = 1024
DEPTH = 2
GRID_W = 64
ROPE_BASE = 10000.0
EPS = 1e-6
H_M, DK_M, DV_M = 4, 64, 128
H_A, Q_RANK, KV_RANK, NOPE_A, ROPE_A, V_A = 8, 384, 256, 64, 32, 64
H_D, DK_D, DV_D = 4, 64, 128
D_FF = 4 * D_MODEL

LANE = 128
VMEM_LIMIT = 56 * 1024 * 1024
ROW_TILE = 256
Q_TILE = 256
CHUNK = 128
KEY_CHUNKS = 16
SCORE_VMEM_BUDGET = 24 * 1024 * 1024
ONES_ROWS = 16
HEAD_GROUP = 4
MLSTM_STREAMS = 2
NEG = -1e30
LOG2E = 1.4426950408889634

C_MQ, C_MK, C_MV, C_MO, C_GI, C_GF = 0, 256, 512, 1024, 1536, 1664
M_COLS = 1792
C_A = 1792
A_COLS = 768
C_D = 2560
D_COLS = 1536
C_G = 4096
G_COLS = 3 * D_MODEL
IN_COLS_P = 7168
S_COLS = H_M * DV_M + LANE


def _dot(a, b):
    return jnp.dot(a, b, preferred_element_type=F32)


def _dot_nt(a, b):
    return lax.dot_general(a, b, (((1,), (1,)), ((), ())), preferred_element_type=F32)


def _dot_tn(a, b):
    return lax.dot_general(a, b, (((0,), (0,)), ((), ())), preferred_element_type=F32)


def _rms_rows(x, g):
    return x * lax.rsqrt(jnp.mean(x * x, axis=-1, keepdims=True) + EPS) * g


def _rope_blocks(x, cos, sa, sb, shift):
    outs = []
    for b in range(x.shape[1] // LANE):
        xb = x[:, b * LANE:(b + 1) * LANE]
        outs.append(xb * cos + pltpu.roll(xb, shift, 1) * sa + pltpu.roll(xb, LANE - shift, 1) * sb)
    return outs[0] if len(outs) == 1 else jnp.concatenate(outs, axis=1)


def _params(sem):
    return pltpu.CompilerParams(dimension_semantics=sem, vmem_limit_bytes=VMEM_LIMIT)


def _layer_spec(l, shape):
    nd = len(shape)
    return pl.BlockSpec((None,) + shape, lambda *_: (l,) + (0,) * nd, pipeline_mode=pl.Buffered(1))


def _mod_spec(l, first_row, tiles_per_cond):
    return pl.BlockSpec((None, None, 1, 6 * D_MODEL), lambda i: (l, first_row + i // tiles_per_cond, 0, 0))


_ANY = pl.BlockSpec(memory_space=pl.ANY)


def _mod_kernel(c_ref, w_ref, b_ref, o_ref):
    c = c_ref[...]
    s = (c * jax.nn.sigmoid(c)).astype(BF16)
    o_ref[...] = _dot(s, w_ref[...].astype(BF16)) + b_ref[...]


def _modulation(cond8, w_mod, b_mod):
    tn = 1536
    n6 = 6 * D_MODEL
    return pl.pallas_call(
        _mod_kernel,
        grid=(DEPTH, n6 // tn),
        in_specs=[
            pl.BlockSpec((8, D_MODEL), lambda l, j: (0, 0)),
            pl.BlockSpec((None, D_MODEL, tn), lambda l, j: (l, 0, j)),
            pl.BlockSpec((None, 1, tn), lambda l, j: (l, 0, j)),
        ],
        out_specs=pl.BlockSpec((None, 8, tn), lambda l, j: (l, 0, j)),
        out_shape=jax.ShapeDtypeStruct((DEPTH, 8, n6), F32),
        compiler_params=_params(("parallel", "parallel")),
        name="modulation",
    )(cond8, w_mod, b_mod.reshape(DEPTH, 1, n6))


def _store_slots(ref, val):
    if len(ref.shape) == 3:
        for s in range(ref.shape[0]):
            ref[s] = val
    else:
        ref[...] = val


def _inproj_kernel(latent, n_aliased, *refs):
    (x_ref, mod_ref, g1_ref, w_ref, gq_ref, gkv_ref, wq_ref, wk_ref, wv_ref) = refs[:9]
    refs = refs[9:]
    if latent:
        tab_ref = refs[0]
        (m_ref, q_ref, k_ref, va_ref, qd_ref, kd_ref, vd_ref, gate_ref) = refs[1:]
    else:
        (m_ref, q_ref, k_ref, va_ref, qd_ref, kd_ref, vd_ref, gate_ref,
         ckv_ref, akr_ref, kdraw_ref, vdraw_ref) = refs[n_aliased:]
    d = D_MODEL
    x = x_ref[...]
    mod = mod_ref[...]
    h = (_rms_rows(x, g1_ref[...]) * (1.0 + mod[:, d:2 * d]) + mod[:, 0:d]).astype(BF16)

    m_ref[...] = _dot(h, w_ref[:, 0:M_COLS])

    za = _dot(h, w_ref[:, C_A:C_A + A_COLS])
    acq = za[:, 0:Q_RANK]
    ackv = za[:, Q_RANK:Q_RANK + KV_RANK]
    akr = za[:, Q_RANK + KV_RANK:A_COLS]
    q = _dot(_rms_rows(acq, gq_ref[...]).astype(BF16), wq_ref[...])
    ckv = _rms_rows(ackv, gkv_ref[...])
    ckv_b = ckv.astype(BF16)
    kn = _dot(ckv_b, wk_ref[...])
    if not latent:
        _store_slots(ckv_ref, ckv)
        _store_slots(akr_ref, akr)
    else:
        tab = tab_ref[...]
        cq, saq, sbq = tab[:, 0:128], tab[:, 128:256], tab[:, 256:384]
        q = _rope_blocks(q, cq, saq, sbq, ROPE_A // 2)
        akr = _rope_blocks(akr, cq, saq, sbq, ROPE_A // 2)
    q_ref[...] = (q * (LOG2E * (NOPE_A + ROPE_A) ** -0.5)).astype(BF16)
    k_ref[...] = (kn + jnp.concatenate([akr] * H_A, axis=1)).astype(BF16)
    va_ref[...] = _dot(ckv_b, wv_ref[...]).astype(BF16)

    zd = _dot(h, w_ref[:, C_D:C_D + D_COLS])
    dq, dk, dv = zd[:, 0:512], zd[:, 512:1024], zd[:, 1024:1536]
    if not latent:
        _store_slots(kdraw_ref, dk)
        _store_slots(vdraw_ref, dv)
    else:
        cd, sad, sbd = tab[:, 384:512], tab[:, 512:640], tab[:, 640:768]
        dq = _rope_blocks(dq, cd, sad, sbd, DK_D // 2)
        dk = _rope_blocks(dk, cd, sad, sbd, DK_D // 2)
    qd_ref[...] = (dq * (LOG2E * DK_D ** -0.5)).astype(BF16)
    kd_ref[...] = dk.astype(BF16)
    vd_ref[...] = dv.astype(BF16)

    gate_ref[...] = _dot(h, w_ref[:, C_G:C_G + G_COLS])


def _inproj(x2, l, mod, wts, batch, n_tok, tables=None, ctx_stacks=None):
    latent = tables is not None
    t = batch * n_tok
    tm = ROW_TILE
    tpb = n_tok // tm
    row = lambda i: (i, 0)

    in_specs = [
        pl.BlockSpec((tm, D_MODEL), row),
        _mod_spec(l, 1, tpb) if latent else _mod_spec(l, 0, batch * tpb),
        _layer_spec(l, (1, D_MODEL)),
        _layer_spec(l, (D_MODEL, IN_COLS_P)),
        _layer_spec(l, (1, Q_RANK)),
        _layer_spec(l, (1, KV_RANK)),
        _layer_spec(l, (Q_RANK, H_A * LANE)),
        _layer_spec(l, (KV_RANK, H_A * LANE)),
        _layer_spec(l, (KV_RANK, H_A * V_A)),
    ]
    args = [x2, mod, wts["norm1_g"], wts["w_in"], wts["mla_q_norm_g"], wts["mla_kv_norm_g"],
            wts["wq"], wts["wk"], wts["wv"]]
    widths = [(M_COLS, F32), (H_A * LANE, BF16), (H_A * LANE, BF16), (H_A * V_A, BF16),
              (512, BF16), (512, BF16), (512, BF16), (G_COLS, F32)]
    out_specs = [pl.BlockSpec((tm, w), row) for w, _ in widths]
    out_shape = [jax.ShapeDtypeStruct((t, w), dt) for w, dt in widths]
    aliases = {}
    if latent:
        in_specs.append(pl.BlockSpec((tm, 6 * LANE), lambda i: (i % tpb, 0)))
        args.append(tables)
    else:
        if ctx_stacks is None:
            stack_spec = lambda w: pl.BlockSpec((None, DEPTH, tm, w), lambda i: (i // tpb, 0, i % tpb, 0))
        else:
            stack_spec = lambda w: pl.BlockSpec((None, None, tm, w), lambda i: (i // tpb, l, i % tpb, 0))
            aliases = {len(args) + n: 8 + n for n in range(4)}
            in_specs += [_ANY] * 4
            args += list(ctx_stacks)
        for w in (KV_RANK, LANE, 512, 512):
            out_specs.append(stack_spec(w))
            out_shape.append(jax.ShapeDtypeStruct((batch, DEPTH, n_tok, w), F32))
    return pl.pallas_call(
        functools.partial(_inproj_kernel, latent, len(aliases)),
        grid=(t // tm,),
        in_specs=in_specs,
        out_specs=out_specs,
        out_shape=out_shape,
        input_output_aliases=aliases,
        compiler_params=_params(("parallel",)),
        name="inproj_lat" if latent else "inproj_ctx",
    )(*args)


def _split3(x):
    hi = x.astype(BF16)
    r1 = x - hi.astype(F32)
    mid = r1.astype(BF16)
    return hi, mid, (r1 - mid.astype(F32)).astype(BF16)


def _mlstm_chunks(items, bi, bfg, smask_ref):
    L = CHUNK
    r_i = lax.broadcasted_iota(jnp.int32, (L, L), 0)
    c_i = lax.broadcasted_iota(jnp.int32, (L, L), 1)
    lane256 = lax.broadcasted_iota(jnp.int32, (L, 256), 1)
    lane128 = lax.broadcasted_iota(jnp.int32, (1, LANE), 1)
    masks = {0: c_i <= r_i, 1: c_i >= r_i}
    tris = {d: jnp.where(m, 1.0, 0.0).astype(BF16) for d, m in masks.items()}
    heads = [(n, hd) for n in range(len(items)) for hd in range(H_M)]

    pre = []
    for z_of, s_of, _, m_prev, d, _ in items:
        q = (z_of(C_MQ, 256) * DK_M ** -0.5).astype(BF16)
        gi = z_of(C_GI, LANE) + bi
        xf = z_of(C_GF, LANE) + bfg
        lf = jnp.minimum(xf, 0.0) - jnp.log(1.0 + jnp.exp(-jnp.abs(xf)))
        b = sum(_dot(tris[d], part) for part in _split3(lf))
        g = gi - b
        pre.append(dict(q=q, kb=z_of(C_MK, 256).astype(BF16), g_t=g.T, b=b, g=g, last=L - 1 if d == 0 else 0))

    blk = lambda x, hd: x[:, hd * LANE:(hd + 1) * LANE]
    rep = lambda x, j: jnp.broadcast_to(x[:, j:j + 1], (L, LANE))
    u, av = {}, {}
    for g0 in range(0, len(heads), HEAD_GROUP):
        group = heads[g0:g0 + HEAD_GROUP]
        gm = {}
        for n, hd in group:
            d = items[n][4]
            j = d * H_M + hd
            gm[n, hd] = jnp.where(masks[d], pre[n]["g_t"][j:j + 1, :], NEG)
            row_max = jnp.broadcast_to(jnp.max(gm[n, hd], axis=-1, keepdims=True), (L, LANE))
            u[n, hd] = jnp.maximum(row_max, items[n][3][:, j:j + 1])
        a = {}
        for n, hd in group:
            q = pre[n]["q"]
            q_h = jnp.where((lane256 >= hd * DK_M) & (lane256 < (hd + 1) * DK_M), q, jnp.zeros_like(q))
            a[n, hd] = jnp.exp(gm[n, hd] - u[n, hd]) * _dot_nt(q_h, pre[n]["kb"])
        for n, hd in group:
            v_h = items[n][0](C_MV + hd * DV_M, DV_M).astype(BF16)
            av[n, hd] = _dot(a[n, hd].astype(BF16), jnp.concatenate([v_h, jnp.ones((L, LANE), BF16)], axis=1))

    lane_l = lax.broadcasted_iota(jnp.int32, (L, LANE), 1)
    results = []
    for n, (z_of, s_of, s_set, m_prev, d, h_set) in enumerate(items):
        p = pre[n]
        s_prev = s_of()
        p["qs"] = _dot(p["q"], s_prev.astype(BF16))
        w_rep, w_old = [], []
        m_new = m_prev
        for hd in range(H_M):
            j = d * H_M + hd
            uh = u[n, hd]
            mp = m_prev[:, j:j + 1]
            b_h = rep(p["b"], j)
            w_int = jnp.exp(mp - uh)
            num = w_int * blk(p["qs"], hd) + av[n, hd][:, :DV_M]
            den = w_int * rep(p["qs"], H_M * DV_M + j) + av[n, hd][:, DV_M:]
            h_set(hd, num / jnp.maximum(jnp.abs(den), jnp.exp(-(b_h + uh))))
            u_l = uh[p["last"]:p["last"] + 1, :]
            m_new = jnp.where(lane128 == j, b_h[p["last"]:p["last"] + 1, :] + u_l, m_new)
            w_rep.append(jnp.exp(rep(p["g"], j) - u_l))
            w_old.append(jnp.exp(mp - u_l))
        w_exp = jnp.concatenate([jnp.where(lane_l < DK_M, w_rep[0], w_rep[1]),
                                 jnp.where(lane_l < DK_M, w_rep[2], w_rep[3])], axis=1)
        k_w = (z_of(C_MK, 256) * w_exp).astype(BF16)
        v_aug = jnp.concatenate([z_of(C_MV, H_M * DV_M).astype(BF16), jnp.ones((L, LANE), BF16)], axis=1)
        upd = _dot_tn(k_w, v_aug)
        n_scale = jnp.zeros((1, LANE), F32)
        for hd in range(H_M):
            n_scale = jnp.where(lane128 == d * H_M + hd, w_old[hd], n_scale)
        scale_row = jnp.concatenate(w_old + [n_scale], axis=1)
        s_set(scale_row * s_prev + smask_ref[d] * upd)
        results.append(m_new)
    return results


def _mlstm_kernel(has_state, *refs):
    zf_ref, zb_ref, bi_ref, bf_ref, smask_ref = refs[:5]
    refs = refs[5:]
    if has_state:
        c0_ref, n0_ref, m0_ref = refs[:3]
        refs = refs[3:]
    hf_ref, hb_ref, ct_ref, nt_ref, mt_ref, s_scr, m_scr = refs
    i = pl.program_id(1)
    hv = H_M * DV_M

    streams = s_scr.shape[0]

    @pl.when(i == 0)
    def _():
        if has_state:
            for g in range(streams):
                for d in range(2):
                    tiled = jnp.concatenate(
                        [c0_ref[g, d]] * H_M + [jnp.broadcast_to(n0_ref[g, d], (256, LANE))], axis=1)
                    s_scr[g, d] = smask_ref[d] * tiled
                    m_scr[g, d] = m0_ref[g]
        else:
            s_scr[...] = jnp.zeros_like(s_scr)
            m_scr[...] = jnp.zeros_like(m_scr)

    bi = bi_ref[...]
    bfg = bf_ref[...]
    slots = [(g, d) for g in range(streams) for d in range(2)]

    def item(g, d):
        z_ref, h_ref = (zf_ref, hf_ref) if d == 0 else (zb_ref, hb_ref)

        def h_set(hd, val):
            h_ref[g, :, hd * DV_M:(hd + 1) * DV_M] = val

        def s_set(val):
            s_scr[g, d] = val

        return (lambda c, w: z_ref[g, :, c:c + w], lambda: s_scr[g, d], s_set, m_scr[g, d], d, h_set)

    for (g, d), m_new in zip(slots, _mlstm_chunks([item(g, d) for g, d in slots], bi, bfg, smask_ref)):
        m_scr[g, d] = m_new

    @pl.when(i == pl.num_programs(1) - 1)
    def _():
        lane = lax.broadcasted_iota(jnp.int32, (1, LANE), 1)
        for g in range(streams):
            for d in range(2):
                s = s_scr[g, d]
                ct_ref[g, d] = ((s[:, 0:DV_M] + s[:, DV_M:2 * DV_M])
                                + (s[:, 2 * DV_M:3 * DV_M] + s[:, 3 * DV_M:hv]))
                nt_ref[g, d] = jnp.sum(s[:, hv:], axis=-1, keepdims=True)
            mt_ref[g] = jnp.where(lane < H_M, m_scr[g, 0], m_scr[g, 1])


def _mlstm(m_in, l, wts, smask, batch, n_tok, state=None):
    nc = n_tok // CHUNK
    hv = H_M * DV_M
    g = math.gcd(batch, MLSTM_STREAMS)
    z3 = m_in.reshape(batch, n_tok, M_COLS)
    fwd = lambda b, i: (b, i, 0)
    bwd = lambda b, i: (b, nc - 1 - i, 0)
    in_specs = [
        pl.BlockSpec((g, CHUNK, M_COLS), fwd),
        pl.BlockSpec((g, CHUNK, M_COLS), bwd),
        pl.BlockSpec((None, 1, LANE), lambda b, i: (l, 0, 0)),
        pl.BlockSpec((None, 1, LANE), lambda b, i: (l, 0, 0)),
        pl.BlockSpec((2, 256, S_COLS), lambda b, i: (0, 0, 0)),
    ]
    args = [z3, z3, wts["gate_bi"], wts["gate_bf"], smask]
    if state is not None:
        in_specs += [
            pl.BlockSpec((g, None, 2, 256, DV_M), lambda b, i: (b, l, 0, 0, 0)),
            pl.BlockSpec((g, None, 2, 256, 1), lambda b, i: (b, l, 0, 0, 0)),
            pl.BlockSpec((g, None, 1, LANE), lambda b, i: (b, l, 0, 0)),
        ]
        args += list(state)
    h_f, h_b, c_t, n_t, m_t = pl.pallas_call(
        functools.partial(_mlstm_kernel, state is not None),
        grid=(batch // g, nc),
        in_specs=in_specs,
        out_specs=[
            pl.BlockSpec((g, CHUNK, hv), fwd),
            pl.BlockSpec((g, CHUNK, hv), bwd),
            pl.BlockSpec((g, 2, 256, DV_M), lambda b, i: (b, 0, 0, 0)),
            pl.BlockSpec((g, 2, 256, 1), lambda b, i: (b, 0, 0, 0)),
            pl.BlockSpec((g, 1, LANE), lambda b, i: (b, 0, 0)),
        ],
        out_shape=[
            jax.ShapeDtypeStruct((batch, n_tok, hv), F32),
            jax.ShapeDtypeStruct((batch, n_tok, hv), F32),
            jax.ShapeDtypeStruct((batch, 2, 256, DV_M), F32),
            jax.ShapeDtypeStruct((batch, 2, 256, 1), F32),
            jax.ShapeDtypeStruct((batch, 1, LANE), F32),
        ],
        scratch_shapes=[pltpu.VMEM((g, 2, 256, S_COLS), F32), pltpu.VMEM((g, 2, 1, LANE), F32)],
        compiler_params=_params(("parallel", "arbitrary")),
        name="mlstm",
    )(*args)
    return h_f.reshape(batch * n_tok, hv), h_b.reshape(batch * n_tok, hv), c_t, n_t, m_t


def _groups_per_step(n_groups, n_maps, n_kv, tq):
    return n_groups if 2 * n_groups * n_maps * n_kv * tq * 4 <= SCORE_VMEM_BUDGET else 1


def _pipelined_attention(kernel_fn, q, kv_new, kv_cache, extra_inputs, extra_specs, batch, n_q, n_groups,
                         widths, n_maps, name):
    past = 0 if kv_cache is None else kv_cache[0].shape[0] // batch
    n_kv = past + n_q
    tq = min(Q_TILE, n_q)
    nq = n_q // tq
    gps = _groups_per_step(n_groups, n_maps, n_kv, tq)
    n_groups //= gps
    n_maps *= gps
    q_w, k_w, v_w, out_w = (w * gps for w in widths)
    n_units = batch * n_groups * nq
    nxt = lambda s: jnp.minimum(s, n_units - 1)
    cur = lambda s: jnp.maximum(s - 1, 0)
    row_blk = lambda u: (u // (n_groups * nq)) * nq + u % nq
    grp = lambda u: (u // nq) % n_groups
    bat = lambda u: u // (n_groups * nq)
    k_map = lambda s: (bat(nxt(s)), grp(nxt(s)))
    v_map = lambda s: (bat(cur(s)), grp(cur(s)))
    in_specs = [pl.BlockSpec((tq, q_w), lambda s: (row_blk(nxt(s)), grp(nxt(s)))),
                pl.BlockSpec((n_q, k_w), k_map), pl.BlockSpec((n_q, v_w), v_map)]
    inputs = [q, *kv_new]
    if past:
        in_specs += [pl.BlockSpec((past, k_w), k_map), pl.BlockSpec((past, v_w), v_map)]
        inputs += list(kv_cache)
    score = pltpu.VMEM((n_maps, n_kv, tq), F32)
    cmax = pltpu.VMEM((n_maps, 1, tq), F32)
    return pl.pallas_call(
        functools.partial(kernel_fn, gps, nq, past),
        grid=(n_units + 1,),
        in_specs=in_specs + extra_specs(lambda s: grp(cur(s)), gps),
        out_specs=pl.BlockSpec((tq, out_w), lambda s: (row_blk(cur(s)), grp(cur(s)))),
        out_shape=jax.ShapeDtypeStruct((batch * n_q, n_groups * out_w), BF16),
        scratch_shapes=[score, cmax, score, cmax, pltpu.VMEM((v_w, n_kv), BF16)],
        compiler_params=_params(("arbitrary",)),
        name=name,
    )(*inputs, *extra_inputs)


def _attention_refs(past, refs, n_extra):
    q_ref, k_ref, v_ref = refs[:3]
    refs = refs[3:]
    kc_ref = vc_ref = None
    if past:
        kc_ref, vc_ref = refs[:2]
        refs = refs[2:]
    return (q_ref, k_ref, v_ref, kc_ref, vc_ref), refs[:n_extra], refs[n_extra:]


def _pipeline_prologue(nq, past, v_ref, vc_ref, s_b, m_b, vt_scr):
    s = pl.program_id(0)

    @pl.when(s == 0)
    def _():
        s_b[...] = jnp.zeros_like(s_b)
        m_b[...] = jnp.zeros_like(m_b)

    @pl.when(lax.rem(jnp.maximum(s - 1, 0), nq) == 0)
    def _():
        if past:
            vt_scr[:, 0:past] = vc_ref[...].T
        vt_scr[:, past:] = v_ref[...].T

    return lax.rem(s, 2)


def _key_chunks(n_kv, past):
    blocks = (n_kv - past) // LANE
    n = min(KEY_CHUNKS, blocks)
    edges = ([0] if past else []) + [past + (i * blocks // n) * LANE for i in range(n + 1)]
    return list(zip(edges[:-1], edges[1:]))


def _keys_of(k_ref, kc_ref, past, lanes):
    def k_of(c0, c1):
        return kc_ref[c0:c1, lanes] if c1 <= past else k_ref[c0 - past:c1 - past, lanes]
    return k_of


def _score_maps(maps, past, s_n, m_n, s_c, m_c, mxu_sums=False):
    out = []
    for a, (k_of, q, vt_of) in enumerate(maps):
        mc = m_c[a]
        mx = l = acc = None
        for c0, c1 in _key_chunks(s_n.shape[1], past):
            st = _dot_nt(k_of(c0, c1), q)
            s_n[a, c0:c1, :] = st
            cm = jnp.max(st, axis=0, keepdims=True)
            mx = cm if mx is None else jnp.maximum(mx, cm)
            p = jnp.exp2(s_c[a, c0:c1, :] - mc)
            vt = vt_of(c0, c1)
            if mxu_sums:
                vt = jnp.concatenate([vt, jnp.ones((ONES_ROWS, c1 - c0), BF16)], axis=0)
            else:
                ps = jnp.sum(p, axis=0, keepdims=True)
                l = ps if l is None else l + ps
            pv = _dot(vt, p.astype(BF16))
            acc = pv if acc is None else acc + pv
        m_n[a] = mx
        out.append((acc[:-ONES_ROWS], acc[-ONES_ROWS:-ONES_ROWS + 1]) if mxu_sums else (acc, l))
    return out


def _mla_kernel(gps, nq, past, *refs):
    (q_ref, k_ref, v_ref, kc_ref, vc_ref), _, (o_ref, s_a, m_a, s_b, m_b, vt_scr) = _attention_refs(past, refs, 0)
    parity = _pipeline_prologue(nq, past, v_ref, vc_ref, s_b, m_b, vt_scr)

    def body(s_n, m_n, s_c, m_c):
        maps = []
        for e in range(2 * gps):
            lanes = slice(e * LANE, (e + 1) * LANE)
            maps.append((_keys_of(k_ref, kc_ref, past, lanes), q_ref[:, lanes],
                         lambda c0, c1, e=e: vt_scr[e * V_A:(e + 1) * V_A, c0:c1]))
        outs = [acc / l for acc, l in _score_maps(maps, past, s_n, m_n, s_c, m_c, mxu_sums=True)]
        o_ref[...] = jnp.concatenate(outs, axis=0).T.astype(BF16)

    @pl.when(parity == 0)
    def _():
        body(s_a, m_a, s_b, m_b)

    @pl.when(parity == 1)
    def _():
        body(s_b, m_b, s_a, m_a)


def _mla_attention(q, kv_new, kv_cache, batch, n_q):
    return _pipelined_attention(_mla_kernel, q, kv_new, kv_cache, (), lambda cur_grp, gps: [], batch, n_q,
                                H_A // 2, (2 * LANE, 2 * LANE, LANE, LANE), 2, "mla_attention")


def _diff_kernel(lam_init, gps, nq, past, *refs):
    ((q_ref, k_ref, v_ref, kc_ref, vc_ref), (lam_ref, g_ref),
     (o_ref, s_a, m_a, s_b, m_b, vt_scr)) = _attention_refs(past, refs, 2)
    parity = _pipeline_prologue(nq, past, v_ref, vc_ref, s_b, m_b, vt_scr)

    def body(s_n, m_n, s_c, m_c):
        lane = lax.broadcasted_iota(jnp.int32, (1, LANE), 1)
        lv = lam_ref[...]
        lam = (jnp.exp(jnp.sum(lv[0:1] * lv[1:2], axis=-1, keepdims=True))
               - jnp.exp(jnp.sum(lv[2:3] * lv[3:4], axis=-1, keepdims=True)) + lam_init)
        maps = []
        for h in range(gps):
            lanes = slice(h * LANE, (h + 1) * LANE)
            q = q_ref[:, lanes]
            zero = jnp.zeros_like(q)
            k_of = _keys_of(k_ref, kc_ref, past, lanes)
            vt_of = lambda c0, c1, h=h: vt_scr[h * LANE:(h + 1) * LANE, c0:c1]
            maps += [(k_of, jnp.where(lane < DK_D, q, zero), vt_of), (k_of, jnp.where(lane >= DK_D, q, zero), vt_of)]
        res = _score_maps(maps, past, s_n, m_n, s_c, m_c)
        outs = []
        for h in range(gps):
            (acc1, l1), (acc2, l2) = res[2 * h], res[2 * h + 1]
            o = (acc1 / l1 - acc2 * (lam / l2)).T
            outs.append(_rms_rows(o, g_ref[:, h * LANE:(h + 1) * LANE]) * (1.0 - lam_init))
        o_ref[...] = (outs[0] if gps == 1 else jnp.concatenate(outs, axis=1)).astype(BF16)

    @pl.when(parity == 0)
    def _():
        body(s_a, m_a, s_b, m_b)

    @pl.when(parity == 1)
    def _():
        body(s_b, m_b, s_a, m_a)


def _diff_attention(q, kv_new, kv_cache, l, wts, lam_init, batch, n_q):
    extra = lambda cur_grp, gps: [pl.BlockSpec((None, 4, DK_D), lambda s: (l, 0, 0)),
                                  pl.BlockSpec((None, 1, gps * LANE), lambda s: (l, 0, cur_grp(s)))]
    return _pipelined_attention(functools.partial(_diff_kernel, lam_init), q, kv_new, kv_cache,
                                (wts["diff_lambda"], wts["diff_norm_g"]), extra, batch, n_q, H_D,
                                (LANE, LANE, LANE, LANE), 2, "diff_attention")


def _merge_kernel(final, x_ref, mod_ref, hf_ref, hb_ref, mo_ref, oa_ref, od_ref, gate_ref,
                  gm_ref, wbm_ref, wba_ref, wbd_ref, wout_ref, g2_ref, wff1_ref, wff2_ref, gfin_ref, o_ref):
    d = D_MODEL
    mod = mod_ref[...]
    hm = hf_ref[...] + hb_ref[...]
    gm = gm_ref[...]
    o_m = jnp.concatenate(
        [_rms_rows(hm[:, h * DV_M:(h + 1) * DV_M], gm[:, h * DV_M:(h + 1) * DV_M]) for h in range(H_M)], axis=1)
    o_m = (o_m * jax.nn.sigmoid(mo_ref[...])).astype(BF16)
    gate = jax.nn.sigmoid(gate_ref[...])
    y = (gate[:, 0:d] * _dot(o_m, wbm_ref[...]) + gate[:, d:2 * d] * _dot(oa_ref[...], wba_ref[...])
         + gate[:, 2 * d:3 * d] * _dot(od_ref[...], wbd_ref[...]))
    x = x_ref[...] + mod[:, 2 * d:3 * d] * _dot(y.astype(BF16), wout_ref[...])
    h2 = (_rms_rows(x, g2_ref[...]) * (1.0 + mod[:, 4 * d:5 * d]) + mod[:, 3 * d:4 * d]).astype(BF16)
    f = jnp.maximum(_dot(h2, wff1_ref[...]), 0.0)
    x = x + mod[:, 5 * d:6 * d] * _dot((f * f).astype(BF16), wff2_ref[...])
    if final:
        x = _rms_rows(x, gfin_ref[...])
    o_ref[...] = x


def _merge(x2, l, mod, wts, batch, n_tok, latent, m_in, h_f, h_b, o_a, o_d, gate, gfin):
    t = x2.shape[0]
    tm = ROW_TILE
    tpb = n_tok // tm
    row = lambda i: (i, 0)
    in_specs = [
        pl.BlockSpec((tm, D_MODEL), row),
        _mod_spec(l, 1, tpb) if latent else _mod_spec(l, 0, batch * tpb),
        pl.BlockSpec((tm, 512), row),
        pl.BlockSpec((tm, 512), row),
        pl.BlockSpec((tm, 512), lambda i: (i, C_MO // 512)),
        pl.BlockSpec((tm, 512), row),
        pl.BlockSpec((tm, 512), row),
        pl.BlockSpec((tm, G_COLS), row),
        _layer_spec(l, (1, 512)),
        _layer_spec(l, (512, D_MODEL)),
        _layer_spec(l, (512, D_MODEL)),
        _layer_spec(l, (512, D_MODEL)),
        _layer_spec(l, (D_MODEL, D_MODEL)),
        _layer_spec(l, (1, D_MODEL)),
        _layer_spec(l, (D_MODEL, D_FF)),
        _layer_spec(l, (D_FF, D_MODEL)),
        pl.BlockSpec((1, D_MODEL), lambda i: (0, 0)),
    ]
    return pl.pallas_call(
        functools.partial(_merge_kernel, l == DEPTH - 1),
        grid=(t // tm,),
        in_specs=in_specs,
        out_specs=pl.BlockSpec((tm, D_MODEL), row),
        out_shape=jax.ShapeDtypeStruct((t, D_MODEL), F32),
        compiler_params=_params(("parallel",)),
        name="merge_mlp",
    )(x2, mod, h_f, h_b, m_in, o_a, o_d, gate, wts["mlstm_norm_g"], wts["w_br_mlstm"], wts["w_br_mla"],
      wts["w_br_diff"], wts["w_out"], wts["norm2_g"], wts["w_ff1"], wts["w_ff2"], gfin)


_BLK_GI, _BLK_GF, _BLK_KR = C_GI // LANE, C_GF // LANE, (C_A + Q_RANK + KV_RANK) // LANE
_BLK_A, _BLK_D = C_A // LANE, C_D // LANE


REPACK_BLOCKS = 8


def _repack_kernel(*refs):
    o_ref = refs[-1]
    lane = lax.broadcasted_iota(jnp.int32, (1, LANE), 1)
    for k, w_ref in enumerate(refs[:-1]):
        c = pl.program_id(1) * REPACK_BLOCKS + k
        lo = jnp.where(c == _BLK_KR, NOPE_A, 0)
        hi = jnp.where((c == _BLK_GI) | (c == _BLK_GF), 2 * H_M, jnp.where(c == _BLK_KR, NOPE_A + ROPE_A, LANE))
        o_ref[:, k * LANE:(k + 1) * LANE] = jnp.where((lane >= lo) & (lane < hi), w_ref[0].T, 0.0).astype(BF16)


def _repack_w_in(w_in):
    depth, d, cols = w_in.shape
    c_mg = 2 * H_M * DK_M + 2 * H_M * DV_M
    c_acq = c_mg + 4 * H_M
    c_akr = c_acq + Q_RANK + KV_RANK
    c_dq = c_akr + ROPE_A

    def src(c):
        return jnp.where(c < _BLK_GI, c * LANE,
               jnp.where(c == _BLK_GI, c_mg,
               jnp.where(c == _BLK_GF, c_mg + 2 * H_M,
               jnp.where(c < _BLK_KR, c_acq + (c - _BLK_A) * LANE,
               jnp.where(c == _BLK_KR, c_akr - NOPE_A, c_dq + (c - _BLK_D) * LANE)))))

    nb = REPACK_BLOCKS
    window = lambda k: pl.BlockSpec((pl.Element(1), pl.Element(LANE), pl.Element(d)),
                                    lambda l, s: (l, pl.multiple_of(src(s * nb + k), 8), 0))
    w_t = jnp.swapaxes(w_in, 1, 2)
    return pl.pallas_call(
        _repack_kernel,
        grid=(depth, IN_COLS_P // (nb * LANE)),
        in_specs=[window(k) for k in range(nb)],
        out_specs=pl.BlockSpec((None, d, nb * LANE), lambda l, s: (l, 0, s)),
        out_shape=jax.ShapeDtypeStruct((depth, d, IN_COLS_P), BF16),
        compiler_params=_params(("parallel", "parallel")),
        name="repack_w_in",
    )(*([w_t] * nb))


def _prep_weights(w_in, mlstm_gate_b, norm1_g, mlstm_norm_g, mla_q_norm_g, mla_w_q_up, mla_kv_norm_g,
                  mla_w_kv_up, diff_lambda, diff_norm_g, w_br_mlstm, w_br_mla, w_br_diff, w_out,
                  norm2_g, w_ff1, w_ff2):
    depth = w_in.shape[0]
    w_p = _repack_w_in(w_in)
    pad8 = lambda a: jnp.pad(a, ((0, 0), (0, LANE - 8)))[:, None, :]
    wq = mla_w_q_up.astype(BF16).reshape(depth, Q_RANK, H_A, NOPE_A + ROPE_A)
    wq = jnp.pad(wq, ((0, 0), (0, 0), (0, 0), (0, LANE - NOPE_A - ROPE_A))).reshape(depth, Q_RANK, H_A * LANE)
    wkv = mla_w_kv_up.astype(BF16).reshape(depth, KV_RANK, H_A, NOPE_A + V_A)
    wk = jnp.pad(wkv[..., :NOPE_A], ((0, 0), (0, 0), (0, 0), (0, LANE - NOPE_A))).reshape(depth, KV_RANK, H_A * LANE)
    wv = wkv[..., NOPE_A:].reshape(depth, KV_RANK, H_A * V_A)
    row = lambda a: a[:, None, :]
    return {
        "w_in": w_p,
        "gate_bi": pad8(mlstm_gate_b[:, :8]), "gate_bf": pad8(mlstm_gate_b[:, 8:]),
        "norm1_g": row(norm1_g), "norm2_g": row(norm2_g), "mlstm_norm_g": row(mlstm_norm_g),
        "mla_q_norm_g": row(mla_q_norm_g), "mla_kv_norm_g": row(mla_kv_norm_g),
        "wq": wq, "wk": wk, "wv": wv,
        "diff_lambda": diff_lambda, "diff_norm_g": row(diff_norm_g),
        "w_br_mlstm": w_br_mlstm.astype(BF16), "w_br_mla": w_br_mla.astype(BF16),
        "w_br_diff": w_br_diff.astype(BF16), "w_out": w_out.astype(BF16),
        "w_ff1": w_ff1.astype(BF16), "w_ff2": w_ff2.astype(BF16),
    }


def _rope_tables(n_tokens):
    rows = n_tokens // GRID_W
    row = jnp.repeat(jnp.arange(rows, dtype=F32), GRID_W)
    col = jnp.tile(jnp.arange(GRID_W, dtype=F32), rows)

    def cs(dim):
        quarter = dim // 4
        inv = ROPE_BASE ** (-jnp.arange(quarter, dtype=F32) / quarter)
        ang = jnp.concatenate([row[:, None] * inv, col[:, None] * inv], axis=-1)
        return jnp.cos(ang), jnp.sin(ang)

    ca, sa = cs(ROPE_A)
    cd, sd = cs(DK_D)
    one = lambda n: jnp.ones((n_tokens, n), F32)
    zero = lambda n: jnp.zeros((n_tokens, n), F32)
    return jnp.concatenate([
        one(64), ca, ca, one(32),
        zero(80), sa, zero(32),
        zero(64), -sa, zero(48),
        cd, cd, cd, cd,
        zero(32), sd, zero(32), sd,
        -sd, zero(32), -sd, zero(32)], axis=1)


def _state_mask():
    r = jnp.arange(256)[:, None] // DK_M
    c = jnp.arange(S_COLS)[None, :]
    diag = (c < H_M * DV_M) & (c // DV_M == r)
    return jnp.stack([(diag | (c == H_M * DV_M + d * H_M + r)) for d in range(2)]).astype(F32)


def _layer(x2, l, mod, wts, smask, gfin, batch, n_tok, cache=None, tables=None, ctx_stacks=None):
    latent = cache is not None
    lam_init = 0.8 - 0.6 * math.exp(-0.3 * l)
    outs = _inproj(x2, l, mod, wts, batch, n_tok, tables, ctx_stacks)
    m_in, q_a, k_a, v_a, q_d, k_d, v_d, gate = outs[:8]
    kv_a = kv_d = None
    if latent:
        ck_a, cv_a, ck_d, cv_d = _cache_kv(l, wts, cache, batch)
        kv_a, kv_d = (ck_a, cv_a), (ck_d, cv_d)
    h_f, h_b, c_t, n_t, m_t = _mlstm(m_in, l, wts, smask, batch, n_tok, cache["state"] if latent else None)
    o_a = _mla_attention(q_a, (k_a, v_a), kv_a, batch, n_tok)
    o_d = _diff_attention(q_d, (k_d, v_d), kv_d, l, wts, lam_init, batch, n_tok)
    x_new = _merge(x2, l, mod, wts, batch, n_tok, latent, m_in, h_f, h_b, o_a, o_d, gate, gfin)
    return x_new, tuple(outs[8:]), (c_t, n_t, m_t)


def _cache_kv_kernel(ckv_ref, kr_ref, kd_ref, vd_ref, wk_ref, wv_ref, ka_out, va_out, kd_out, vd_out):
    ckv = ckv_ref[...].astype(BF16)
    ka_out[...] = (_dot(ckv, wk_ref[...]) + jnp.concatenate([kr_ref[...]] * H_A, axis=1)).astype(BF16)
    va_out[...] = _dot(ckv, wv_ref[...]).astype(BF16)
    kd_out[...] = kd_ref[...].astype(BF16)
    vd_out[...] = vd_ref[...].astype(BF16)


def _cache_kv(l, wts, cache, batch):
    past = cache["past"]
    tm = ROW_TILE
    ppb = past // tm
    cached = lambda i: (i // ppb, l, i % ppb, 0)
    widths = (H_A * LANE, H_A * V_A, 512, 512)
    return pl.pallas_call(
        _cache_kv_kernel,
        grid=(batch * ppb,),
        in_specs=[pl.BlockSpec((None, None, tm, KV_RANK), cached),
                  pl.BlockSpec((None, None, tm, LANE), cached),
                  pl.BlockSpec((None, None, tm, 512), cached),
                  pl.BlockSpec((None, None, tm, 512), cached),
                  _layer_spec(l, (KV_RANK, H_A * LANE)), _layer_spec(l, (KV_RANK, H_A * V_A))],
        out_specs=[pl.BlockSpec((tm, w), lambda i: (i, 0)) for w in widths],
        out_shape=[jax.ShapeDtypeStruct((batch * past, w), BF16) for w in widths],
        compiler_params=_params(("parallel",)),
        name="cache_kv",
    )(cache["ckv"], cache["krope"], cache["diff_k"], cache["diff_v"], wts["wk"], wts["wv"])


def kernel(x_prompt, x_sample, c, cache_mla_ckv, cache_mla_krope, cache_diff_k, cache_diff_v, state_mlstm_C, state_mlstm_n, state_mlstm_m, c_ctx, w_mod, b_mod, norm1_g, w_in, mlstm_gate_b, mlstm_norm_g, mla_q_norm_g, mla_w_q_up, mla_kv_norm_g, mla_w_kv_up, diff_lambda, diff_norm_g, w_br_mlstm, w_br_mla, w_br_diff, w_out, norm2_g, w_ff1, w_ff2, final_norm_g):
    bp, sp, _ = x_prompt.shape
    bs, ss, _ = x_sample.shape
    past = cache_mla_ckv.shape[2]
    assert bs + 1 <= 8 and ss % GRID_W == 0
    assert sp % ROW_TILE == 0 and ss % ROW_TILE == 0 and past % ROW_TILE == 0

    cond8 = jnp.concatenate([c_ctx[None, :], c, jnp.zeros((8 - 1 - bs, D_MODEL), F32)], axis=0)
    mod = _modulation(cond8, w_mod, b_mod).reshape(DEPTH, 8, 1, 6 * D_MODEL)
    wts = _prep_weights(w_in, mlstm_gate_b, norm1_g, mlstm_norm_g, mla_q_norm_g, mla_w_q_up, mla_kv_norm_g,
                        mla_w_kv_up, diff_lambda, diff_norm_g, w_br_mlstm, w_br_mla, w_br_diff, w_out,
                        norm2_g, w_ff1, w_ff2)
    tables = _rope_tables(ss)
    smask = _state_mask()
    gfin = final_norm_g[None, :]
    rows = H_M * DK_M
    cache = {
        "past": past,
        "ckv": cache_mla_ckv,
        "krope": jnp.pad(cache_mla_krope, ((0, 0), (0, 0), (0, 0), (NOPE_A, LANE - NOPE_A - ROPE_A))),
        "diff_k": cache_diff_k.reshape(bs, DEPTH, past, H_D * 2 * DK_D),
        "diff_v": cache_diff_v.reshape(bs, DEPTH, past, H_D * DV_D),
        "state": (state_mlstm_C.reshape(bs, DEPTH, 2, rows, DV_M),
                  state_mlstm_n.reshape(bs, DEPTH, 2, rows, 1),
                  jnp.pad(state_mlstm_m.reshape(bs, DEPTH, 1, 2 * H_M), ((0, 0), (0, 0), (0, 0), (0, LANE - 2 * H_M)))),
    }

    y_p = x_prompt.reshape(bp * sp, D_MODEL)
    y_s = x_sample.reshape(bs * ss, D_MODEL)
    stacks = None
    states = []
    for l in range(DEPTH):
        y_p, stacks, state = _layer(y_p, l, mod, wts, smask, gfin, bp, sp, ctx_stacks=stacks)
        states.append(state)
        y_s, _, _ = _layer(y_s, l, mod, wts, smask, gfin, bs, ss, cache=cache, tables=tables)

    ckv, akr, kd, vd = stacks
    c_t = jnp.stack([s[0] for s in states], axis=1).reshape(bp, DEPTH, 2, H_M, DK_M, DV_M)
    n_t = jnp.stack([s[1] for s in states], axis=1).reshape(bp, DEPTH, 2, H_M, DK_M)
    m_t = jnp.stack([s[2][:, 0, :2 * H_M] for s in states], axis=1).reshape(bp, DEPTH, 2, H_M)
    return (y_p.reshape(bp, sp, D_MODEL), y_s.reshape(bs, ss, D_MODEL),
            ckv, akr[..., NOPE_A:NOPE_A + ROPE_A],
            kd.reshape(bp, DEPTH, sp, H_D, 2 * DK_D), vd.reshape(bp, DEPTH, sp, H_D, DV_D),
            c_t, n_t, m_t)
```

```python
import functools
import math

import jax
import jax.numpy as jnp
from jax import lax
from jax.experimental import pallas as pl
from jax.experimental.pallas import tpu as pltpu

F32 = jnp.float32
BF16 = jnp.bfloat16

D_MODEL = 1024
DEPTH = 2
GRID_W = 64
ROPE_BASE = 10000.0
EPS = 1e-6
H_M, DK_M, DV_M = 4, 64, 128
H_A, Q_RANK, KV_RANK, NOPE_A, ROPE_A, V_A = 8, 384, 256, 64, 32, 64
H_D, DK_D, DV_D = 4, 64, 128
D_FF = 4 * D_MODEL

LANE = 128
VMEM_LIMIT = 56 * 1024 * 1024
ROW_TILE = 256
Q_TILE = 256
CHUNK = 128
KEY_CHUNKS = 16
SCORE_VMEM_BUDGET = 24 * 1024 * 1024
ONES_ROWS = 16
HEAD_GROUP = 4
MLSTM_STREAMS = 2
NEG = -1e30
LOG2E = 1.4426950408889634

C_MQ, C_MK, C_MV, C_MO, C_GI, C_GF = 0, 256, 512, 1024, 1536, 1664
M_COLS = 1792
C_A = 1792
A_COLS = 768
C_D = 2560
D_COLS = 1536
C_G = 4096
G_COLS = 3 * D_MODEL
IN_COLS_P = 7168
S_COLS = H_M * DV_M + LANE


def _dot(a, b):
    return jnp.dot(a, b, preferred_element_type=F32)


def _dot_nt(a, b):
    return lax.dot_general(a, b, (((1,), (1,)), ((), ())), preferred_element_type=F32)


def _dot_tn(a, b):
    return lax.dot_general(a, b, (((0,), (0,)), ((), ())), preferred_element_type=F32)


def _rms_rows(x, g):
    return x * lax.rsqrt(jnp.mean(x * x, axis=-1, keepdims=True) + EPS) * g


def _rope_blocks(x, cos, sa, sb, shift):
    outs = []
    for b in range(x.shape[1] // LANE):
        xb = x[:, b * LANE:(b + 1) * LANE]
        outs.append(xb * cos + pltpu.roll(xb, shift, 1) * sa + pltpu.roll(xb, LANE - shift, 1) * sb)
    return outs[0] if len(outs) == 1 else jnp.concatenate(outs, axis=1)


def _params(sem):
    return pltpu.CompilerParams(dimension_semantics=sem, vmem_limit_bytes=VMEM_LIMIT)


def _layer_spec(l, shape):
    nd = len(shape)
    return pl.BlockSpec((None,) + shape, lambda *_: (l,) + (0,) * nd, pipeline_mode=pl.Buffered(1))


def _mod_spec(l, first_row, tiles_per_cond):
    return pl.BlockSpec((None, None, 1, 6 * D_MODEL), lambda i: (l, first_row + i // tiles_per_cond, 0, 0))


_ANY = pl.BlockSpec(memory_space=pl.ANY)


def _mod_kernel(c_ref, w_ref, b_ref, o_ref):
    c = c_ref[...]
    s = (c * jax.nn.sigmoid(c)).astype(BF16)
    o_ref[...] = _dot(s, w_ref[...].astype(BF16)) + b_ref[...]


def _modulation(cond8, w_mod, b_mod):
    tn = 1536
    n6 = 6 * D_MODEL
    return pl.pallas_call(
        _mod_kernel,
        grid=(DEPTH, n6 // tn),
        in_specs=[
            pl.BlockSpec((8, D_MODEL), lambda l, j: (0, 0)),
            pl.BlockSpec((None, D_MODEL, tn), lambda l, j: (l, 0, j)),
            pl.BlockSpec((None, 1, tn), lambda l, j: (l, 0, j)),
        ],
        out_specs=pl.BlockSpec((None, 8, tn), lambda l, j: (l, 0, j)),
        out_shape=jax.ShapeDtypeStruct((DEPTH, 8, n6), F32),
        compiler_params=_params(("parallel", "parallel")),
        name="modulation",
    )(cond8, w_mod, b_mod.reshape(DEPTH, 1, n6))


def _store_slots(ref, val):
    if len(ref.shape) == 3:
        for s in range(ref.shape[0]):
            ref[s] = val
    else:
        ref[...] = val


def _inproj_kernel(latent, n_aliased, *refs):
    (x_ref, mod_ref, g1_ref, w_ref, gq_ref, gkv_ref, wq_ref, wk_ref, wv_ref) = refs[:9]
    refs = refs[9:]
    if latent:
        tab_ref = refs[0]
        (m_ref, q_ref, k_ref, va_ref, qd_ref, kd_ref, vd_ref, gate_ref) = refs[1:]
    else:
        (m_ref, q_ref, k_ref, va_ref, qd_ref, kd_ref, vd_ref, gate_ref,
         ckv_ref, akr_ref, kdraw_ref, vdraw_ref) = refs[n_aliased:]
    d = D_MODEL
    x = x_ref[...]
    mod = mod_ref[...]
    h = (_rms_rows(x, g1_ref[...]) * (1.0 + mod[:, d:2 * d]) + mod[:, 0:d]).astype(BF16)

    m_ref[...] = _dot(h, w_ref[:, 0:M_COLS])

    za = _dot(h, w_ref[:, C_A:C_A + A_COLS])
    acq = za[:, 0:Q_RANK]
    ackv = za[:, Q_RANK:Q_RANK + KV_RANK]
    akr = za[:, Q_RANK + KV_RANK:A_COLS]
    q = _dot(_rms_rows(acq, gq_ref[...]).astype(BF16), wq_ref[...])
    ckv = _rms_rows(ackv, gkv_ref[...])
    ckv_b = ckv.astype(BF16)
    kn = _dot(ckv_b, wk_ref[...])
    if not latent:
        _store_slots(ckv_ref, ckv)
        _store_slots(akr_ref, akr)
    else:
        tab = tab_ref[...]
        cq, saq, sbq = tab[:, 0:128], tab[:, 128:256], tab[:, 256:384]
        q = _rope_blocks(q, cq, saq, sbq, ROPE_A // 2)
        akr = _rope_blocks(akr, cq, saq, sbq, ROPE_A // 2)
    q_ref[...] = (q * (LOG2E * (NOPE_A + ROPE_A) ** -0.5)).astype(BF16)
    k_ref[...] = (kn + jnp.concatenate([akr] * H_A, axis=1)).astype(BF16)
    va_ref[...] = _dot(ckv_b, wv_ref[...]).astype(BF16)

    zd = _dot(h, w_ref[:, C_D:C_D + D_COLS])
    dq, dk, dv = zd[:, 0:512], zd[:, 512:1024], zd[:, 1024:1536]
    if not latent:
        _store_slots(kdraw_ref, dk)
        _store_slots(vdraw_ref, dv)
    else:
        cd, sad, sbd = tab[:, 384:512], tab[:, 512:640], tab[:, 640:768]
        dq = _rope_blocks(dq, cd, sad, sbd, DK_D // 2)
        dk = _rope_blocks(dk, cd, sad, sbd, DK_D // 2)
    qd_ref[...] = (dq * (LOG2E * DK_D ** -0.5)).astype(BF16)
    kd_ref[...] = dk.astype(BF16)
    vd_ref[...] = dv.astype(BF16)

    gate_ref[...] = _dot(h, w_ref[:, C_G:C_G + G_COLS])


def _inproj(x2, l, mod, wts, batch, n_tok, tables=None, ctx_stacks=None):
    latent = tables is not None
    t = batch * n_tok
    tm = ROW_TILE
    tpb = n_tok // tm
    row = lambda i: (i, 0)

    in_specs = [
        pl.BlockSpec((tm, D_MODEL), row),
        _mod_spec(l, 1, tpb) if latent else _mod_spec(l, 0, batch * tpb),
        _layer_spec(l, (1, D_MODEL)),
        _layer_spec(l, (D_MODEL, IN_COLS_P)),
        _layer_spec(l, (1, Q_RANK)),
        _layer_spec(l, (1, KV_RANK)),
        _layer_spec(l, (Q_RANK, H_A * LANE)),
        _layer_spec(l, (KV_RANK, H_A * LANE)),
        _layer_spec(l, (KV_RANK, H_A * V_A)),
    ]
    args = [x2, mod, wts["norm1_g"], wts["w_in"], wts["mla_q_norm_g"], wts["mla_kv_norm_g"],
            wts["wq"], wts["wk"], wts["wv"]]
    widths = [(M_COLS, F32), (H_A * LANE, BF16), (H_A * LANE, BF16), (H_A * V_A, BF16),
              (512, BF16), (512, BF16), (512, BF16), (G_COLS, F32)]
    out_specs = [pl.BlockSpec((tm, w), row) for w, _ in widths]
    out_shape = [jax.ShapeDtypeStruct((t, w), dt) for w, dt in widths]
    aliases = {}
    if latent:
        in_specs.append(pl.BlockSpec((tm, 6 * LANE), lambda i: (i % tpb, 0)))
        args.append(tables)
    else:
        if ctx_stacks is None:
            stack_spec = lambda w: pl.BlockSpec((None, DEPTH, tm, w), lambda i: (i // tpb, 0, i % tpb, 0))
        else:
            stack_spec = lambda w: pl.BlockSpec((None, None, tm, w), lambda i: (i // tpb, l, i % tpb, 0))
            aliases = {len(args) + n: 8 + n for n in range(4)}
            in_specs += [_ANY] * 4
            args += list(ctx_stacks)
        for w in (KV_RANK, LANE, 512, 512):
            out_specs.append(stack_spec(w))
            out_shape.append(jax.ShapeDtypeStruct((batch, DEPTH, n_tok, w), F32))
    return pl.pallas_call(
        functools.partial(_inproj_kernel, latent, len(aliases)),
        grid=(t // tm,),
        in_specs=in_specs,
        out_specs=out_specs,
        out_shape=out_shape,
        input_output_aliases=aliases,
        compiler_params=_params(("parallel",)),
        name="inproj_lat" if latent else "inproj_ctx",
    )(*args)


def _split3(x):
    hi = x.astype(BF16)
    r1 = x - hi.astype(F32)
    mid = r1.astype(BF16)
    return hi, mid, (r1 - mid.astype(F32)).astype(BF16)


def _mlstm_chunks(items, bi, bfg, smask_ref):
    L = CHUNK
    r_i = lax.broadcasted_iota(jnp.int32, (L, L), 0)
    c_i = lax.broadcasted_iota(jnp.int32, (L, L), 1)
    lane256 = lax.broadcasted_iota(jnp.int32, (L, 256), 1)
    lane128 = lax.broadcasted_iota(jnp.int32, (1, LANE), 1)
    masks = {0: c_i <= r_i, 1: c_i >= r_i}
    tris = {d: jnp.where(m, 1.0, 0.0).astype(BF16) for d, m in masks.items()}
    heads = [(n, hd) for n in range(len(items)) for hd in range(H_M)]

    pre = []
    for z_of, s_of, _, m_prev, d, _ in items:
        q = (z_of(C_MQ, 256) * DK_M ** -0.5).astype(BF16)
        gi = z_of(C_GI, LANE) + bi
        xf = z_of(C_GF, LANE) + bfg
        lf = jnp.minimum(xf, 0.0) - jnp.log(1.0 + jnp.exp(-jnp.abs(xf)))
        b = sum(_dot(tris[d], part) for part in _split3(lf))
        g = gi - b
        pre.append(dict(q=q, kb=z_of(C_MK, 256).astype(BF16), g_t=g.T, b=b, g=g, last=L - 1 if d == 0 else 0))

    blk = lambda x, hd: x[:, hd * LANE:(hd + 1) * LANE]
    rep = lambda x, j: jnp.broadcast_to(x[:, j:j + 1], (L, LANE))
    u, av = {}, {}
    for g0 in range(0, len(heads), HEAD_GROUP):
        group = heads[g0:g0 + HEAD_GROUP]
        gm = {}
        for n, hd in group:
            d = items[n][4]
            j = d * H_M + hd
            gm[n, hd] = jnp.where(masks[d], pre[n]["g_t"][j:j + 1, :], NEG)
            row_max = jnp.broadcast_to(jnp.max(gm[n, hd], axis=-1, keepdims=True), (L, LANE))
            u[n, hd] = jnp.maximum(row_max, items[n][3][:, j:j + 1])
        a = {}
        for n, hd in group:
            q = pre[n]["q"]
            q_h = jnp.where((lane256 >= hd * DK_M) & (lane256 < (hd + 1) * DK_M), q, jnp.zeros_like(q))
            a[n, hd] = jnp.exp(gm[n, hd] - u[n, hd]) * _dot_nt(q_h, pre[n]["kb"])
        for n, hd in group:
            v_h = items[n][0](C_MV + hd * DV_M, DV_M).astype(BF16)
            av[n, hd] = _dot(a[n, hd].astype(BF16), jnp.concatenate([v_h, jnp.ones((L, LANE), BF16)], axis=1))

    lane_l = lax.broadcasted_iota(jnp.int32, (L, LANE), 1)
    results = []
    for n, (z_of, s_of, s_set, m_prev, d, h_set) in enumerate(items):
        p = pre[n]
        s_prev = s_of()
        p["qs"] = _dot(p["q"], s_prev.astype(BF16))
        w_rep, w_old = [], []
        m_new = m_prev
        for hd in range(H_M):
            j = d * H_M + hd
            uh = u[n, hd]
            mp = m_prev[:, j:j + 1]
            b_h = rep(p["b"], j)
            w_int = jnp.exp(mp - uh)
            num = w_int * blk(p["qs"], hd) + av[n, hd][:, :DV_M]
            den = w_int * rep(p["qs"], H_M * DV_M + j) + av[n, hd][:, DV_M:]
            h_set(hd, num / jnp.maximum(jnp.abs(den), jnp.exp(-(b_h + uh))))
            u_l = uh[p["last"]:p["last"] + 1, :]
            m_new = jnp.where(lane128 == j, b_h[p["last"]:p["last"] + 1, :] + u_l, m_new)
            w_rep.append(jnp.exp(rep(p["g"], j) - u_l))
            w_old.append(jnp.exp(mp - u_l))
        w_exp = jnp.concatenate([jnp.where(lane_l < DK_M, w_rep[0], w_rep[1]),
                                 jnp.where(lane_l < DK_M, w_rep[2], w_rep[3])], axis=1)
        k_w = (z_of(C_MK, 256) * w_exp).astype(BF16)
        v_aug = jnp.concatenate([z_of(C_MV, H_M * DV_M).astype(BF16), jnp.ones((L, LANE), BF16)], axis=1)
        upd = _dot_tn(k_w, v_aug)
        n_scale = jnp.zeros((1, LANE), F32)
        for hd in range(H_M):
            n_scale = jnp.where(lane128 == d * H_M + hd, w_old[hd], n_scale)
        scale_row = jnp.concatenate(w_old + [n_scale], axis=1)
        s_set(scale_row * s_prev + smask_ref[d] * upd)
        results.append(m_new)
    return results


def _mlstm_kernel(has_state, *refs):
    zf_ref, zb_ref, bi_ref, bf_ref, smask_ref = refs[:5]
    refs = refs[5:]
    if has_state:
        c0_ref, n0_ref, m0_ref = refs[:3]
        refs = refs[3:]
    hf_ref, hb_ref, ct_ref, nt_ref, mt_ref, s_scr, m_scr = refs
    i = pl.program_id(1)
    hv = H_M * DV_M

    streams = s_scr.shape[0]

    @pl.when(i == 0)
    def _():
        if has_state:
            for g in range(streams):
                for d in range(2):
                    tiled = jnp.concatenate(
                        [c0_ref[g, d]] * H_M + [jnp.broadcast_to(n0_ref[g, d], (256, LANE))], axis=1)
                    s_scr[g, d] = smask_ref[d] * tiled
                    m_scr[g, d] = m0_ref[g]
        else:
            s_scr[...] = jnp.zeros_like(s_scr)
            m_scr[...] = jnp.zeros_like(m_scr)

    bi = bi_ref[...]
    bfg = bf_ref[...]
    slots = [(g, d) for g in range(streams) for d in range(2)]

    def item(g, d):
        z_ref, h_ref = (zf_ref, hf_ref) if d == 0 else (zb_ref, hb_ref)

        def h_set(hd, val):
            h_ref[g, :, hd * DV_M:(hd + 1) * DV_M] = val

        def s_set(val):
            s_scr[g, d] = val

        return (lambda c, w: z_ref[g, :, c:c + w], lambda: s_scr[g, d], s_set, m_scr[g, d], d, h_set)

    for (g, d), m_new in zip(slots, _mlstm_chunks([item(g, d) for g, d in slots], bi, bfg, smask_ref)):
        m_scr[g, d] = m_new

    @pl.when(i == pl.num_programs(1) - 1)
    def _():
        lane = lax.broadcasted_iota(jnp.int32, (1, LANE), 1)
        for g in range(streams):
            for d in range(2):
                s = s_scr[g, d]
                ct_ref[g, d] = ((s[:, 0:DV_M] + s[:, DV_M:2 * DV_M])
                                + (s[:, 2 * DV_M:3 * DV_M] + s[:, 3 * DV_M:hv]))
                nt_ref[g, d] = jnp.sum(s[:, hv:], axis=-1, keepdims=True)
            mt_ref[g] = jnp.where(lane < H_M, m_scr[g, 0], m_scr[g, 1])


def _mlstm(m_in, l, wts, smask, batch, n_tok, state=None):
    nc = n_tok // CHUNK
    hv = H_M * DV_M
    g = math.gcd(batch, MLSTM_STREAMS)
    z3 = m_in.reshape(batch, n_tok, M_COLS)
    fwd = lambda b, i: (b, i, 0)
    bwd = lambda b, i: (b, nc - 1 - i, 0)
    in_specs = [
        pl.BlockSpec((g, CHUNK, M_COLS), fwd),
        pl.BlockSpec((g, CHUNK, M_COLS), bwd),
        pl.BlockSpec((None, 1, LANE), lambda b, i: (l, 0, 0)),
        pl.BlockSpec((None, 1, LANE), lambda b, i: (l, 0, 0)),
        pl.BlockSpec((2, 256, S_COLS), lambda b, i: (0, 0, 0)),
    ]
    args = [z3, z3, wts["gate_bi"], wts["gate_bf"], smask]
    if state is not None:
        in_specs += [
            pl.BlockSpec((g, None, 2, 256, DV_M), lambda b, i: (b, l, 0, 0, 0)),
            pl.BlockSpec((g, None, 2, 256, 1), lambda b, i: (b, l, 0, 0, 0)),
            pl.BlockSpec((g, None, 1, LANE), lambda b, i: (b, l, 0, 0)),
        ]
        args += list(state)
    h_f, h_b, c_t, n_t, m_t = pl.pallas_call(
        functools.partial(_mlstm_kernel, state is not None),
        grid=(batch // g, nc),
        in_specs=in_specs,
        out_specs=[
            pl.BlockSpec((g, CHUNK, hv), fwd),
            pl.BlockSpec((g, CHUNK, hv), bwd),
            pl.BlockSpec((g, 2, 256, DV_M), lambda b, i: (b, 0, 0, 0)),
            pl.BlockSpec((g, 2, 256, 1), lambda b, i: (b, 0, 0, 0)),
            pl.BlockSpec((g, 1, LANE), lambda b, i: (b, 0, 0)),
        ],
        out_shape=[
            jax.ShapeDtypeStruct((batch, n_tok, hv), F32),
            jax.ShapeDtypeStruct((batch, n_tok, hv), F32),
            jax.ShapeDtypeStruct((batch, 2, 256, DV_M), F32),
            jax.ShapeDtypeStruct((batch, 2, 256, 1), F32),
            jax.ShapeDtypeStruct((batch, 1, LANE), F32),
        ],
        scratch_shapes=[pltpu.VMEM((g, 2, 256, S_COLS), F32), pltpu.VMEM((g, 2, 1, LANE), F32)],
        compiler_params=_params(("parallel", "arbitrary")),
        name="mlstm",
    )(*args)
    return h_f.reshape(batch * n_tok, hv), h_b.reshape(batch * n_tok, hv), c_t, n_t, m_t


def _groups_per_step(n_groups, n_maps, n_kv, tq):
    return n_groups if 2 * n_groups * n_maps * n_kv * tq * 4 <= SCORE_VMEM_BUDGET else 1


def _pipelined_attention(kernel_fn, q, kv_new, kv_cache, extra_inputs, extra_specs, batch, n_q, n_groups,
                         widths, n_maps, name):
    past = 0 if kv_cache is None else kv_cache[0].shape[0] // batch
    n_kv = past + n_q
    tq = min(Q_TILE, n_q)
    nq = n_q // tq
    gps = _groups_per_step(n_groups, n_maps, n_kv, tq)
    n_groups //= gps
    n_maps *= gps
    q_w, k_w, v_w, out_w = (w * gps for w in widths)
    n_units = batch * n_groups * nq
    nxt = lambda s: jnp.minimum(s, n_units - 1)
    cur = lambda s: jnp.maximum(s - 1, 0)
    row_blk = lambda u: (u // (n_groups * nq)) * nq + u % nq
    grp = lambda u: (u // nq) % n_groups
    bat = lambda u: u // (n_groups * nq)
    k_map = lambda s: (bat(nxt(s)), grp(nxt(s)))
    v_map = lambda s: (bat(cur(s)), grp(cur(s)))
    in_specs = [pl.BlockSpec((tq, q_w), lambda s: (row_blk(nxt(s)), grp(nxt(s)))),
                pl.BlockSpec((n_q, k_w), k_map), pl.BlockSpec((n_q, v_w), v_map)]
    inputs = [q, *kv_new]
    if past:
        in_specs += [pl.BlockSpec((past, k_w), k_map), pl.BlockSpec((past, v_w), v_map)]
        inputs += list(kv_cache)
    score = pltpu.VMEM((n_maps, n_kv, tq), F32)
    cmax = pltpu.VMEM((n_maps, 1, tq), F32)
    return pl.pallas_call(
        functools.partial(kernel_fn, gps, nq, past),
        grid=(n_units + 1,),
        in_specs=in_specs + extra_specs(lambda s: grp(cur(s)), gps),
        out_specs=pl.BlockSpec((tq, out_w), lambda s: (row_blk(cur(s)), grp(cur(s)))),
        out_shape=jax.ShapeDtypeStruct((batch * n_q, n_groups * out_w), BF16),
        scratch_shapes=[score, cmax, score, cmax, pltpu.VMEM((v_w, n_kv), BF16)],
        compiler_params=_params(("arbitrary",)),
        name=name,
    )(*inputs, *extra_inputs)


def _attention_refs(past, refs, n_extra):
    q_ref, k_ref, v_ref = refs[:3]
    refs = refs[3:]
    kc_ref = vc_ref = None
    if past:
        kc_ref, vc_ref = refs[:2]
        refs = refs[2:]
    return (q_ref, k_ref, v_ref, kc_ref, vc_ref), refs[:n_extra], refs[n_extra:]


def _pipeline_prologue(nq, past, v_ref, vc_ref, s_b, m_b, vt_scr):
    s = pl.program_id(0)

    @pl.when(s == 0)
    def _():
        s_b[...] = jnp.zeros_like(s_b)
        m_b[...] = jnp.zeros_like(m_b)

    @pl.when(lax.rem(jnp.maximum(s - 1, 0), nq) == 0)
    def _():
        if past:
            vt_scr[:, 0:past] = vc_ref[...].T
        vt_scr[:, past:] = v_ref[...].T

    return lax.rem(s, 2)


def _key_chunks(n_kv, past):
    blocks = (n_kv - past) // LANE
    n = min(KEY_CHUNKS, blocks)
    edges = ([0] if past else []) + [past + (i * blocks // n) * LANE for i in range(n + 1)]
    return list(zip(edges[:-1], edges[1:]))


def _keys_of(k_ref, kc_ref, past, lanes):
    def k_of(c0, c1):
        return kc_ref[c0:c1, lanes] if c1 <= past else k_ref[c0 - past:c1 - past, lanes]
    return k_of


def _score_maps(maps, past, s_n, m_n, s_c, m_c, mxu_sums=False):
    out = []
    for a, (k_of, q, vt_of) in enumerate(maps):
        mc = m_c[a]
        mx = l = acc = None
        for c0, c1 in _key_chunks(s_n.shape[1], past):
            p = jnp.exp2(s_c[a, c0:c1, :] - mc)
            vt = vt_of(c0, c1)
            if mxu_sums:
                vt = jnp.concatenate([vt, jnp.ones((ONES_ROWS, c1 - c0), BF16)], axis=0)
            else:
                ps = jnp.sum(p, axis=0, keepdims=True)
                l = ps if l is None else l + ps
            pv = _dot(vt, p.astype(BF16))
            acc = pv if acc is None else acc + pv
            st = _dot_nt(k_of(c0, c1), q)
            s_n[a, c0:c1, :] = st
            cm = jnp.max(st, axis=0, keepdims=True)
            mx = cm if mx is None else jnp.maximum(mx, cm)
        m_n[a] = mx
        out.append((acc[:-ONES_ROWS], acc[-ONES_ROWS:-ONES_ROWS + 1]) if mxu_sums else (acc, l))
    return out


def _mla_kernel(gps, nq, past, *refs):
    (q_ref, k_ref, v_ref, kc_ref, vc_ref), _, (o_ref, s_a, m_a, s_b, m_b, vt_scr) = _attention_refs(past, refs, 0)
    parity = _pipeline_prologue(nq, past, v_ref, vc_ref, s_b, m_b, vt_scr)

    def body(s_n, m_n, s_c, m_c):
        maps = []
        for e in range(2 * gps):
            lanes = slice(e * LANE, (e + 1) * LANE)
            maps.append((_keys_of(k_ref, kc_ref, past, lanes), q_ref[:, lanes],
                         lambda c0, c1, e=e: vt_scr[e * V_A:(e + 1) * V_A, c0:c1]))
        outs = [acc / l for acc, l in _score_maps(maps, past, s_n, m_n, s_c, m_c, mxu_sums=True)]
        o_ref[...] = jnp.concatenate(outs, axis=0).T.astype(BF16)

    @pl.when(parity == 0)
    def _():
        body(s_a, m_a, s_b, m_b)

    @pl.when(parity == 1)
    def _():
        body(s_b, m_b, s_a, m_a)


def _mla_attention(q, kv_new, kv_cache, batch, n_q):
    return _pipelined_attention(_mla_kernel, q, kv_new, kv_cache, (), lambda cur_grp, gps: [], batch, n_q,
                                H_A // 2, (2 * LANE, 2 * LANE, LANE, LANE), 2, "mla_attention")


def _diff_kernel(lam_init, gps, nq, past, *refs):
    ((q_ref, k_ref, v_ref, kc_ref, vc_ref), (lam_ref, g_ref),
     (o_ref, s_a, m_a, s_b, m_b, vt_scr)) = _attention_refs(past, refs, 2)
    parity = _pipeline_prologue(nq, past, v_ref, vc_ref, s_b, m_b, vt_scr)

    def body(s_n, m_n, s_c, m_c):
        lane = lax.broadcasted_iota(jnp.int32, (1, LANE), 1)
        lv = lam_ref[...]
        lam = (jnp.exp(jnp.sum(lv[0:1] * lv[1:2], axis=-1, keepdims=True))
               - jnp.exp(jnp.sum(lv[2:3] * lv[3:4], axis=-1, keepdims=True)) + lam_init)
        maps = []
        for h in range(gps):
            lanes = slice(h * LANE, (h + 1) * LANE)
            q = q_ref[:, lanes]
            zero = jnp.zeros_like(q)
            k_of = _keys_of(k_ref, kc_ref, past, lanes)
            vt_of = lambda c0, c1, h=h: vt_scr[h * LANE:(h + 1) * LANE, c0:c1]
            maps += [(k_of, jnp.where(lane < DK_D, q, zero), vt_of), (k_of, jnp.where(lane >= DK_D, q, zero), vt_of)]
        res = _score_maps(maps, past, s_n, m_n, s_c, m_c)
        outs = []
        for h in range(gps):
            (acc1, l1), (acc2, l2) = res[2 * h], res[2 * h + 1]
            o = (acc1 / l1 - acc2 * (lam / l2)).T
            outs.append(_rms_rows(o, g_ref[:, h * LANE:(h + 1) * LANE]) * (1.0 - lam_init))
        o_ref[...] = (outs[0] if gps == 1 else jnp.concatenate(outs, axis=1)).astype(BF16)

    @pl.when(parity == 0)
    def _():
        body(s_a, m_a, s_b, m_b)

    @pl.when(parity == 1)
    def _():
        body(s_b, m_b, s_a, m_a)


def _diff_attention(q, kv_new, kv_cache, l, wts, lam_init, batch, n_q):
    extra = lambda cur_grp, gps: [pl.BlockSpec((None, 4, DK_D), lambda s: (l, 0, 0)),
                                  pl.BlockSpec((None, 1, gps * LANE), lambda s: (l, 0, cur_grp(s)))]
    return _pipelined_attention(functools.partial(_diff_kernel, lam_init), q, kv_new, kv_cache,
                                (wts["diff_lambda"], wts["diff_norm_g"]), extra, batch, n_q, H_D,
                                (LANE, LANE, LANE, LANE), 2, "diff_attention")


def _merge_kernel(final, x_ref, mod_ref, hf_ref, hb_ref, mo_ref, oa_ref, od_ref, gate_ref,
                  gm_ref, wbm_ref, wba_ref, wbd_ref, wout_ref, g2_ref, wff1_ref, wff2_ref, gfin_ref, o_ref):
    d = D_MODEL
    mod = mod_ref[...]
    hm = hf_ref[...] + hb_ref[...]
    gm = gm_ref[...]
    o_m = jnp.concatenate(
        [_rms_rows(hm[:, h * DV_M:(h + 1) * DV_M], gm[:, h * DV_M:(h + 1) * DV_M]) for h in range(H_M)], axis=1)
    o_m = (o_m * jax.nn.sigmoid(mo_ref[...])).astype(BF16)
    gate = jax.nn.sigmoid(gate_ref[...])
    y = (gate[:, 0:d] * _dot(o_m, wbm_ref[...]) + gate[:, d:2 * d] * _dot(oa_ref[...], wba_ref[...])
         + gate[:, 2 * d:3 * d] * _dot(od_ref[...], wbd_ref[...]))
    x = x_ref[...] + mod[:, 2 * d:3 * d] * _dot(y.astype(BF16), wout_ref[...])
    h2 = (_rms_rows(x, g2_ref[...]) * (1.0 + mod[:, 4 * d:5 * d]) + mod[:, 3 * d:4 * d]).astype(BF16)
    f = jnp.maximum(_dot(h2, wff1_ref[...]), 0.0)
    x = x + mod[:, 5 * d:6 * d] * _dot((f * f).astype(BF16), wff2_ref[...])
    if final:
        x = _rms_rows(x, gfin_ref[...])
    o_ref[...] = x


def _merge(x2, l, mod, wts, batch, n_tok, latent, m_in, h_f, h_b, o_a, o_d, gate, gfin):
    t = x2.shape[0]
    tm = ROW_TILE
    tpb = n_tok // tm
    row = lambda i: (i, 0)
    in_specs = [
        pl.BlockSpec((tm, D_MODEL), row),
        _mod_spec(l, 1, tpb) if latent else _mod_spec(l, 0, batch * tpb),
        pl.BlockSpec((tm, 512), row),
        pl.BlockSpec((tm, 512), row),
        pl.BlockSpec((tm, 512), lambda i: (i, C_MO // 512)),
        pl.BlockSpec((tm, 512), row),
        pl.BlockSpec((tm, 512), row),
        pl.BlockSpec((tm, G_COLS), row),
        _layer_spec(l, (1, 512)),
        _layer_spec(l, (512, D_MODEL)),
        _layer_spec(l, (512, D_MODEL)),
        _layer_spec(l, (512, D_MODEL)),
        _layer_spec(l, (D_MODEL, D_MODEL)),
        _layer_spec(l, (1, D_MODEL)),
        _layer_spec(l, (D_MODEL, D_FF)),
        _layer_spec(l, (D_FF, D_MODEL)),
        pl.BlockSpec((1, D_MODEL), lambda i: (0, 0)),
    ]
    return pl.pallas_call(
        functools.partial(_merge_kernel, l == DEPTH - 1),
        grid=(t // tm,),
        in_specs=in_specs,
        out_specs=pl.BlockSpec((tm, D_MODEL), row),
        out_shape=jax.ShapeDtypeStruct((t, D_MODEL), F32),
        compiler_params=_params(("parallel",)),
        name="merge_mlp",
    )(x2, mod, h_f, h_b, m_in, o_a, o_d, gate, wts["mlstm_norm_g"], wts["w_br_mlstm"], wts["w_br_mla"],
      wts["w_br_diff"], wts["w_out"], wts["norm2_g"], wts["w_ff1"], wts["w_ff2"], gfin)


_BLK_GI, _BLK_GF, _BLK_KR = C_GI // LANE, C_GF // LANE, (C_A + Q_RANK + KV_RANK) // LANE
_BLK_A, _BLK_D = C_A // LANE, C_D // LANE


REPACK_BLOCKS = 8


def _repack_kernel(*refs):
    o_ref = refs[-1]
    lane = lax.broadcasted_iota(jnp.int32, (1, LANE), 1)
    for k, w_ref in enumerate(refs[:-1]):
        c = pl.program_id(1) * REPACK_BLOCKS + k
        lo = jnp.where(c == _BLK_KR, NOPE_A, 0)
        hi = jnp.where((c == _BLK_GI) | (c == _BLK_GF), 2 * H_M, jnp.where(c == _BLK_KR, NOPE_A + ROPE_A, LANE))
        o_ref[:, k * LANE:(k + 1) * LANE] = jnp.where((lane >= lo) & (lane < hi), w_ref[0].T, 0.0).astype(BF16)


def _repack_w_in(w_in):
    depth, d, cols = w_in.shape
    c_mg = 2 * H_M * DK_M + 2 * H_M * DV_M
    c_acq = c_mg + 4 * H_M
    c_akr = c_acq + Q_RANK + KV_RANK
    c_dq = c_akr + ROPE_A

    def src(c):
        return jnp.where(c < _BLK_GI, c * LANE,
               jnp.where(c == _BLK_GI, c_mg,
               jnp.where(c == _BLK_GF, c_mg + 2 * H_M,
               jnp.where(c < _BLK_KR, c_acq + (c - _BLK_A) * LANE,
               jnp.where(c == _BLK_KR, c_akr - NOPE_A, c_dq + (c - _BLK_D) * LANE)))))

    nb = REPACK_BLOCKS
    window = lambda k: pl.BlockSpec((pl.Element(1), pl.Element(LANE), pl.Element(d)),
                                    lambda l, s: (l, pl.multiple_of(src(s * nb + k), 8), 0))
    w_t = jnp.swapaxes(w_in, 1, 2)
    return pl.pallas_call(
        _repack_kernel,
        grid=(depth, IN_COLS_P // (nb * LANE)),
        in_specs=[window(k) for k in range(nb)],
        out_specs=pl.BlockSpec((None, d, nb * LANE), lambda l, s: (l, 0, s)),
        out_shape=jax.ShapeDtypeStruct((depth, d, IN_COLS_P), BF16),
        compiler_params=_params(("parallel", "parallel")),
        name="repack_w_in",
    )(*([w_t] * nb))


def _prep_weights(w_in, mlstm_gate_b, norm1_g, mlstm_norm_g, mla_q_norm_g, mla_w_q_up, mla_kv_norm_g,
                  mla_w_kv_up, diff_lambda, diff_norm_g, w_br_mlstm, w_br_mla, w_br_diff, w_out,
                  norm2_g, w_ff1, w_ff2):
    depth = w_in.shape[0]
    w_p = _repack_w_in(w_in)
    pad8 = lambda a: jnp.pad(a, ((0, 0), (0, LANE - 8)))[:, None, :]
    wq = mla_w_q_up.astype(BF16).reshape(depth, Q_RANK, H_A, NOPE_A + ROPE_A)
    wq = jnp.pad(wq, ((0, 0), (0, 0), (0, 0), (0, LANE - NOPE_A - ROPE_A))).reshape(depth, Q_RANK, H_A * LANE)
    wkv = mla_w_kv_up.astype(BF16).reshape(depth, KV_RANK, H_A, NOPE_A + V_A)
    wk = jnp.pad(wkv[..., :NOPE_A], ((0, 0), (0, 0), (0, 0), (0, LANE - NOPE_A))).reshape(depth, KV_RANK, H_A * LANE)
    wv = wkv[..., NOPE_A:].reshape(depth, KV_RANK, H_A * V_A)
    row = lambda a: a[:, None, :]
    return {
        "w_in": w_p,
        "gate_bi": pad8(mlstm_gate_b[:, :8]), "gate_bf": pad8(mlstm_gate_b[:, 8:]),
        "norm1_g": row(norm1_g), "norm2_g": row(norm2_g), "mlstm_norm_g": row(mlstm_norm_g),
        "mla_q_norm_g": row(mla_q_norm_g), "mla_kv_norm_g": row(mla_kv_norm_g),
        "wq": wq, "wk": wk, "wv": wv,
        "diff_lambda": diff_lambda, "diff_norm_g": row(diff_norm_g),
        "w_br_mlstm": w_br_mlstm.astype(BF16), "w_br_mla": w_br_mla.astype(BF16),
        "w_br_diff": w_br_diff.astype(BF16), "w_out": w_out.astype(BF16),
        "w_ff1": w_ff1.astype(BF16), "w_ff2": w_ff2.astype(BF16),
    }


def _rope_tables(n_tokens):
    rows = n_tokens // GRID_W
    row = jnp.repeat(jnp.arange(rows, dtype=F32), GRID_W)
    col = jnp.tile(jnp.arange(GRID_W, dtype=F32), rows)

    def cs(dim):
        quarter = dim // 4
        inv = ROPE_BASE ** (-jnp.arange(quarter, dtype=F32) / quarter)
        ang = jnp.concatenate([row[:, None] * inv, col[:, None] * inv], axis=-1)
        return jnp.cos(ang), jnp.sin(ang)

    ca, sa = cs(ROPE_A)
    cd, sd = cs(DK_D)
    one = lambda n: jnp.ones((n_tokens, n), F32)
    zero = lambda n: jnp.zeros((n_tokens, n), F32)
    return jnp.concatenate([
        one(64), ca, ca, one(32),
        zero(80), sa, zero(32),
        zero(64), -sa, zero(48),
        cd, cd, cd, cd,
        zero(32), sd, zero(32), sd,
        -sd, zero(32), -sd, zero(32)], axis=1)


def _state_mask():
    r = jnp.arange(256)[:, None] // DK_M
    c = jnp.arange(S_COLS)[None, :]
    diag = (c < H_M * DV_M) & (c // DV_M == r)
    return jnp.stack([(diag | (c == H_M * DV_M + d * H_M + r)) for d in range(2)]).astype(F32)


def _layer(x2, l, mod, wts, smask, gfin, batch, n_tok, cache=None, tables=None, ctx_stacks=None):
    latent = cache is not None
    lam_init = 0.8 - 0.6 * math.exp(-0.3 * l)
    outs = _inproj(x2, l, mod, wts, batch, n_tok, tables, ctx_stacks)
    m_in, q_a, k_a, v_a, q_d, k_d, v_d, gate = outs[:8]
    kv_a = kv_d = None
    if latent:
        ck_a, cv_a, ck_d, cv_d = _cache_kv(l, wts, cache, batch)
        kv_a, kv_d = (ck_a, cv_a), (ck_d, cv_d)
    h_f, h_b, c_t, n_t, m_t = _mlstm(m_in, l, wts, smask, batch, n_tok, cache["state"] if latent else None)
    o_a = _mla_attention(q_a, (k_a, v_a), kv_a, batch, n_tok)
    o_d = _diff_attention(q_d, (k_d, v_d), kv_d, l, wts, lam_init, batch, n_tok)
    x_new = _merge(x2, l, mod, wts, batch, n_tok, latent, m_in, h_f, h_b, o_a, o_d, gate, gfin)
    return x_new, tuple(outs[8:]), (c_t, n_t, m_t)


def _cache_kv_kernel(ckv_ref, kr_ref, kd_ref, vd_ref, wk_ref, wv_ref, ka_out, va_out, kd_out, vd_out):
    ckv = ckv_ref[...].astype(BF16)
    ka_out[...] = (_dot(ckv, wk_ref[...]) + jnp.concatenate([kr_ref[...]] * H_A, axis=1)).astype(BF16)
    va_out[...] = _dot(ckv, wv_ref[...]).astype(BF16)
    kd_out[...] = kd_ref[...].astype(BF16)
    vd_out[...] = vd_ref[...].astype(BF16)


def _cache_kv(l, wts, cache, batch):
    past = cache["past"]
    tm = ROW_TILE
    ppb = past // tm
    cached = lambda i: (i // ppb, l, i % ppb, 0)
    widths = (H_A * LANE, H_A * V_A, 512, 512)
    return pl.pallas_call(
        _cache_kv_kernel,
        grid=(batch * ppb,),
        in_specs=[pl.BlockSpec((None, None, tm, KV_RANK), cached),
                  pl.BlockSpec((None, None, tm, LANE), cached),
                  pl.BlockSpec((None, None, tm, 512), cached),
                  pl.BlockSpec((None, None, tm, 512), cached),
                  _layer_spec(l, (KV_RANK, H_A * LANE)), _layer_spec(l, (KV_RANK, H_A * V_A))],
        out_specs=[pl.BlockSpec((tm, w), lambda i: (i, 0)) for w in widths],
        out_shape=[jax.ShapeDtypeStruct((batch * past, w), BF16) for w in widths],
        compiler_params=_params(("parallel",)),
        name="cache_kv",
    )(cache["ckv"], cache["krope"], cache["diff_k"], cache["diff_v"], wts["wk"], wts["wv"])


def kernel(x_prompt, x_sample, c, cache_mla_ckv, cache_mla_krope, cache_diff_k, cache_diff_v, state_mlstm_C, state_mlstm_n, state_mlstm_m, c_ctx, w_mod, b_mod, norm1_g, w_in, mlstm_gate_b, mlstm_norm_g, mla_q_norm_g, mla_w_q_up, mla_kv_norm_g, mla_w_kv_up, diff_lambda, diff_norm_g, w_br_mlstm, w_br_mla, w_br_diff, w_out, norm2_g, w_ff1, w_ff2, final_norm_g):
    bp, sp, _ = x_prompt.shape
    bs, ss, _ = x_sample.shape
    past = cache_mla_ckv.shape[2]
    assert bs + 1 <= 8 and ss % GRID_W == 0
    assert sp % ROW_TILE == 0 and ss % ROW_TILE == 0 and past % ROW_TILE == 0

    cond8 = jnp.concatenate([c_ctx[None, :], c, jnp.zeros((8 - 1 - bs, D_MODEL), F32)], axis=0)
    mod = _modulation(cond8, w_mod, b_mod).reshape(DEPTH, 8, 1, 6 * D_MODEL)
    wts = _prep_weights(w_in, mlstm_gate_b, norm1_g, mlstm_norm_g, mla_q_norm_g, mla_w_q_up, mla_kv_norm_g,
                        mla_w_kv_up, diff_lambda, diff_norm_g, w_br_mlstm, w_br_mla, w_br_diff, w_out,
                        norm2_g, w_ff1, w_ff2)
    tables = _rope_tables(ss)
    smask = _state_mask()
    gfin = final_norm_g[None, :]
    rows = H_M * DK_M
    cache = {
        "past": past,
        "ckv": cache_mla_ckv,
        "krope": jnp.pad(cache_mla_krope, ((0, 0), (0, 0), (0, 0), (NOPE_A, LANE - NOPE_A - ROPE_A))),
        "diff_k": cache_diff_k.reshape(bs, DEPTH, past, H_D * 2 * DK_D),
        "diff_v": cache_diff_v.reshape(bs, DEPTH, past, H_D * DV_D),
        "state": (state_mlstm_C.reshape(bs, DEPTH, 2, rows, DV_M),
                  state_mlstm_n.reshape(bs, DEPTH, 2, rows, 1),
                  jnp.pad(state_mlstm_m.reshape(bs, DEPTH, 1, 2 * H_M), ((0, 0), (0, 0), (0, 0), (0, LANE - 2 * H_M)))),
    }

    y_p = x_prompt.reshape(bp * sp, D_MODEL)
    y_s = x_sample.reshape(bs * ss, D_MODEL)
    stacks = None
    states = []
    for l in range(DEPTH):
        y_p, stacks, state = _layer(y_p, l, mod, wts, smask, gfin, bp, sp, ctx_stacks=stacks)
        states.append(state)
        y_s, _, _ = _layer(y_s, l, mod, wts, smask, gfin, bs, ss, cache=cache, tables=tables)

    ckv, akr, kd, vd = stacks
    c_t = jnp.stack([s[0] for s in states], axis=1).reshape(bp, DEPTH, 2, H_M, DK_M, DV_M)
    n_t = jnp.stack([s[1] for s in states], axis=1).reshape(bp, DEPTH, 2, H_M, DK_M)
    m_t = jnp.stack([s[2][:, 0, :2 * H_M] for s in states], axis=1).reshape(bp, DEPTH, 2, H_M)
    return (y_p.reshape(bp, sp, D_MODEL), y_s.reshape(bs, ss, D_MODEL),
            ckv, akr[..., NOPE_A:NOPE_A + ROPE_A],
            kd.reshape(bp, DEPTH, sp, H_D, 2 * DK_D), vd.reshape(bp, DEPTH, sp, H_D, DV_D),
            c_t, n_t, m_t)
```
